```python
import math
import jax, jax.numpy as jnp
from jax import lax
import numpy as np

D_MODEL = 2048
BATCH = 2
SEQ = 4096
DEPTH = 1
DEC_BATCH = 128
DEC_SEQ = 8
PAST_LEN = 2048
PAGE_SIZE = 128

N_HEADS = 8
HEAD_DIM = D_MODEL // (2 * N_HEADS)
V_DIM = 2 * HEAD_DIM
QK_WIDTH = N_HEADS * 2 * HEAD_DIM
ATTN_WIDTH = N_HEADS * V_DIM
Q_BLOCK = 128
CONV_WIDTH = D_MODEL
CONV_K = 3
N_BRANCH = 2
IN_WIDTH = 2 * QK_WIDTH + ATTN_WIDTH + 3 * CONV_WIDTH + N_BRANCH * D_MODEL
N_MEM = 256
X_HEADS = 4
X_HEAD_DIM = 128
X_WIDTH = X_HEADS * X_HEAD_DIM
N_GROUPS = 4
EXPERTS_PER_GROUP = 8
N_EXPERTS = N_GROUPS * EXPERTS_PER_GROUP
TOP_K_IN_GROUP = 2
D_FF_EXPERT = 512
NORM_EPS = 1e-6

kernel_name = 'hybrid_diffattn_shortconv_hmoe_step'


def rms_norm(x, g):
    xf = x.astype(jnp.float32)
    y = xf * lax.rsqrt(jnp.mean(xf * xf, axis=-1, keepdims=True) + NORM_EPS)
    return (y * g.astype(jnp.float32)).astype(x.dtype)


def alibi_slopes():
    return jnp.exp2(-8.0 * jnp.arange(1, N_HEADS + 1, dtype=jnp.float32) / N_HEADS)


def diff_attn(q, k, v, q_pos, k_pos, lam, slopes):
    s = jnp.einsum('...qhmd,...khmd->...mhqk', q, k,
                   preferred_element_type=jnp.float32) * (HEAD_DIM ** -0.5)
    dist = (q_pos[:, None] - k_pos[None, :]).astype(jnp.float32)
    s = jnp.where(dist >= 0, s - slopes[:, None, None] * dist, -jnp.inf)
    p = jax.nn.softmax(s, axis=-1)
    a = p[..., 0, :, :, :] - lam * p[..., 1, :, :, :]
    return jnp.einsum('...hqk,...khe->...qhe', a.astype(v.dtype), v)


def prompt_attn(q, k, v, lam, slopes):
    b, s = q.shape[0], q.shape[1]
    nb = s // Q_BLOCK
    qb = q.reshape(b, nb, Q_BLOCK, N_HEADS, 2, HEAD_DIM).swapaxes(0, 1)
    k_pos = jnp.arange(s)

    def one(args):
        qi, i = args
        q_pos = i * Q_BLOCK + jnp.arange(Q_BLOCK)
        return diff_attn(qi, k, v, q_pos, k_pos, lam, slopes)

    o = lax.map(one, (qb, jnp.arange(nb)))
    return o.swapaxes(0, 1).reshape(b, s, N_HEADS, V_DIM)


def sample_attn(q, k_new, v_new, cache_k, cache_v, page_table, lam, slopes):
    n_pages = page_table.shape[1]
    past = n_pages * cache_k.shape[1]
    t = q.shape[1]
    k_pos = jnp.arange(past + t)
    q_pos = past + jnp.arange(t)

    def one(args):
        pt, qs, ks, vs = args
        kp = cache_k[pt].reshape(past, N_HEADS, 2, HEAD_DIM).astype(ks.dtype)
        vp = cache_v[pt].reshape(past, N_HEADS, V_DIM).astype(vs.dtype)
        kk = jnp.concatenate([kp, ks], axis=0)
        vv = jnp.concatenate([vp, vs], axis=0)
        return diff_attn(qs, kk, vv, q_pos, k_pos, lam, slopes)

    return lax.map(one, (page_table, q, k_new, v_new))


def mix_inputs(h, w_in, b_gate, q_g, k_g):
    lead = h.shape[:-1]
    sizes = (QK_WIDTH, QK_WIDTH, ATTN_WIDTH, CONV_WIDTH, CONV_WIDTH, CONV_WIDTH)
    splits, acc = [], 0
    for sz in sizes:
        acc += sz
        splits.append(acc)
    q, k, v, c, b, u, g = jnp.split(h @ w_in, splits, axis=-1)
    q = rms_norm(q.reshape(*lead, N_HEADS, 2, HEAD_DIM), q_g)
    k = rms_norm(k.reshape(*lead, N_HEADS, 2, HEAD_DIM), k_g)
    v = v.reshape(*lead, N_HEADS, V_DIM)
    z = c * u
    gates = jax.nn.sigmoid(g.reshape(*lead, N_BRANCH, D_MODEL) + b_gate)
    return q, k, v, z, b, gates


def causal_conv(zp, w):
    t = zp.shape[1] - (CONV_K - 1)
    y = zp[:, 0:t] * w[0]
    for j in range(1, CONV_K):
        y = y + zp[:, j:j + t] * w[j]
    return y


def mix_outputs(o_attn, y_conv, b, gates, subln_g, lam_init, w_out):
    lead = o_attn.shape[:-2]
    o_a = (rms_norm(o_attn, subln_g) * (1.0 - lam_init)).reshape(*lead, ATTN_WIDTH)
    o_c = b * y_conv
    merged = gates[..., 0, :] * o_a + gates[..., 1, :] * o_c
    return merged @ w_out


def mem_kv(mem, g_mem, w_xkv, xk_g):
    mn = rms_norm(mem, g_mem)
    kv = (mn @ w_xkv).reshape(*mem.shape[:-1], 2, X_HEADS, X_HEAD_DIM)
    return rms_norm(kv[..., 0, :, :], xk_g), kv[..., 1, :, :]


def mem_attn(hn, mk, mv, w_xq, xq_g, w_xo):
    lead = hn.shape[:-1]
    q = rms_norm((hn @ w_xq).reshape(*lead, X_HEADS, X_HEAD_DIM), xq_g)
    s = jnp.einsum('btgd,bmgd->bgtm', q, mk.astype(q.dtype),
                   preferred_element_type=jnp.float32) * (X_HEAD_DIM ** -0.5)
    p = jax.nn.softmax(s, axis=-1)
    o = jnp.einsum('bgtm,bmgd->btgd', p.astype(hn.dtype), mv.astype(hn.dtype))
    return o.reshape(*lead, X_WIDTH) @ w_xo


def hier_moe(h, w_group, b_group, w_er, b_er, w_gu, w_down):
    lead = h.shape[:-1]
    hf = h.reshape(-1, D_MODEL)
    n = hf.shape[0]
    g_logit = (hf @ w_group).astype(jnp.float32) + b_group.astype(jnp.float32)
    g_prob = jax.nn.softmax(g_logit, axis=-1)
    g_idx = jnp.argmax(g_logit, axis=-1)
    g_w = jnp.take_along_axis(g_prob, g_idx[:, None], axis=-1)
    e_logit = ((hf @ w_er).astype(jnp.float32) + b_er.astype(jnp.float32)).reshape(
        n, N_GROUPS, EXPERTS_PER_GROUP)
    e_logit = jnp.take_along_axis(e_logit, g_idx[:, None, None], axis=1)[:, 0]
    top_v, top_i = lax.top_k(e_logit, TOP_K_IN_GROUP)
    top_w = jax.nn.softmax(top_v, axis=-1) * g_w
    eid = g_idx[:, None] * EXPERTS_PER_GROUP + top_i
    comb = jnp.sum(jax.nn.one_hot(eid, N_EXPERTS, dtype=jnp.float32) * top_w[..., None], axis=1)
    gu = jnp.einsum('nd,edf->nef', hf, w_gu)
    act = jax.nn.silu(gu[..., :D_FF_EXPERT]) * gu[..., D_FF_EXPERT:]
    y = jnp.einsum('nef,efd->nd', act * comb[..., None].astype(act.dtype), w_down)
    return y.reshape(*lead, D_MODEL)


def setup_inputs(seed: int = 0) -> dict:
    key = jax.random.key(seed)
    ks = jax.random.split(key, 40)
    f32 = jnp.float32
    n_pages = PAST_LEN // PAGE_SIZE
    n_used = DEC_BATCH * n_pages
    n_phys = n_used + max(1, n_used // 4)
    L = DEPTH

    def nrm(k, shape, scale=1.0):
        return jax.random.normal(k, shape, f32) * scale

    def gain(k, shape):
        return 1.0 + 0.02 * jax.random.normal(k, shape, f32)

    page_table = jax.random.permutation(ks[0], n_phys)[:n_used].reshape(
        DEC_BATCH, n_pages).astype(jnp.int32)
    return {
        'x_prompt': nrm(ks[1], (BATCH, SEQ, D_MODEL)),
        'x_sample': nrm(ks[2], (DEC_BATCH, DEC_SEQ, D_MODEL)),
        'cache_attn_k': nrm(ks[3], (L, n_phys, PAGE_SIZE, N_HEADS, 2 * HEAD_DIM)),
        'cache_attn_v': nrm(ks[4], (L, n_phys, PAGE_SIZE, N_HEADS, V_DIM)),
        'cache_conv': nrm(ks[5], (L, DEC_BATCH, CONV_K - 1, CONV_WIDTH)),
        'cache_mem_k': nrm(ks[6], (L, DEC_BATCH, N_MEM, X_HEADS, X_HEAD_DIM)),
        'cache_mem_v': nrm(ks[7], (L, DEC_BATCH, N_MEM, X_HEADS, X_HEAD_DIM)),
        'page_table': page_table,
        'mem_prompt': nrm(ks[8], (BATCH, N_MEM, D_MODEL)),
        'norm_mix_g': gain(ks[9], (L, D_MODEL)),
        'w_in': nrm(ks[10], (L, D_MODEL, IN_WIDTH), D_MODEL ** -0.5),
        'b_gate': nrm(ks[11], (L, N_BRANCH, D_MODEL), 0.1),
        'q_norm_g': gain(ks[12], (L, HEAD_DIM)),
        'k_norm_g': gain(ks[13], (L, HEAD_DIM)),
        'lambda_q1': nrm(ks[14], (L, HEAD_DIM), 0.1),
        'lambda_k1': nrm(ks[15], (L, HEAD_DIM), 0.1),
        'lambda_q2': nrm(ks[16], (L, HEAD_DIM), 0.1),
        'lambda_k2': nrm(ks[17], (L, HEAD_DIM), 0.1),
        'subln_g': gain(ks[18], (L, V_DIM)),
        'conv_w': nrm(ks[19], (L, CONV_K, CONV_WIDTH), CONV_K ** -0.5),
        'w_out': nrm(ks[20], (L, D_MODEL, D_MODEL), D_MODEL ** -0.5),
        'norm_mem_g': gain(ks[21], (L, D_MODEL)),
        'norm_x_g': gain(ks[22], (L, D_MODEL)),
        'w_xq': nrm(ks[23], (L, D_MODEL, X_WIDTH), D_MODEL ** -0.5),
        'w_xkv': nrm(ks[24], (L, D_MODEL, 2 * X_WIDTH), D_MODEL ** -0.5),
        'xq_norm_g': gain(ks[25], (L, X_HEAD_DIM)),
        'xk_norm_g': gain(ks[26], (L, X_HEAD_DIM)),
        'w_xo': nrm(ks[27], (L, X_WIDTH, D_MODEL), X_WIDTH ** -0.5),
        'norm_ffn_g': gain(ks[28], (L, D_MODEL)),
        'w_group': nrm(ks[29], (L, D_MODEL, N_GROUPS), D_MODEL ** -0.5),
        'b_group': nrm(ks[30], (L, N_GROUPS), 0.01),
        'w_expert_router': nrm(ks[31], (L, D_MODEL, N_EXPERTS), D_MODEL ** -0.5),
        'b_expert_router': nrm(ks[32], (L, N_EXPERTS), 0.01),
        'w_gate_up': nrm(ks[33], (L, N_EXPERTS, D_MODEL, 2 * D_FF_EXPERT), D_MODEL ** -0.5),
        'w_down': nrm(ks[34], (L, N_EXPERTS, D_FF_EXPERT, D_MODEL), D_FF_EXPERT ** -0.5),
    }


def reference(x_prompt, x_sample, cache_attn_k, cache_attn_v, cache_conv, cache_mem_k,
              cache_mem_v, page_table, mem_prompt, norm_mix_g, w_in, b_gate, q_norm_g,
              k_norm_g, lambda_q1, lambda_k1, lambda_q2, lambda_k2, subln_g, conv_w, w_out,
              norm_mem_g, norm_x_g, w_xq, w_xkv, xq_norm_g, xk_norm_g, w_xo, norm_ffn_g,
              w_group, b_group, w_expert_router, b_expert_router, w_gate_up, w_down):
    slopes = alibi_slopes()
    xp, xs = x_prompt, x_sample
    bp, sp = xp.shape[0], xp.shape[1]
    bs, ts = xs.shape[0], xs.shape[1]
    p_k, p_v, p_conv, p_mk, p_mv, s_k, s_v, s_conv = [], [], [], [], [], [], [], []
    for l in range(DEPTH):
        lam_init = 0.8 - 0.6 * math.exp(-0.3 * l)
        f = lambda a: a.astype(jnp.float32)
        lam = (jnp.exp(jnp.sum(f(lambda_q1[l]) * f(lambda_k1[l])))
               - jnp.exp(jnp.sum(f(lambda_q2[l]) * f(lambda_k2[l]))) + lam_init)

        h = rms_norm(xp, norm_mix_g[l])
        q, k, v, z, b, g = mix_inputs(h, w_in[l], b_gate[l], q_norm_g[l], k_norm_g[l])
        o_att = prompt_attn(q, k, v, lam, slopes)
        zp = jnp.pad(z, ((0, 0), (CONV_K - 1, 0), (0, 0)))
        y_conv = causal_conv(zp, conv_w[l])
        xp = xp + mix_outputs(o_att, y_conv, b, g, subln_g[l], lam_init, w_out[l])
        p_k.append(k.reshape(bp, sp, N_HEADS, 2 * HEAD_DIM))
        p_v.append(v)
        p_conv.append(zp[:, zp.shape[1] - (CONV_K - 1):])

        h = rms_norm(xs, norm_mix_g[l])
        q, k, v, z, b, g = mix_inputs(h, w_in[l], b_gate[l], q_norm_g[l], k_norm_g[l])
        o_att = sample_attn(q, k, v, cache_attn_k[l], cache_attn_v[l], page_table, lam, slopes)
        zs = jnp.concatenate([cache_conv[l].astype(z.dtype), z], axis=1)
        y_conv = causal_conv(zs, conv_w[l])
        xs = xs + mix_outputs(o_att, y_conv, b, g, subln_g[l], lam_init, w_out[l])
        s_k.append(k.reshape(bs, ts, N_HEADS, 2 * HEAD_DIM))
        s_v.append(v)
        s_conv.append(zs[:, zs.shape[1] - (CONV_K - 1):])

        mk, mv = mem_kv(mem_prompt, norm_mem_g[l], w_xkv[l], xk_norm_g[l])
        xp = xp + mem_attn(rms_norm(xp, norm_x_g[l]), mk, mv, w_xq[l], xq_norm_g[l], w_xo[l])
        xs = xs + mem_attn(rms_norm(xs, norm_x_g[l]), cache_mem_k[l], cache_mem_v[l],
                           w_xq[l], xq_norm_g[l], w_xo[l])
        p_mk.append(mk)
        p_mv.append(mv)

        xp = xp + hier_moe(rms_norm(xp, norm_ffn_g[l]), w_group[l], b_group[l],
                           w_expert_router[l], b_expert_router[l], w_gate_up[l], w_down[l])
        xs = xs + hier_moe(rms_norm(xs, norm_ffn_g[l]), w_group[l], b_group[l],
                           w_expert_router[l], b_expert_router[l], w_gate_up[l], w_down[l])
    return (xp, xs, jnp.stack(p_k), jnp.stack(p_v), jnp.stack(p_conv), jnp.stack(p_mk),
            jnp.stack(p_mv), jnp.stack(s_k), jnp.stack(s_v), jnp.stack(s_conv))
```

```python
import functools
import math

import jax
import jax.numpy as jnp
from jax import lax
from jax.experimental import pallas as pl
from jax.experimental.pallas import tpu as pltpu

F32 = jnp.float32
BF16 = jnp.bfloat16

D_MODEL = 2048
N_HEADS = 8
HEAD_DIM = 128
V_DIM = 256
HEAD_W = 2 * HEAD_DIM
PAGE = 128
X_HEADS = 4
X_HEAD_DIM = 128
X_WIDTH = X_HEADS * X_HEAD_DIM
N_MEM = 256
N_GROUPS = 4
EXPERTS_PER_GROUP = 8
N_EXPERTS = N_GROUPS * EXPERTS_PER_GROUP
D_FF = 512
EPS = 1e-6
NEG = -1e30
LANES = 128
MIB = 1024 * 1024


def _cparams(sem, vmem_mib):
    return pltpu.CompilerParams(dimension_semantics=sem, vmem_limit_bytes=vmem_mib * MIB)


def _rms(x, g):
    ms = jnp.mean(x * x, axis=-1, keepdims=True)
    return x * lax.rsqrt(ms + EPS) * g


def _headnorm(acc, g, scale=None):
    outs = []
    for c in range(acc.shape[1] // HEAD_DIM):
        y = _rms(acc[:, c * HEAD_DIM:(c + 1) * HEAD_DIM], g)
        outs.append(y if scale is None else y * scale)
    return jnp.concatenate(outs, axis=1)


def _norm_kernel(x_ref, g_ref, o_ref):
    o_ref[...] = _rms(x_ref[...], g_ref[...]).astype(o_ref.dtype)


def _rmsnorm_cast(x2d, g, tm):
    t, d = x2d.shape
    return pl.pallas_call(
        _norm_kernel,
        out_shape=jax.ShapeDtypeStruct((t, d), BF16),
        grid=(t // tm,),
        in_specs=[pl.BlockSpec((tm, d), lambda i: (i, 0)),
                  pl.BlockSpec((1, d), lambda i: (0, 0))],
        out_specs=pl.BlockSpec((tm, d), lambda i: (i, 0)),
        compiler_params=_cparams(("parallel",), 32),
        name="rmsnorm_cast",
    )(x2d, g.reshape(1, d))


def _inproj_kernel(h_ref, wq, wk, wv, wc, wb, wu, wga, wgb, qg_ref, kg_ref, bga_ref, bgb_ref,
                   q_o, kf_o, kb_o, vf_o, vb_o, z_o, b_o, ga_o, gb_o, *, q_scale):
    h = h_ref[...]

    def mm(w):
        return jnp.dot(h, w[...], preferred_element_type=F32)

    q_o[...] = _headnorm(mm(wq), qg_ref[...], q_scale).astype(q_o.dtype)
    k = _headnorm(mm(wk), kg_ref[...])
    kf_o[...] = k
    kb_o[...] = k.astype(BF16)
    v = mm(wv)
    vf_o[...] = v
    vb_o[...] = v.astype(BF16)
    z_o[...] = mm(wc) * mm(wu)
    b_o[...] = mm(wb)
    ga_o[...] = jax.nn.sigmoid(mm(wga) + bga_ref[...])
    gb_o[...] = jax.nn.sigmoid(mm(wgb) + bgb_ref[...])


def _inproj(hb, w_in_bf, q_g, k_g, b_gate, q_dtype, tm, tn):
    t = hb.shape[0]
    nb = D_MODEL // tn
    region = {"q": 0, "k": 1, "v": 2, "c": 3, "b": 4, "u": 5, "ga": 6, "gb": 7}

    def wspec(r):
        return pl.BlockSpec((D_MODEL, tn), lambda i, j, r=r: (0, r * nb + j))

    tile = pl.BlockSpec((tm, tn), lambda i, j: (i, j))
    vec = pl.BlockSpec((1, HEAD_DIM), lambda i, j: (0, 0))
    bias = pl.BlockSpec((1, tn), lambda i, j: (0, j))
    f32o = jax.ShapeDtypeStruct((t, D_MODEL), F32)
    bfo = jax.ShapeDtypeStruct((t, D_MODEL), BF16)
    return pl.pallas_call(
        functools.partial(_inproj_kernel, q_scale=HEAD_DIM ** -0.5),
        out_shape=(jax.ShapeDtypeStruct((t, D_MODEL), q_dtype), f32o, bfo, f32o, bfo, f32o, f32o, f32o, f32o),
        grid=(t // tm, nb),
        in_specs=[pl.BlockSpec((tm, D_MODEL), lambda i, j: (i, 0))]
        + [wspec(region[n]) for n in ("q", "k", "v", "c", "b", "u", "ga", "gb")]
        + [vec, vec, bias, bias],
        out_specs=(tile,) * 9,
        compiler_params=_cparams(("parallel", "arbitrary"), 48),
        name="inproj",
    )(hb, *([w_in_bf] * 8), q_g.reshape(1, HEAD_DIM), k_g.reshape(1, HEAD_DIM),
      b_gate[0:1], b_gate[1:2])


def _softmax_step(s, v, m_scr, l_scr, acc_scr, idx):
    m_prev = m_scr[idx]
    m_next = jnp.maximum(m_prev, jnp.max(s, axis=1, keepdims=True))
    p = jnp.exp(s - m_next[:, :1])
    alpha = jnp.exp(m_prev - m_next)
    l_scr[idx] = alpha * l_scr[idx] + jnp.sum(p, axis=1, keepdims=True)
    m_scr[idx] = m_next
    pv = jnp.dot(p.astype(BF16), v, preferred_element_type=F32)
    acc_scr[idx] = acc_scr[idx] * alpha[:, :1] + pv


def _diff_out(a1, l1, a2, l2, lam, g, out_scale):
    o = a1 * (1.0 / l1) - lam * (a2 * (1.0 / l2))
    return _rms(o, g) * out_scale


def _pattn_kernel(lam_ref, slopes_ref, q_ref, k_ref, v_ref, g_ref, o_ref, m_scr, l_scr, acc_scr,
                  *, t, out_scale):
    h = pl.program_id(1)
    qi = pl.program_id(2)
    ki = pl.program_id(3)

    @pl.when(ki == 0)
    def _():
        m_scr[...] = jnp.full(m_scr.shape, NEG, F32)
        l_scr[...] = jnp.zeros(l_scr.shape, F32)
        acc_scr[...] = jnp.zeros(acc_scr.shape, F32)

    def body(masked):
        q = q_ref[0]
        k = k_ref[0]
        v = v_ref[0]
        kpos = ((ki - qi) * t + lax.broadcasted_iota(jnp.int32, (1, t), 1)).astype(F32)
        bias = slopes_ref[h] * kpos
        if masked:
            keep = (lax.broadcasted_iota(jnp.int32, (t, t), 1)
                    <= lax.broadcasted_iota(jnp.int32, (t, t), 0))
        for m in range(2):
            s = lax.dot_general(q[:, m * HEAD_DIM:(m + 1) * HEAD_DIM], k[:, m * HEAD_DIM:(m + 1) * HEAD_DIM],
                                (((1,), (1,)), ((), ())), preferred_element_type=F32)
            s = s + bias
            if masked:
                s = jnp.where(keep, s, NEG)
            _softmax_step(s, v, m_scr, l_scr, acc_scr, m)

    @pl.when(ki < qi)
    def _():
        body(False)

    @pl.when(ki == qi)
    def _():
        body(True)
        o_ref[0] = _diff_out(acc_scr[0], l_scr[0][:, :1], acc_scr[1], l_scr[1][:, :1],
                             lam_ref[0], g_ref[...], out_scale)


def _prompt_attn(qb, kb, vb, lam, slopes, subln_g, out_scale, t):
    b, s, _ = qb.shape
    n = s // t
    smem = pl.BlockSpec(memory_space=pltpu.SMEM)
    return pl.pallas_call(
        functools.partial(_pattn_kernel, t=t, out_scale=out_scale),
        out_shape=jax.ShapeDtypeStruct((b, s, D_MODEL), F32),
        grid=(b, N_HEADS, n, n),
        in_specs=[smem, smem,
                  pl.BlockSpec((1, t, HEAD_W), lambda bi, h, qi, ki: (bi, qi, h)),
                  pl.BlockSpec((1, t, HEAD_W), lambda bi, h, qi, ki: (bi, jnp.minimum(ki, qi), h)),
                  pl.BlockSpec((1, t, HEAD_W), lambda bi, h, qi, ki: (bi, jnp.minimum(ki, qi), h)),
                  pl.BlockSpec((1, V_DIM), lambda bi, h, qi, ki: (0, 0))],
        out_specs=pl.BlockSpec((1, t, HEAD_W), lambda bi, h, qi, ki: (bi, qi, h)),
        scratch_shapes=[pltpu.VMEM((2, t, LANES), F32), pltpu.VMEM((2, t, LANES), F32),
                        pltpu.VMEM((2, t, V_DIM), F32)],
        compiler_params=_cparams(("parallel", "parallel", "parallel", "arbitrary"), 40),
        name="prompt_attn",
    )(lam, slopes, qb, kb, vb, subln_g.reshape(1, V_DIM))


def _page_head(lo_ref, hi_ref, h):
    rows = pl.ds(h, PAGE, stride=N_HEADS)
    return jnp.concatenate([lo_ref[0, rows, :], hi_ref[0, rows, :]], axis=1).astype(BF16)


def _sattn_kernel(pt_ref, lam_ref, slopes_ref, q_ref, kn_ref, vn_ref, *rest, npg, n_steps, past, out_scale):
    del pt_ref
    k_refs = rest[:2 * npg]
    v_refs = rest[2 * npg:4 * npg]
    g_ref, o_ref, qs_scr, m_scr, l_scr, acc_scr = rest[4 * npg:]
    p = pl.program_id(1)
    tq = q_ref.shape[1]

    @pl.when(p == 0)
    def _():
        m_scr[...] = jnp.full(m_scr.shape, NEG, F32)
        l_scr[...] = jnp.zeros(l_scr.shape, F32)
        acc_scr[...] = jnp.zeros(acc_scr.shape, F32)
        q = q_ref[0]
        zero = jnp.zeros((tq, HEAD_DIM), F32)
        for h in range(N_HEADS):
            q1 = q[:, h * HEAD_W:h * HEAD_W + HEAD_DIM]
            q2 = q[:, h * HEAD_W + HEAD_DIM:(h + 1) * HEAD_W]
            qs_scr[h] = jnp.concatenate([jnp.concatenate([q1, zero], axis=1),
                                         jnp.concatenate([zero, q2], axis=1)], axis=0).astype(BF16)

    lane = lax.broadcasted_iota(jnp.int32, (1, PAGE), 1)

    def update(h, s_list, v_list):
        m_prev = m_scr[h]
        s_max = functools.reduce(jnp.maximum, s_list)
        m_next = jnp.maximum(m_prev, jnp.max(s_max, axis=1, keepdims=True))
        alpha = jnp.exp(m_prev - m_next)
        p_sum = None
        pv = None
        for s, v in zip(s_list, v_list):
            pj = jnp.exp(s - m_next)
            p_sum = pj if p_sum is None else p_sum + pj
            d = jnp.dot(pj.astype(BF16), v, preferred_element_type=F32)
            pv = d if pv is None else pv + d
        l_scr[h] = alpha * l_scr[h] + jnp.sum(p_sum, axis=1, keepdims=True)
        m_scr[h] = m_next
        acc_scr[h] = acc_scr[h] * alpha[:, :1] + pv

    for h in range(N_HEADS):
        qs = qs_scr[h]
        slope = slopes_ref[h]
        s_list, v_list = [], []
        for j in range(npg):
            kh = _page_head(k_refs[2 * j], k_refs[2 * j + 1], h)
            s = lax.dot_general(qs, kh, (((1,), (1,)), ((), ())), preferred_element_type=F32)
            kpos = ((p * npg + j) * PAGE - past + lane).astype(F32)
            s_list.append(s + slope * kpos)
            v_list.append(_page_head(v_refs[2 * j], v_refs[2 * j + 1], h))
        update(h, s_list, v_list)

    @pl.when(p == n_steps - 1)
    def _():
        kn = kn_ref[0]
        vn = vn_ref[0]
        pad = jnp.zeros((PAGE - tq, HEAD_W), F32)
        row = lax.broadcasted_iota(jnp.int32, (2 * tq, PAGE), 0) % tq
        col = lax.broadcasted_iota(jnp.int32, (2 * tq, PAGE), 1)
        keep = col <= row
        lam = lam_ref[0]
        for h in range(N_HEADS):
            kh = jnp.concatenate([kn[:, h * HEAD_W:(h + 1) * HEAD_W], pad], axis=0).astype(BF16)
            vh = jnp.concatenate([vn[:, h * HEAD_W:(h + 1) * HEAD_W], pad], axis=0).astype(BF16)
            s = lax.dot_general(qs_scr[h], kh, (((1,), (1,)), ((), ())), preferred_element_type=F32)
            s = jnp.where(keep, s + slopes_ref[h] * lane.astype(F32), NEG)
            update(h, [s], [vh])
            acc = acc_scr[h]
            l = l_scr[h]
            o_ref[0, :, h * HEAD_W:(h + 1) * HEAD_W] = _diff_out(
                acc[:tq], l[:tq, :1], acc[tq:], l[tq:, :1], lam, g_ref[...], out_scale)


def _sample_attn(qs, kn, vn, pool_k, pool_v, page_table, lam, slopes, subln_g, out_scale, npg):
    db, tq, _ = qs.shape
    n_pages = page_table.shape[1]
    n_steps = n_pages // npg
    n_phys = pool_k.shape[0]
    pk = pool_k.reshape(n_phys, PAGE * N_HEADS, HEAD_W)
    pv = pool_v.reshape(n_phys, PAGE * N_HEADS, V_DIM)
    smem = pl.BlockSpec(memory_space=pltpu.SMEM)
    tok = pl.BlockSpec((1, tq, D_MODEL), lambda b, p, pt: (b, 0, 0))

    def half_page(j, c):
        return pl.BlockSpec((1, PAGE * N_HEADS, LANES), lambda b, p, pt, j=j, c=c: (pt[b, p * npg + j], 0, c))

    pages = [half_page(j, c) for j in range(npg) for c in range(2)]
    grid_spec = pltpu.PrefetchScalarGridSpec(
        num_scalar_prefetch=1,
        grid=(db, n_steps),
        in_specs=[smem, smem, tok, tok, tok] + pages * 2
        + [pl.BlockSpec((1, V_DIM), lambda b, p, pt: (0, 0))],
        out_specs=tok,
        scratch_shapes=[pltpu.VMEM((N_HEADS, 2 * tq, HEAD_W), BF16),
                        pltpu.VMEM((N_HEADS, 2 * tq, LANES), F32),
                        pltpu.VMEM((N_HEADS, 2 * tq, LANES), F32),
                        pltpu.VMEM((N_HEADS, 2 * tq, V_DIM), F32)],
    )
    return pl.pallas_call(
        functools.partial(_sattn_kernel, npg=npg, n_steps=n_steps, past=n_pages * PAGE, out_scale=out_scale),
        out_shape=jax.ShapeDtypeStruct((db, tq, D_MODEL), F32),
        grid_spec=grid_spec,
        compiler_params=_cparams(("parallel", "arbitrary"), 48),
        name="sample_attn",
    )(page_table, lam, slopes, qs, kn, vn, *([pk] * (2 * npg)), *([pv] * (2 * npg)), subln_g.reshape(1, V_DIM))


def _conv_merge(o, ga, gb, b, z, z1, z2, cw):
    y = z2 * cw[0] + z1 * cw[1] + z * cw[2]
    return (ga * o + gb * (b * y)).astype(BF16)


def _mix_prompt_kernel(o_ref, ga_ref, gb_ref, b_ref, z_ref, zp_ref, cw_ref, out_ref, *, tiles_per_seq):
    i = pl.program_id(0)
    z = z_ref[...]
    tm = z.shape[0]
    zp = jnp.where(i % tiles_per_seq == 0, 0.0, zp_ref[...])
    pm1 = zp[7:8, :]
    pm2 = zp[6:7, :]
    row = lax.broadcasted_iota(jnp.int32, (tm, 1), 0)
    z1 = jnp.where(row == 0, pm1, pltpu.roll(z, 1, 0))
    z2 = jnp.where(row == 0, pm2, jnp.where(row == 1, pm1, pltpu.roll(z, 2, 0)))
    cw = cw_ref[...]
    out_ref[...] = _conv_merge(o_ref[...], ga_ref[...], gb_ref[...], b_ref[...], z, z1, z2,
                               (cw[0:1], cw[1:2], cw[2:3]))


def _mix_prompt(o, ga, gb, b, z, conv_w, seq, tm):
    t = z.shape[0]
    tile = pl.BlockSpec((tm, D_MODEL), lambda i: (i, 0))
    prev = pl.BlockSpec((8, D_MODEL), lambda i: (jnp.maximum(i * (tm // 8) - 1, 0), 0))
    return pl.pallas_call(
        functools.partial(_mix_prompt_kernel, tiles_per_seq=seq // tm),
        out_shape=jax.ShapeDtypeStruct((t, D_MODEL), BF16),
        grid=(t // tm,),
        in_specs=[tile, tile, tile, tile, tile, prev, pl.BlockSpec((3, D_MODEL), lambda i: (0, 0))],
        out_specs=tile,
        compiler_params=_cparams(("parallel",), 40),
        name="mix_prompt",
    )(o, ga, gb, b, z, z, conv_w)


def _mix_sample_kernel(o_ref, ga_ref, gb_ref, b_ref, z_ref, cc_ref, cw_ref, out_ref):
    z = z_ref[...]
    cc = cc_ref[...]
    c0 = cc[:, 0:1, :]
    c1 = cc[:, 1:2, :]
    row = lax.broadcasted_iota(jnp.int32, (1, z.shape[1], 1), 1)
    z1 = jnp.where(row == 0, c1, pltpu.roll(z, 1, 1))
    z2 = jnp.where(row == 0, c0, jnp.where(row == 1, c1, pltpu.roll(z, 2, 1)))
    cw = cw_ref[...]
    out_ref[...] = _conv_merge(o_ref[...], ga_ref[...], gb_ref[...], b_ref[...], z, z1, z2,
                               (cw[0:1][None], cw[1:2][None], cw[2:3][None]))


def _mix_sample(o, ga, gb, b, z, cache_conv, conv_w, g):
    db, ts, _ = z.shape
    tile = pl.BlockSpec((g, ts, D_MODEL), lambda i: (i, 0, 0))
    return pl.pallas_call(
        _mix_sample_kernel,
        out_shape=jax.ShapeDtypeStruct((db, ts, D_MODEL), BF16),
        grid=(db // g,),
        in_specs=[tile, tile, tile, tile, tile,
                  pl.BlockSpec((g, cache_conv.shape[1], D_MODEL), lambda i: (i, 0, 0)),
                  pl.BlockSpec((3, D_MODEL), lambda i: (0, 0))],
        out_specs=tile,
        compiler_params=_cparams(("parallel",), 40),
        name="mix_sample",
    )(o, ga, gb, b, z, cache_conv, conv_w)


def _mm_res_kernel(a_ref, w_ref, r_ref, o_ref):
    o_ref[...] = r_ref[...] + jnp.dot(a_ref[...], w_ref[...], preferred_element_type=F32)


def _mm_res(a, w, r, tm, tn):
    t, k = a.shape
    n = w.shape[1]
    return pl.pallas_call(
        _mm_res_kernel,
        out_shape=jax.ShapeDtypeStruct((t, n), F32),
        grid=(t // tm, n // tn),
        in_specs=[pl.BlockSpec((tm, k), lambda i, j: (i, 0)),
                  pl.BlockSpec((k, tn), lambda i, j: (0, j)),
                  pl.BlockSpec((tm, tn), lambda i, j: (i, j))],
        out_specs=pl.BlockSpec((tm, tn), lambda i, j: (i, j)),
        compiler_params=_cparams(("parallel", "parallel"), 48),
        name="outproj",
    )(a, w, r)


def _memkv_kernel(h_ref, wk_ref, wv_ref, g_ref, kf_o, kb_o, vf_o, vb_o):
    h = h_ref[...]
    k = _headnorm(jnp.dot(h, wk_ref[...], preferred_element_type=F32), g_ref[...])
    kf_o[...] = k
    kb_o[...] = k.astype(BF16)
    v = jnp.dot(h, wv_ref[...], preferred_element_type=F32)
    vf_o[...] = v
    vb_o[...] = v.astype(BF16)


def _mem_kv(mem_nb, w_xkv_bf, xk_g):
    t = mem_nb.shape[0]
    f32o = jax.ShapeDtypeStruct((t, X_WIDTH), F32)
    bfo = jax.ShapeDtypeStruct((t, X_WIDTH), BF16)
    full = lambda s: pl.BlockSpec(s, lambda i: (0,) * len(s))
    return pl.pallas_call(
        _memkv_kernel,
        out_shape=(f32o, bfo, f32o, bfo),
        grid=(1,),
        in_specs=[full((t, D_MODEL)),
                  pl.BlockSpec((D_MODEL, X_WIDTH), lambda i: (0, 0)),
                  pl.BlockSpec((D_MODEL, X_WIDTH), lambda i: (0, 1)),
                  full((1, X_HEAD_DIM))],
        out_specs=(full((t, X_WIDTH)),) * 4,
        compiler_params=_cparams(("arbitrary",), 32),
        name="mem_kv",
    )(mem_nb, w_xkv_bf, w_xkv_bf, xk_g.reshape(1, X_HEAD_DIM))


def _xattn_head(qh, kh, vh):
    s = lax.dot_general(qh, kh, (((1,), (1,)), ((), ())), preferred_element_type=F32)
    e = jnp.exp(s - jnp.max(s, axis=1, keepdims=True))
    p = e * (1.0 / jnp.sum(e, axis=1, keepdims=True))
    return jnp.dot(p.astype(BF16), vh, preferred_element_type=F32)


def _xattn_q(x, g_ref, wq_ref, qg_ref):
    hn = _rms(x, g_ref[...]).astype(BF16)
    q = jnp.dot(hn, wq_ref[...], preferred_element_type=F32)
    return _headnorm(q, qg_ref[...], X_HEAD_DIM ** -0.5).astype(BF16)


def _xattn_prompt_kernel(x_ref, g_ref, wq_ref, qg_ref, mk_ref, mv_ref, wo_ref, o_ref):
    x = x_ref[0]
    q = _xattn_q(x, g_ref, wq_ref, qg_ref)
    mk = mk_ref[0]
    mv = mv_ref[0]
    heads = []
    for g in range(X_HEADS):
        sl = slice(g * X_HEAD_DIM, (g + 1) * X_HEAD_DIM)
        heads.append(_xattn_head(q[:, sl], mk[:, sl], mv[:, sl]))
    o = jnp.concatenate(heads, axis=1).astype(BF16)
    o_ref[0] = x + jnp.dot(o, wo_ref[...], preferred_element_type=F32)


def _xattn_prompt(x, g, w_xq_bf, xq_g, mk_b, mv_b, w_xo_bf, tm):
    b, s, _ = x.shape
    tile = pl.BlockSpec((1, tm, D_MODEL), lambda bi, i: (bi, i, 0))
    mem = pl.BlockSpec((1, N_MEM, X_WIDTH), lambda bi, i: (bi, 0, 0))
    const = lambda shp: pl.BlockSpec(shp, lambda bi, i: (0, 0))
    return pl.pallas_call(
        _xattn_prompt_kernel,
        out_shape=jax.ShapeDtypeStruct((b, s, D_MODEL), F32),
        grid=(b, s // tm),
        in_specs=[tile, const((1, D_MODEL)), const((D_MODEL, X_WIDTH)), const((1, X_HEAD_DIM)),
                  mem, mem, const((X_WIDTH, D_MODEL))],
        out_specs=tile,
        compiler_params=_cparams(("parallel", "parallel"), 48),
        name="xattn_prompt",
    )(x, g.reshape(1, D_MODEL), w_xq_bf, xq_g.reshape(1, X_HEAD_DIM), mk_b, mv_b, w_xo_bf)


def _xattn_sample_kernel(x_ref, g_ref, wq_ref, qg_ref, mk_ref, mv_ref, wo_ref, o_ref, q_scr, a_scr, *, ts):
    x = x_ref[...]
    q_scr[...] = _xattn_q(x, g_ref, wq_ref, qg_ref)
    for bi in range(mk_ref.shape[0]):
        rows = slice(bi * ts, (bi + 1) * ts)
        for g in range(X_HEADS):
            sl = slice(g * X_HEAD_DIM, (g + 1) * X_HEAD_DIM)
            mem_rows = pl.ds(g, N_MEM, stride=X_HEADS)
            a_scr[rows, sl] = _xattn_head(q_scr[rows, sl], mk_ref[bi, mem_rows, :].astype(BF16),
                                          mv_ref[bi, mem_rows, :].astype(BF16))
    o_ref[...] = x + jnp.dot(a_scr[...].astype(BF16), wo_ref[...], preferred_element_type=F32)


def _xattn_sample(x2d, g, w_xq_bf, xq_g, cache_mk, cache_mv, w_xo_bf, ts, gb):
    t = x2d.shape[0]
    db = cache_mk.shape[0]
    mk = cache_mk.reshape(db, N_MEM * X_HEADS, X_HEAD_DIM)
    mv = cache_mv.reshape(db, N_MEM * X_HEADS, X_HEAD_DIM)
    tile = pl.BlockSpec((gb * ts, D_MODEL), lambda i: (i, 0))
    mem = pl.BlockSpec((gb, N_MEM * X_HEADS, X_HEAD_DIM), lambda i: (i, 0, 0))
    const = lambda shp: pl.BlockSpec(shp, lambda i: (0, 0))
    return pl.pallas_call(
        functools.partial(_xattn_sample_kernel, ts=ts),
        out_shape=jax.ShapeDtypeStruct((t, D_MODEL), F32),
        grid=(db // gb,),
        in_specs=[tile, const((1, D_MODEL)), const((D_MODEL, X_WIDTH)), const((1, X_HEAD_DIM)),
                  mem, mem, const((X_WIDTH, D_MODEL))],
        out_specs=tile,
        scratch_shapes=[pltpu.VMEM((gb * ts, X_WIDTH), BF16), pltpu.VMEM((gb * ts, X_WIDTH), F32)],
        compiler_params=_cparams(("parallel",), 48),
        name="xattn_sample",
    )(x2d, g.reshape(1, D_MODEL), w_xq_bf, xq_g.reshape(1, X_HEAD_DIM), mk, mv, w_xo_bf)


def _first_argmax(v, lane):
    vmax = jnp.max(v, axis=1, keepdims=True)
    idx = jnp.min(jnp.where(v == vmax, lane, LANES), axis=1, keepdims=True)
    return vmax, idx


def _router_kernel(xp_ref, xs_ref, g_ref, wr_ref, br_ref, h_o, eid_o, w_o, *, n_prompt_tiles):
    i = pl.program_id(0)

    def body(x_ref):
        h = _rms(x_ref[...], g_ref[...])
        h_o[...] = h
        logits = jnp.dot(h, wr_ref[...], preferred_element_type=F32,
                         precision=lax.Precision.HIGHEST) + br_ref[...]
        lane = lax.broadcasted_iota(jnp.int32, logits.shape, 1)
        ninf = -jnp.inf
        gl = jnp.where(lane < N_GROUPS, logits, ninf)
        gmax, gidx = _first_argmax(gl, lane)
        g_w = 1.0 / jnp.sum(jnp.exp(gl - gmax), axis=1, keepdims=True)
        lo = N_GROUPS + gidx * EXPERTS_PER_GROUP
        el = jnp.where((lane >= lo) & (lane < lo + EXPERTS_PER_GROUP), logits, ninf)
        v1, i1 = _first_argmax(el, lane)
        v2, i2 = _first_argmax(jnp.where(lane == i1, ninf, el), lane)
        e2 = jnp.exp(v2 - v1)
        w1 = g_w / (1.0 + e2)
        w2 = g_w * e2 / (1.0 + e2)
        eid_o[...] = jnp.where(lane == 0, i1 - N_GROUPS, jnp.where(lane == 1, i2 - N_GROUPS, 0))
        w_o[...] = jnp.where(lane == 0, w1, jnp.where(lane == 1, w2, 0.0))

    @pl.when(i < n_prompt_tiles)
    def _():
        body(xp_ref)

    @pl.when(i >= n_prompt_tiles)
    def _():
        body(xs_ref)


def _router(xp2d, xs2d, g, w_r, b_r, tm):
    tp, ts = xp2d.shape[0], xs2d.shape[0]
    npt, nst = tp // tm, ts // tm
    t = tp + ts
    tile = pl.BlockSpec((tm, D_MODEL), lambda i: (i, 0))
    const = lambda shp: pl.BlockSpec(shp, lambda i: (0, 0))
    return pl.pallas_call(
        functools.partial(_router_kernel, n_prompt_tiles=npt),
        out_shape=(jax.ShapeDtypeStruct((t, D_MODEL), F32),
                   jax.ShapeDtypeStruct((t, LANES), jnp.int32),
                   jax.ShapeDtypeStruct((t, LANES), F32)),
        grid=(npt + nst,),
        in_specs=[pl.BlockSpec((tm, D_MODEL), lambda i: (jnp.minimum(i, npt - 1), 0)),
                  pl.BlockSpec((tm, D_MODEL), lambda i: (jnp.maximum(i - npt, 0), 0)),
                  const((1, D_MODEL)), const((D_MODEL, LANES)), const((1, LANES))],
        out_specs=(tile, pl.BlockSpec((tm, LANES), lambda i: (i, 0)), pl.BlockSpec((tm, LANES), lambda i: (i, 0))),
        compiler_params=_cparams(("parallel",), 40),
        name="moe_router",
    )(xp2d, xs2d, g.reshape(1, D_MODEL), w_r, b_r)


def _gather_kernel(idx_ref, src_hbm, o_ref, sem, *, tm):
    base = pl.program_id(0) * tm

    def start(r, carry):
        pltpu.make_async_copy(src_hbm.at[pl.ds(idx_ref[base + r], 1)], o_ref.at[pl.ds(r, 1)], sem).start()
        return carry

    lax.fori_loop(0, tm, start, 0)
    pltpu.make_async_copy(src_hbm.at[pl.ds(0, tm)], o_ref, sem).wait()


def _gather_rows(src, idx, n_tiles, tm):
    d = src.shape[1]
    grid_spec = pltpu.PrefetchScalarGridSpec(
        num_scalar_prefetch=1,
        grid=(n_tiles,),
        in_specs=[pl.BlockSpec(memory_space=pl.ANY)],
        out_specs=pl.BlockSpec((tm, d), lambda i, idx: (i, 0)),
        scratch_shapes=[pltpu.SemaphoreType.DMA(())],
    )
    return pl.pallas_call(
        functools.partial(_gather_kernel, tm=tm),
        out_shape=jax.ShapeDtypeStruct((n_tiles * tm, d), src.dtype),
        grid_spec=grid_spec,
        compiler_params=_cparams(("arbitrary",), 32),
        name="moe_gather",
    )(idx, src)


def _ffn_kernel(te_ref, na_ref, x_ref, rw_ref, wgu_ref, wd_ref, o_ref, wgu_scr, wd_scr):
    t = pl.program_id(0)

    @pl.when(t < na_ref[0])
    def _():
        e = te_ref[t]
        e_prev = te_ref[jnp.maximum(t - 1, 0)]

        @pl.when((t == 0) | (e != e_prev))
        def _():
            wgu_scr[...] = wgu_ref[0].astype(BF16)
            wd_scr[...] = wd_ref[0].astype(BF16)

        gu = jnp.dot(x_ref[...].astype(BF16), wgu_scr[...], preferred_element_type=F32)
        gate = gu[:, :D_FF]
        act = (gate * jax.nn.sigmoid(gate)) * gu[:, D_FF:] * rw_ref[...]
        o_ref[...] = jnp.dot(act.astype(BF16), wd_scr[...], preferred_element_type=F32)

    @pl.when(t >= na_ref[0])
    def _():
        o_ref[...] = jnp.zeros(o_ref.shape, F32)


def _expert_ffn(xs, row_w, tile_e, n_active, w_gu, w_down, tm):
    n_tiles = xs.shape[0] // tm
    grid_spec = pltpu.PrefetchScalarGridSpec(
        num_scalar_prefetch=2,
        grid=(n_tiles,),
        in_specs=[pl.BlockSpec((tm, D_MODEL), lambda t, te, na: (t, 0)),
                  pl.BlockSpec((tm, 1), lambda t, te, na: (t, 0)),
                  pl.BlockSpec((1, D_MODEL, 2 * D_FF), lambda t, te, na: (te[t], 0, 0)),
                  pl.BlockSpec((1, D_FF, D_MODEL), lambda t, te, na: (te[t], 0, 0))],
        out_specs=pl.BlockSpec((tm, D_MODEL), lambda t, te, na: (t, 0)),
        scratch_shapes=[pltpu.VMEM((D_MODEL, 2 * D_FF), BF16), pltpu.VMEM((D_FF, D_MODEL), BF16)],
    )
    return pl.pallas_call(
        _ffn_kernel,
        out_shape=jax.ShapeDtypeStruct((n_tiles * tm, D_MODEL), F32),
        grid_spec=grid_spec,
        compiler_params=_cparams(("arbitrary",), 48),
        name="moe_ffn",
    )(tile_e, n_active, xs, row_w, w_gu, w_down)


def _combine_kernel(da_ref, db_ref, x_ref, ys_hbm, o_ref, buf_a, buf_b, sem, *, tm, base_tok):
    base = base_tok + pl.program_id(0) * tm

    def start(r, carry):
        pltpu.make_async_copy(ys_hbm.at[pl.ds(da_ref[base + r], 1)], buf_a.at[pl.ds(r, 1)], sem.at[0]).start()
        pltpu.make_async_copy(ys_hbm.at[pl.ds(db_ref[base + r], 1)], buf_b.at[pl.ds(r, 1)], sem.at[1]).start()
        return carry

    lax.fori_loop(0, tm, start, 0)
    pltpu.make_async_copy(ys_hbm.at[pl.ds(0, tm)], buf_a, sem.at[0]).wait()
    pltpu.make_async_copy(ys_hbm.at[pl.ds(0, tm)], buf_b, sem.at[1]).wait()
    o_ref[...] = x_ref[...] + (buf_a[...] + buf_b[...])


def _combine(x2d, ys, dest_a, dest_b, base_tok, tm):
    t, d = x2d.shape
    grid_spec = pltpu.PrefetchScalarGridSpec(
        num_scalar_prefetch=2,
        grid=(t // tm,),
        in_specs=[pl.BlockSpec((tm, d), lambda i, da, db: (i, 0)),
                  pl.BlockSpec(memory_space=pl.ANY)],
        out_specs=pl.BlockSpec((tm, d), lambda i, da, db: (i, 0)),
        scratch_shapes=[pltpu.VMEM((tm, d), F32), pltpu.VMEM((tm, d), F32), pltpu.SemaphoreType.DMA((2,))],
    )
    return pl.pallas_call(
        functools.partial(_combine_kernel, tm=tm, base_tok=base_tok),
        out_shape=jax.ShapeDtypeStruct((t, d), F32),
        grid_spec=grid_spec,
        compiler_params=_cparams(("arbitrary",), 32),
        name="moe_combine",
    )(dest_a, dest_b, x2d, ys)


def _route_plan(eid, wts, tm, n_tiles):
    n_entries = eid.size
    flat_e = eid.reshape(-1)
    onehot = (flat_e[:, None] == jnp.arange(N_EXPERTS, dtype=jnp.int32)[None, :]).astype(jnp.int32)
    csum = jnp.cumsum(onehot, axis=0)
    rank = jnp.take_along_axis(csum, flat_e[:, None], axis=1)[:, 0] - 1
    counts = csum[-1]
    padded = ((counts + tm - 1) // tm) * tm
    ends = jnp.cumsum(padded)
    dest = (ends - padded)[flat_e] + rank
    token = jnp.arange(n_entries, dtype=jnp.int32) // eid.shape[1]
    src_row = jnp.zeros((n_tiles * tm,), jnp.int32).at[dest].set(token)
    row_w = jnp.zeros((n_tiles * tm,), F32).at[dest].set(wts.reshape(-1))
    n_active = (ends[-1] // tm).astype(jnp.int32)
    tile_e = jnp.searchsorted(ends, jnp.arange(n_tiles, dtype=jnp.int32) * tm, side="right").astype(jnp.int32)
    last_e = tile_e[jnp.maximum(n_active - 1, 0)]
    tile_e = jnp.where(jnp.arange(n_tiles) < n_active, tile_e, last_e)
    return src_row, row_w.reshape(-1, 1), tile_e, n_active.reshape(1), dest.reshape(eid.shape).astype(jnp.int32)


def kernel(x_prompt, x_sample, cache_attn_k, cache_attn_v, cache_conv, cache_mem_k, cache_mem_v, page_table, mem_prompt, norm_mix_g, w_in, b_gate, q_norm_g, k_norm_g, lambda_q1, lambda_k1, lambda_q2, lambda_k2, subln_g, conv_w, w_out, norm_mem_g, norm_x_g, w_xq, w_xkv, xq_norm_g, xk_norm_g, w_xo, norm_ffn_g, w_group, b_group, w_expert_router, b_expert_router, w_gate_up, w_down):
    depth = w_in.shape[0]
    bp, sp, _ = x_prompt.shape
    bs, ts, _ = x_sample.shape
    tp, tsn = bp * sp, bs * ts
    slopes = jnp.exp2(-8.0 * jnp.arange(1, N_HEADS + 1, dtype=F32) / N_HEADS)
    xp = x_prompt.reshape(tp, D_MODEL)
    xs = x_sample.reshape(tsn, D_MODEL)
    moe_tm = 256
    moe_tiles = (tp + tsn) * 2 // moe_tm + N_EXPERTS
    outs = [[] for _ in range(8)]
    for l in range(depth):
        lam_init = 0.8 - 0.6 * math.exp(-0.3 * l)
        lam = (jnp.exp(jnp.sum(lambda_q1[l] * lambda_k1[l])) - jnp.exp(jnp.sum(lambda_q2[l] * lambda_k2[l]))
               + lam_init).reshape(1).astype(F32)
        out_scale = 1.0 - lam_init
        w_in_bf = w_in[l].astype(BF16)
        w_out_bf = w_out[l].astype(BF16)
        w_xq_bf = w_xq[l].astype(BF16)
        w_xkv_bf = w_xkv[l].astype(BF16)
        w_xo_bf = w_xo[l].astype(BF16)

        hb = _rmsnorm_cast(xp, norm_mix_g[l], 512)
        q, kf, kb, vf, vb, z, b, ga, gb = _inproj(hb, w_in_bf, q_norm_g[l], k_norm_g[l], b_gate[l], BF16, 1024, 256)
        o = _prompt_attn(q.reshape(bp, sp, D_MODEL), kb.reshape(bp, sp, D_MODEL), vb.reshape(bp, sp, D_MODEL),
                         lam, slopes, subln_g[l], out_scale, 512)
        merged = _mix_prompt(o.reshape(tp, D_MODEL), ga, gb, b, z, conv_w[l], sp, 256)
        xp = _mm_res(merged, w_out_bf, xp, 512, 1024)
        outs[0].append(kf.reshape(bp, sp, N_HEADS, HEAD_W))
        outs[1].append(vf.reshape(bp, sp, N_HEADS, V_DIM))
        outs[2].append(z.reshape(bp, sp, D_MODEL)[:, sp - 2:, :])

        hb = _rmsnorm_cast(xs, norm_mix_g[l], 512)
        q, kf, kb, vf, vb, z, b, ga, gb = _inproj(hb, w_in_bf, q_norm_g[l], k_norm_g[l], b_gate[l], F32, 1024, 256)
        shp = (bs, ts, D_MODEL)
        o = _sample_attn(q.reshape(shp), kf.reshape(shp), vf.reshape(shp), cache_attn_k[l], cache_attn_v[l],
                         page_table, lam, slopes, subln_g[l], out_scale, 4)
        z3 = z.reshape(shp)
        merged = _mix_sample(o, ga.reshape(shp), gb.reshape(shp), b.reshape(shp), z3, cache_conv[l], conv_w[l], 32)
        xs = _mm_res(merged.reshape(tsn, D_MODEL), w_out_bf, xs, 512, 1024)
        outs[5].append(kf.reshape(bs, ts, N_HEADS, HEAD_W))
        outs[6].append(vf.reshape(bs, ts, N_HEADS, V_DIM))
        outs[7].append(z3[:, ts - 2:, :])

        mem_nb = _rmsnorm_cast(mem_prompt.reshape(bp * N_MEM, D_MODEL), norm_mem_g[l], 512)
        mkf, mkb, mvf, mvb = _mem_kv(mem_nb, w_xkv_bf, xk_norm_g[l])
        xp = _xattn_prompt(xp.reshape(bp, sp, D_MODEL), norm_x_g[l], w_xq_bf, xq_norm_g[l],
                           mkb.reshape(bp, N_MEM, X_WIDTH), mvb.reshape(bp, N_MEM, X_WIDTH), w_xo_bf,
                           512).reshape(tp, D_MODEL)
        xs = _xattn_sample(xs, norm_x_g[l], w_xq_bf, xq_norm_g[l], cache_mem_k[l], cache_mem_v[l], w_xo_bf, ts, 8)
        outs[3].append(mkf.reshape(bp, N_MEM, X_HEADS, X_HEAD_DIM))
        outs[4].append(mvf.reshape(bp, N_MEM, X_HEADS, X_HEAD_DIM))

        w_r = jnp.zeros((D_MODEL, LANES), F32).at[:, :N_GROUPS].set(w_group[l])
        w_r = w_r.at[:, N_GROUPS:N_GROUPS + N_EXPERTS].set(w_expert_router[l])
        b_r = jnp.zeros((1, LANES), F32).at[0, :N_GROUPS].set(b_group[l])
        b_r = b_r.at[0, N_GROUPS:N_GROUPS + N_EXPERTS].set(b_expert_router[l])
        hf, eid, wts = _router(xp, xs, norm_ffn_g[l], w_r, b_r, 512)
        src_row, row_w, tile_e, n_active, dest = _route_plan(eid[:, :2], wts[:, :2], moe_tm, moe_tiles)
        rows = _gather_rows(hf, src_row, moe_tiles, moe_tm)
        ys = _expert_ffn(rows, row_w, tile_e, n_active, w_gate_up[l], w_down[l], moe_tm)
        xp = _combine(xp, ys, dest[:, 0], dest[:, 1], 0, moe_tm)
        xs = _combine(xs, ys, dest[:, 0], dest[:, 1], tp, moe_tm)
    stack = lambda i: jnp.stack(outs[i])
    return (xp.reshape(bp, sp, D_MODEL), xs.reshape(bs, ts, D_MODEL), stack(0), stack(1), stack(2), stack(3),
            stack(4), stack(5), stack(6), stack(7))
```

```python
import functools
import math

import jax
import jax.numpy as jnp
from jax import lax
from jax.experimental import pallas as pl
from jax.experimental.pallas import tpu as pltpu

F32 = jnp.float32
BF16 = jnp.bfloat16

D_MODEL = 2048
N_HEADS = 8
HEAD_DIM = 128
V_DIM = 256
HEAD_W = 2 * HEAD_DIM
PAGE = 128
X_HEADS = 4
X_HEAD_DIM = 128
X_WIDTH = X_HEADS * X_HEAD_DIM
N_MEM = 256
N_GROUPS = 4
EXPERTS_PER_GROUP = 8
N_EXPERTS = N_GROUPS * EXPERTS_PER_GROUP
D_FF = 512
EPS = 1e-6
NEG = -1e30
LANES = 128
MIB = 1024 * 1024
LOG2E = math.log2(math.e)


def _cparams(sem, vmem_mib):
    return pltpu.CompilerParams(dimension_semantics=sem, vmem_limit_bytes=vmem_mib * MIB)


def _rms(x, g):
    ms = jnp.mean(x * x, axis=-1, keepdims=True)
    return x * lax.rsqrt(ms + EPS) * g


def _headnorm(acc, g, scale=None):
    outs = []
    for c in range(acc.shape[1] // HEAD_DIM):
        y = _rms(acc[:, c * HEAD_DIM:(c + 1) * HEAD_DIM], g)
        outs.append(y if scale is None else y * scale)
    return jnp.concatenate(outs, axis=1)


def _norm_kernel(x_ref, g_ref, o_ref):
    o_ref[...] = _rms(x_ref[...], g_ref[...]).astype(o_ref.dtype)


def _rmsnorm_cast(x2d, g, tm):
    t, d = x2d.shape
    return pl.pallas_call(
        _norm_kernel,
        out_shape=jax.ShapeDtypeStruct((t, d), BF16),
        grid=(t // tm,),
        in_specs=[pl.BlockSpec((tm, d), lambda i: (i, 0)),
                  pl.BlockSpec((1, d), lambda i: (0, 0))],
        out_specs=pl.BlockSpec((tm, d), lambda i: (i, 0)),
        compiler_params=_cparams(("parallel",), 32),
        name="rmsnorm_cast",
    )(x2d, g.reshape(1, d))


def _inproj_kernel(h_ref, wq, wk, wv, wc, wb, wu, wga, wgb, qg_ref, kg_ref, bga_ref, bgb_ref,
                   q_o, kf_o, kb_o, vf_o, vb_o, z_o, b_o, ga_o, gb_o, *, q_scale):
    h = h_ref[...]

    def mm(w):
        return jnp.dot(h, w[...], preferred_element_type=F32)

    q_o[...] = _headnorm(mm(wq), qg_ref[...], q_scale).astype(q_o.dtype)
    k = _headnorm(mm(wk), kg_ref[...])
    kf_o[...] = k
    kb_o[...] = k.astype(BF16)
    v = mm(wv)
    vf_o[...] = v
    vb_o[...] = v.astype(BF16)
    z_o[...] = mm(wc) * mm(wu)
    b_o[...] = mm(wb)
    ga_o[...] = jax.nn.sigmoid(mm(wga) + bga_ref[...])
    gb_o[...] = jax.nn.sigmoid(mm(wgb) + bgb_ref[...])


def _inproj(hb, w_in_bf, q_g, k_g, b_gate, q_dtype, tm, tn):
    t = hb.shape[0]
    nb = D_MODEL // tn
    region = {"q": 0, "k": 1, "v": 2, "c": 3, "b": 4, "u": 5, "ga": 6, "gb": 7}

    def wspec(r):
        return pl.BlockSpec((D_MODEL, tn), lambda i, j, r=r: (0, r * nb + j))

    tile = pl.BlockSpec((tm, tn), lambda i, j: (i, j))
    vec = pl.BlockSpec((1, HEAD_DIM), lambda i, j: (0, 0))
    bias = pl.BlockSpec((1, tn), lambda i, j: (0, j))
    f32o = jax.ShapeDtypeStruct((t, D_MODEL), F32)
    bfo = jax.ShapeDtypeStruct((t, D_MODEL), BF16)
    return pl.pallas_call(
        functools.partial(_inproj_kernel, q_scale=HEAD_DIM ** -0.5 * LOG2E),
        out_shape=(jax.ShapeDtypeStruct((t, D_MODEL), q_dtype), f32o, bfo, f32o, bfo, f32o, f32o, f32o, f32o),
        grid=(t // tm, nb),
        in_specs=[pl.BlockSpec((tm, D_MODEL), lambda i, j: (i, 0))]
        + [wspec(region[n]) for n in ("q", "k", "v", "c", "b", "u", "ga", "gb")]
        + [vec, vec, bias, bias],
        out_specs=(tile,) * 9,
        compiler_params=_cparams(("parallel", "arbitrary"), 48),
        name="inproj",
    )(hb, *([w_in_bf] * 8), q_g.reshape(1, HEAD_DIM), k_g.reshape(1, HEAD_DIM),
      b_gate[0:1], b_gate[1:2])


def _lane_tiles(x):
    return [x[:, c * LANES:(c + 1) * LANES] for c in range(x.shape[1] // LANES)]


def _lane_repeat(x, n):
    return x if n == 1 else jnp.concatenate([x] * n, axis=1)


def _online_update(s_list, v_list, m_prev, l_prev, acc_prev):
    tiles = [tile for s in s_list for tile in _lane_tiles(s)]
    m_cur = jnp.max(functools.reduce(jnp.maximum, tiles), axis=1, keepdims=True)
    m_next = jnp.maximum(m_prev, m_cur)
    alpha = jnp.exp2(m_prev - m_next)
    l_next = alpha * l_prev
    pv = None
    for s, v in zip(s_list, v_list):
        p = jnp.exp2(s - _lane_repeat(m_next, s.shape[1] // LANES))
        l_next = l_next + functools.reduce(jnp.add, _lane_tiles(p))
        d = jnp.dot(p.astype(BF16), v, preferred_element_type=F32)
        pv = d if pv is None else pv + d
    acc_next = acc_prev * _lane_repeat(alpha, V_DIM // LANES) + pv
    return m_next, l_next, acc_next


def _diff_out(a1, l1, a2, l2, lam, g, out_scale):
    r1 = 1.0 / jnp.sum(l1, axis=1, keepdims=True)
    r2 = 1.0 / jnp.sum(l2, axis=1, keepdims=True)
    o = a1 * r1 - lam * (a2 * r2)
    return _rms(o, g) * out_scale


def _init_softmax_state(m_scr, l_scr, acc_scr):
    m_scr[...] = jnp.full(m_scr.shape, NEG, F32)
    l_scr[...] = jnp.zeros(l_scr.shape, F32)
    acc_scr[...] = jnp.zeros(acc_scr.shape, F32)


def _pattn_kernel(qi_ref, ki_ref, lam_ref, slopes_ref, q_ref, k_ref, v_ref, g_ref, o_ref, m_scr, l_scr, acc_scr,
                  *, t, out_scale):
    h = pl.program_id(1)
    step = pl.program_id(2)
    qi = qi_ref[step]
    ki = ki_ref[step]

    @pl.when(ki == 0)
    def _():
        _init_softmax_state(m_scr, l_scr, acc_scr)

    def body(masked):
        q = q_ref[0]
        k = k_ref[0]
        v = v_ref[0]
        kpos = ((ki - qi) * t + lax.broadcasted_iota(jnp.int32, (1, t), 1)).astype(F32)
        bias = slopes_ref[h] * kpos
        if masked:
            keep = (lax.broadcasted_iota(jnp.int32, (t, t), 1)
                    <= lax.broadcasted_iota(jnp.int32, (t, t), 0))
        state = []
        for m in range(2):
            s = lax.dot_general(q[:, m * HEAD_DIM:(m + 1) * HEAD_DIM], k[:, m * HEAD_DIM:(m + 1) * HEAD_DIM],
                                (((1,), (1,)), ((), ())), preferred_element_type=F32)
            s = s + bias
            if masked:
                s = jnp.where(keep, s, NEG)
            state.append(_online_update([s], [v], m_scr[m], l_scr[m], acc_scr[m]))
        return state

    @pl.when(ki < qi)
    def _():
        for m, (m_next, l_next, acc_next) in enumerate(body(False)):
            m_scr[m] = m_next
            l_scr[m] = l_next
            acc_scr[m] = acc_next

    @pl.when(ki == qi)
    def _():
        (_, l1, a1), (_, l2, a2) = body(True)
        o_ref[0] = _diff_out(a1, l1, a2, l2, lam_ref[0], g_ref[...], out_scale)


def _prompt_attn(qb, kb, vb, lam, slopes2, subln_g, out_scale, t):
    b, s, _ = qb.shape
    n = s // t
    pairs = [(qi, ki) for qi in range(n) for ki in range(qi + 1)]
    qi_tab = jnp.asarray([p[0] for p in pairs], jnp.int32)
    ki_tab = jnp.asarray([p[1] for p in pairs], jnp.int32)
    smem = pl.BlockSpec(memory_space=pltpu.SMEM)
    grid_spec = pltpu.PrefetchScalarGridSpec(
        num_scalar_prefetch=2,
        grid=(b, N_HEADS, len(pairs)),
        in_specs=[smem, smem,
                  pl.BlockSpec((1, t, HEAD_W), lambda bi, h, st, qt, kt: (bi, qt[st], h)),
                  pl.BlockSpec((1, t, HEAD_W), lambda bi, h, st, qt, kt: (bi, kt[st], h)),
                  pl.BlockSpec((1, t, HEAD_W), lambda bi, h, st, qt, kt: (bi, kt[st], h)),
                  pl.BlockSpec((1, V_DIM), lambda bi, h, st, qt, kt: (0, 0))],
        out_specs=pl.BlockSpec((1, t, HEAD_W), lambda bi, h, st, qt, kt: (bi, qt[st], h)),
        scratch_shapes=[pltpu.VMEM((2, t, LANES), F32), pltpu.VMEM((2, t, LANES), F32),
                        pltpu.VMEM((2, t, V_DIM), F32)],
    )
    return pl.pallas_call(
        functools.partial(_pattn_kernel, t=t, out_scale=out_scale),
        out_shape=jax.ShapeDtypeStruct((b, s, D_MODEL), F32),
        grid_spec=grid_spec,
        compiler_params=_cparams(("parallel", "parallel", "arbitrary"), 40),
        name="prompt_attn",
    )(qi_tab, ki_tab, lam, slopes2, qb, kb, vb, subln_g.reshape(1, V_DIM))


def _page_head(lo_ref, hi_ref, h):
    rows = pl.ds(h, PAGE, stride=N_HEADS)
    return jnp.concatenate([lo_ref[0, rows, :], hi_ref[0, rows, :]], axis=1).astype(BF16)


def _sattn_kernel(pt_ref, lam_ref, slopes_ref, q_ref, kn_ref, vn_ref, *rest, npg, n_steps, past, out_scale):
    del pt_ref
    k_refs = rest[:2 * npg]
    v_refs = rest[2 * npg:4 * npg]
    g_ref, o_ref, qs_scr, m_scr, l_scr, acc_scr = rest[4 * npg:]
    p = pl.program_id(1)
    tq = q_ref.shape[1]

    @pl.when(p == 0)
    def _():
        _init_softmax_state(m_scr, l_scr, acc_scr)
        q = q_ref[0]
        zero = jnp.zeros((tq, HEAD_DIM), F32)
        for h in range(N_HEADS):
            q1 = q[:, h * HEAD_W:h * HEAD_W + HEAD_DIM]
            q2 = q[:, h * HEAD_W + HEAD_DIM:(h + 1) * HEAD_W]
            qs_scr[h] = jnp.concatenate([jnp.concatenate([q1, zero], axis=1),
                                         jnp.concatenate([zero, q2], axis=1)], axis=0).astype(BF16)

    lane = lax.broadcasted_iota(jnp.int32, (1, PAGE), 1)
    nt = (((1,), (1,)), ((), ()))

    def body(last):
        if last:
            kn = kn_ref[0]
            vn = vn_ref[0]
            pad = jnp.zeros((PAGE - tq, HEAD_W), F32)
            keep = lane <= lax.broadcasted_iota(jnp.int32, (2 * tq, PAGE), 0) % tq
        state = []
        for h in range(N_HEADS):
            qs = qs_scr[h]
            slope = slopes_ref[h]
            s_list, v_list = [], []
            for j in range(npg):
                kh = _page_head(k_refs[2 * j], k_refs[2 * j + 1], h)
                s = lax.dot_general(qs, kh, nt, preferred_element_type=F32)
                kpos = ((p * npg + j) * PAGE - past + lane).astype(F32)
                s_list.append(s + slope * kpos)
                v_list.append(_page_head(v_refs[2 * j], v_refs[2 * j + 1], h))
            if last:
                cols = slice(h * HEAD_W, (h + 1) * HEAD_W)
                kh = jnp.concatenate([kn[:, cols], pad], axis=0).astype(BF16)
                s = lax.dot_general(qs, kh, nt, preferred_element_type=F32)
                s_list.append(jnp.where(keep, s + slope * lane.astype(F32), NEG))
                v_list.append(jnp.concatenate([vn[:, cols], pad], axis=0).astype(BF16))
            state.append(_online_update(s_list, v_list, m_scr[h], l_scr[h], acc_scr[h]))
        return state

    @pl.when(p < n_steps - 1)
    def _():
        for h, (m_next, l_next, acc_next) in enumerate(body(False)):
            m_scr[h] = m_next
            l_scr[h] = l_next
            acc_scr[h] = acc_next

    @pl.when(p == n_steps - 1)
    def _():
        lam = lam_ref[0]
        g = g_ref[...]
        o_ref[0] = jnp.concatenate(
            [_diff_out(acc[:tq], l[:tq], acc[tq:], l[tq:], lam, g, out_scale) for _, l, acc in body(True)], axis=1)


def _sample_attn(qs, kn, vn, pool_k, pool_v, page_table, lam, slopes, subln_g, out_scale, npg):
    db, tq, _ = qs.shape
    n_pages = page_table.shape[1]
    n_steps = n_pages // npg
    n_phys = pool_k.shape[0]
    pk = pool_k.reshape(n_phys, PAGE * N_HEADS, HEAD_W)
    pv = pool_v.reshape(n_phys, PAGE * N_HEADS, V_DIM)
    smem = pl.BlockSpec(memory_space=pltpu.SMEM)
    tok = pl.BlockSpec((1, tq, D_MODEL), lambda b, p, pt: (b, 0, 0))

    def half_page(j, c):
        return pl.BlockSpec((1, PAGE * N_HEADS, LANES), lambda b, p, pt, j=j, c=c: (pt[b, p * npg + j], 0, c))

    pages = [half_page(j, c) for j in range(npg) for c in range(2)]
    grid_spec = pltpu.PrefetchScalarGridSpec(
        num_scalar_prefetch=1,
        grid=(db, n_steps),
        in_specs=[smem, smem, tok, tok, tok] + pages * 2
        + [pl.BlockSpec((1, V_DIM), lambda b, p, pt: (0, 0))],
        out_specs=tok,
        scratch_shapes=[pltpu.VMEM((N_HEADS, 2 * tq, HEAD_W), BF16),
                        pltpu.VMEM((N_HEADS, 2 * tq, LANES), F32),
                        pltpu.VMEM((N_HEADS, 2 * tq, LANES), F32),
                        pltpu.VMEM((N_HEADS, 2 * tq, V_DIM), F32)],
    )
    return pl.pallas_call(
        functools.partial(_sattn_kernel, npg=npg, n_steps=n_steps, past=n_pages * PAGE, out_scale=out_scale),
        out_shape=jax.ShapeDtypeStruct((db, tq, D_MODEL), F32),
        grid_spec=grid_spec,
        compiler_params=_cparams(("parallel", "arbitrary"), 48),
        name="sample_attn",
    )(page_table, lam, slopes, qs, kn, vn, *([pk] * (2 * npg)), *([pv] * (2 * npg)), subln_g.reshape(1, V_DIM))


def _conv_merge(o, ga, gb, b, z, z1, z2, cw):
    y = z2 * cw[0] + z1 * cw[1] + z * cw[2]
    return (ga * o + gb * (b * y)).astype(BF16)


def _mix_prompt_kernel(o_ref, ga_ref, gb_ref, b_ref, z_ref, zp_ref, cw_ref, out_ref, *, tiles_per_seq):
    i = pl.program_id(0)
    z = z_ref[...]
    tm = z.shape[0]
    zp = jnp.where(i % tiles_per_seq == 0, 0.0, zp_ref[...])
    pm1 = zp[7:8, :]
    pm2 = zp[6:7, :]
    row = lax.broadcasted_iota(jnp.int32, (tm, 1), 0)
    z1 = jnp.where(row == 0, pm1, pltpu.roll(z, 1, 0))
    z2 = jnp.where(row == 0, pm2, jnp.where(row == 1, pm1, pltpu.roll(z, 2, 0)))
    cw = cw_ref[...]
    out_ref[...] = _conv_merge(o_ref[...], ga_ref[...], gb_ref[...], b_ref[...], z, z1, z2,
                               (cw[0:1], cw[1:2], cw[2:3]))


def _mix_prompt(o, ga, gb, b, z, conv_w, seq, tm):
    t = z.shape[0]
    tile = pl.BlockSpec((tm, D_MODEL), lambda i: (i, 0))
    prev = pl.BlockSpec((8, D_MODEL), lambda i: (jnp.maximum(i * (tm // 8) - 1, 0), 0))
    return pl.pallas_call(
        functools.partial(_mix_prompt_kernel, tiles_per_seq=seq // tm),
        out_shape=jax.ShapeDtypeStruct((t, D_MODEL), BF16),
        grid=(t // tm,),
        in_specs=[tile, tile, tile, tile, tile, prev, pl.BlockSpec((3, D_MODEL), lambda i: (0, 0))],
        out_specs=tile,
        compiler_params=_cparams(("parallel",), 40),
        name="mix_prompt",
    )(o, ga, gb, b, z, z, conv_w)


def _mix_sample_kernel(o_ref, ga_ref, gb_ref, b_ref, z_ref, cc_ref, cw_ref, out_ref):
    z = z_ref[...]
    cc = cc_ref[...]
    c0 = cc[:, 0:1, :]
    c1 = cc[:, 1:2, :]
    row = lax.broadcasted_iota(jnp.int32, (1, z.shape[1], 1), 1)
    z1 = jnp.where(row == 0, c1, pltpu.roll(z, 1, 1))
    z2 = jnp.where(row == 0, c0, jnp.where(row == 1, c1, pltpu.roll(z, 2, 1)))
    cw = cw_ref[...]
    out_ref[...] = _conv_merge(o_ref[...], ga_ref[...], gb_ref[...], b_ref[...], z, z1, z2,
                               (cw[0:1][None], cw[1:2][None], cw[2:3][None]))


def _mix_sample(o, ga, gb, b, z, cache_conv, conv_w, g):
    db, ts, _ = z.shape
    tile = pl.BlockSpec((g, ts, D_MODEL), lambda i: (i, 0, 0))
    return pl.pallas_call(
        _mix_sample_kernel,
        out_shape=jax.ShapeDtypeStruct((db, ts, D_MODEL), BF16),
        grid=(db // g,),
        in_specs=[tile, tile, tile, tile, tile,
                  pl.BlockSpec((g, cache_conv.shape[1], D_MODEL), lambda i: (i, 0, 0)),
                  pl.BlockSpec((3, D_MODEL), lambda i: (0, 0))],
        out_specs=tile,
        compiler_params=_cparams(("parallel",), 40),
        name="mix_sample",
    )(o, ga, gb, b, z, cache_conv, conv_w)


def _mm_res_kernel(a_ref, w_ref, r_ref, o_ref):
    o_ref[...] = r_ref[...] + jnp.dot(a_ref[...], w_ref[...], preferred_element_type=F32)


def _mm_res(a, w, r, tm, tn):
    t, k = a.shape
    n = w.shape[1]
    return pl.pallas_call(
        _mm_res_kernel,
        out_shape=jax.ShapeDtypeStruct((t, n), F32),
        grid=(t // tm, n // tn),
        in_specs=[pl.BlockSpec((tm, k), lambda i, j: (i, 0)),
                  pl.BlockSpec((k, tn), lambda i, j: (0, j)),
                  pl.BlockSpec((tm, tn), lambda i, j: (i, j))],
        out_specs=pl.BlockSpec((tm, tn), lambda i, j: (i, j)),
        compiler_params=_cparams(("parallel", "parallel"), 48),
        name="outproj",
    )(a, w, r)


def _memkv_kernel(h_ref, wk_ref, wv_ref, g_ref, kf_o, kb_o, vf_o, vb_o):
    h = h_ref[...]
    k = _headnorm(jnp.dot(h, wk_ref[...], preferred_element_type=F32), g_ref[...])
    kf_o[...] = k
    kb_o[...] = k.astype(BF16)
    v = jnp.dot(h, wv_ref[...], preferred_element_type=F32)
    vf_o[...] = v
    vb_o[...] = v.astype(BF16)


def _mem_kv(mem_nb, w_xkv_bf, xk_g):
    t = mem_nb.shape[0]
    f32o = jax.ShapeDtypeStruct((t, X_WIDTH), F32)
    bfo = jax.ShapeDtypeStruct((t, X_WIDTH), BF16)
    full = lambda s: pl.BlockSpec(s, lambda i: (0,) * len(s))
    return pl.pallas_call(
        _memkv_kernel,
        out_shape=(f32o, bfo, f32o, bfo),
        grid=(1,),
        in_specs=[full((t, D_MODEL)),
                  pl.BlockSpec((D_MODEL, X_WIDTH), lambda i: (0, 0)),
                  pl.BlockSpec((D_MODEL, X_WIDTH), lambda i: (0, 1)),
                  full((1, X_HEAD_DIM))],
        out_specs=(full((t, X_WIDTH)),) * 4,
        compiler_params=_cparams(("arbitrary",), 32),
        name="mem_kv",
    )(mem_nb, w_xkv_bf, w_xkv_bf, xk_g.reshape(1, X_HEAD_DIM))


def _xattn_head(qh, kh, vh):
    s = lax.dot_general(qh, kh, (((1,), (1,)), ((), ())), preferred_element_type=F32)
    e = jnp.exp(s - jnp.max(s, axis=1, keepdims=True))
    p = e * (1.0 / jnp.sum(e, axis=1, keepdims=True))
    return jnp.dot(p.astype(BF16), vh, preferred_element_type=F32)


def _xattn_q(x, g_ref, wq_ref, qg_ref):
    hn = _rms(x, g_ref[...]).astype(BF16)
    q = jnp.dot(hn, wq_ref[...], preferred_element_type=F32)
    return _headnorm(q, qg_ref[...], X_HEAD_DIM ** -0.5)


def _xattn_prompt_kernel(x_ref, g_ref, wq_ref, qg_ref, mk_ref, mv_ref, wo_ref, o_ref):
    x = x_ref[0]
    q = _xattn_q(x, g_ref, wq_ref, qg_ref).astype(BF16)
    mk = mk_ref[0]
    mv = mv_ref[0]
    heads = []
    for g in range(X_HEADS):
        sl = slice(g * X_HEAD_DIM, (g + 1) * X_HEAD_DIM)
        heads.append(_xattn_head(q[:, sl], mk[:, sl], mv[:, sl]))
    o = jnp.concatenate(heads, axis=1).astype(BF16)
    o_ref[0] = x + jnp.dot(o, wo_ref[...], preferred_element_type=F32)


def _xattn_prompt(x, g, w_xq_bf, xq_g, mk_b, mv_b, w_xo_bf, tm):
    b, s, _ = x.shape
    tile = pl.BlockSpec((1, tm, D_MODEL), lambda bi, i: (bi, i, 0))
    mem = pl.BlockSpec((1, N_MEM, X_WIDTH), lambda bi, i: (bi, 0, 0))
    const = lambda shp: pl.BlockSpec(shp, lambda bi, i: (0, 0))
    return pl.pallas_call(
        _xattn_prompt_kernel,
        out_shape=jax.ShapeDtypeStruct((b, s, D_MODEL), F32),
        grid=(b, s // tm),
        in_specs=[tile, const((1, D_MODEL)), const((D_MODEL, X_WIDTH)), const((1, X_HEAD_DIM)),
                  mem, mem, const((X_WIDTH, D_MODEL))],
        out_specs=tile,
        compiler_params=_cparams(("parallel", "parallel"), 48),
        name="xattn_prompt",
    )(x, g.reshape(1, D_MODEL), w_xq_bf, xq_g.reshape(1, X_HEAD_DIM), mk_b, mv_b, w_xo_bf)


def _xattn_sample_kernel(x_ref, g_ref, wq_ref, qg_ref, mk_ref, mv_ref, wo_ref, o_ref, *, ts):
    x = x_ref[...]
    q = _xattn_q(x, g_ref, wq_ref, qg_ref)
    per_batch = []
    for bi in range(mk_ref.shape[0]):
        rows = slice(bi * ts, (bi + 1) * ts)
        heads = []
        for g in range(X_HEADS):
            sl = slice(g * X_HEAD_DIM, (g + 1) * X_HEAD_DIM)
            mem_rows = pl.ds(g, N_MEM, stride=X_HEADS)
            heads.append(_xattn_head(q[rows, sl].astype(BF16), mk_ref[bi, mem_rows, :].astype(BF16),
                                     mv_ref[bi, mem_rows, :].astype(BF16)))
        per_batch.append(jnp.concatenate(heads, axis=1))
    a = jnp.concatenate(per_batch, axis=0).astype(BF16)
    o_ref[...] = x + jnp.dot(a, wo_ref[...], preferred_element_type=F32)


def _xattn_sample(x2d, g, w_xq_bf, xq_g, cache_mk, cache_mv, w_xo_bf, ts, gb):
    t = x2d.shape[0]
    db = cache_mk.shape[0]
    mk = cache_mk.reshape(db, N_MEM * X_HEADS, X_HEAD_DIM)
    mv = cache_mv.reshape(db, N_MEM * X_HEADS, X_HEAD_DIM)
    tile = pl.BlockSpec((gb * ts, D_MODEL), lambda i: (i, 0))
    mem = pl.BlockSpec((gb, N_MEM * X_HEADS, X_HEAD_DIM), lambda i: (i, 0, 0))
    const = lambda shp: pl.BlockSpec(shp, lambda i: (0, 0))
    return pl.pallas_call(
        functools.partial(_xattn_sample_kernel, ts=ts),
        out_shape=jax.ShapeDtypeStruct((t, D_MODEL), F32),
        grid=(db // gb,),
        in_specs=[tile, const((1, D_MODEL)), const((D_MODEL, X_WIDTH)), const((1, X_HEAD_DIM)),
                  mem, mem, const((X_WIDTH, D_MODEL))],
        out_specs=tile,
        compiler_params=_cparams(("parallel",), 48),
        name="xattn_sample",
    )(x2d, g.reshape(1, D_MODEL), w_xq_bf, xq_g.reshape(1, X_HEAD_DIM), mk, mv, w_xo_bf)


def _first_argmax(v, lane):
    vmax = jnp.max(v, axis=1, keepdims=True)
    idx = jnp.min(jnp.where(v == vmax, lane, LANES), axis=1, keepdims=True)
    return vmax, idx


def _router_kernel(xp_ref, xs_ref, g_ref, wr_ref, br_ref, h_o, eid_o, w_o, *, n_prompt_tiles):
    i = pl.program_id(0)

    def body(x_ref):
        h = _rms(x_ref[...], g_ref[...])
        h_o[...] = h
        logits = jnp.dot(h, wr_ref[...], preferred_element_type=F32,
                         precision=lax.Precision.HIGHEST) + br_ref[...]
        lane = lax.broadcasted_iota(jnp.int32, logits.shape, 1)
        ninf = -jnp.inf
        gl = jnp.where(lane < N_GROUPS, logits, ninf)
        gmax, gidx = _first_argmax(gl, lane)
        g_w = 1.0 / jnp.sum(jnp.exp(gl - gmax), axis=1, keepdims=True)
        lo = N_GROUPS + gidx * EXPERTS_PER_GROUP
        el = jnp.where((lane >= lo) & (lane < lo + EXPERTS_PER_GROUP), logits, ninf)
        v1, i1 = _first_argmax(el, lane)
        v2, i2 = _first_argmax(jnp.where(lane == i1, ninf, el), lane)
        e2 = jnp.exp(v2 - v1)
        w1 = g_w / (1.0 + e2)
        w2 = g_w * e2 / (1.0 + e2)
        eid_o[...] = jnp.where(lane == 0, i1 - N_GROUPS, jnp.where(lane == 1, i2 - N_GROUPS, 0))
        w_o[...] = jnp.where(lane == 0, w1, jnp.where(lane == 1, w2, 0.0))

    @pl.when(i < n_prompt_tiles)
    def _():
        body(xp_ref)

    @pl.when(i >= n_prompt_tiles)
    def _():
        body(xs_ref)


def _router(xp2d, xs2d, g, w_r, b_r, tm):
    tp, ts = xp2d.shape[0], xs2d.shape[0]
    npt, nst = tp // tm, ts // tm
    t = tp + ts
    tile = pl.BlockSpec((tm, D_MODEL), lambda i: (i, 0))
    const = lambda shp: pl.BlockSpec(shp, lambda i: (0, 0))
    return pl.pallas_call(
        functools.partial(_router_kernel, n_prompt_tiles=npt),
        out_shape=(jax.ShapeDtypeStruct((t, D_MODEL), F32),
                   jax.ShapeDtypeStruct((t, LANES), jnp.int32),
                   jax.ShapeDtypeStruct((t, LANES), F32)),
        grid=(npt + nst,),
        in_specs=[pl.BlockSpec((tm, D_MODEL), lambda i: (jnp.minimum(i, npt - 1), 0)),
                  pl.BlockSpec((tm, D_MODEL), lambda i: (jnp.maximum(i - npt, 0), 0)),
                  const((1, D_MODEL)), const((D_MODEL, LANES)), const((1, LANES))],
        out_specs=(tile, pl.BlockSpec((tm, LANES), lambda i: (i, 0)), pl.BlockSpec((tm, LANES), lambda i: (i, 0))),
        compiler_params=_cparams(("parallel",), 40),
        name="moe_router",
    )(xp2d, xs2d, g.reshape(1, D_MODEL), w_r, b_r)


def _dispatch_kernel(dest_ref, zflag_ref, h_ref, rows_hbm, zero_scr, sem, *, tmd, tm, n_tiles):
    i = pl.program_id(0)

    def zero_tile(t, start):
        @pl.when(zflag_ref[t] != 0)
        def _():
            cp = pltpu.make_async_copy(zero_scr, rows_hbm.at[pl.ds(pl.multiple_of(t * tm, tm), tm)], sem.at[2])
            if start:
                cp.start()
            else:
                cp.wait()

    @pl.when(i == 0)
    def _():
        zero_scr[...] = jnp.zeros(zero_scr.shape, F32)
        lax.fori_loop(0, n_tiles, lambda t, c: (zero_tile(t, True), c)[1], 0)
        lax.fori_loop(0, n_tiles, lambda t, c: (zero_tile(t, False), c)[1], 0)

    base = 2 * i * tmd

    def row_copy(r, slot):
        return pltpu.make_async_copy(h_ref.at[pl.ds(r, 1)], rows_hbm.at[pl.ds(dest_ref[base + 2 * r + slot], 1)],
                                     sem.at[slot])

    def start(r, carry):
        row_copy(r, 0).start()
        row_copy(r, 1).start()
        return carry

    lax.fori_loop(0, tmd, start, 0)
    for slot in range(2):
        pltpu.make_async_copy(h_ref, rows_hbm.at[pl.ds(0, tmd)], sem.at[slot]).wait()


def _dispatch_rows(hf, dest_flat, zflag, n_tiles, tm, tmd):
    t, d = hf.shape
    grid_spec = pltpu.PrefetchScalarGridSpec(
        num_scalar_prefetch=2,
        grid=(t // tmd,),
        in_specs=[pl.BlockSpec((tmd, d), lambda i, dest, zf: (i, 0))],
        out_specs=pl.BlockSpec(memory_space=pl.ANY),
        scratch_shapes=[pltpu.VMEM((tm, d), F32), pltpu.SemaphoreType.DMA((3,))],
    )
    return pl.pallas_call(
        functools.partial(_dispatch_kernel, tmd=tmd, tm=tm, n_tiles=n_tiles),
        out_shape=jax.ShapeDtypeStruct((n_tiles * tm, d), F32),
        grid_spec=grid_spec,
        compiler_params=_cparams(("arbitrary",), 32),
        name="moe_dispatch",
    )(dest_flat, zflag, hf)


def _ffn_kernel(te_ref, na_ref, x_ref, wgu_ref, wd_ref, o_ref, wgu_scr, wd_scr):
    t = pl.program_id(0)

    @pl.when(t < na_ref[0])
    def _():
        e = te_ref[t]
        e_prev = te_ref[jnp.maximum(t - 1, 0)]

        @pl.when((t == 0) | (e != e_prev))
        def _():
            wgu_scr[...] = wgu_ref[0].astype(BF16)
            wd_scr[...] = wd_ref[0].astype(BF16)

        gu = jnp.dot(x_ref[...].astype(BF16), wgu_scr[...], preferred_element_type=F32)
        gate = gu[:, :D_FF]
        act = (gate * jax.nn.sigmoid(gate)) * gu[:, D_FF:]
        o_ref[...] = jnp.dot(act.astype(BF16), wd_scr[...], preferred_element_type=F32)

    @pl.when(t >= na_ref[0])
    def _():
        o_ref[...] = jnp.zeros(o_ref.shape, F32)


def _expert_ffn(xs, tile_e, n_active, w_gu, w_down, tm):
    n_tiles = xs.shape[0] // tm
    grid_spec = pltpu.PrefetchScalarGridSpec(
        num_scalar_prefetch=2,
        grid=(n_tiles,),
        in_specs=[pl.BlockSpec((tm, D_MODEL), lambda t, te, na: (jnp.minimum(t, na[0] - 1), 0)),
                  pl.BlockSpec((1, D_MODEL, 2 * D_FF), lambda t, te, na: (te[t], 0, 0)),
                  pl.BlockSpec((1, D_FF, D_MODEL), lambda t, te, na: (te[t], 0, 0))],
        out_specs=pl.BlockSpec((tm, D_MODEL), lambda t, te, na: (t, 0)),
        scratch_shapes=[pltpu.VMEM((D_MODEL, 2 * D_FF), BF16), pltpu.VMEM((D_FF, D_MODEL), BF16)],
    )
    return pl.pallas_call(
        _ffn_kernel,
        out_shape=jax.ShapeDtypeStruct((n_tiles * tm, D_MODEL), F32),
        grid_spec=grid_spec,
        compiler_params=_cparams(("arbitrary",), 48),
        name="moe_ffn",
    )(tile_e, n_active, xs, w_gu, w_down)


def _combine_kernel(dest_ref, x_ref, w_ref, ys_hbm, o_ref, buf_a, buf_b, sem, *, tm, base_tok):
    base = 2 * (base_tok + pl.program_id(0) * tm)
    bufs = (buf_a, buf_b)

    def start(r, carry):
        for slot in range(2):
            pltpu.make_async_copy(ys_hbm.at[pl.ds(dest_ref[base + 2 * r + slot], 1)], bufs[slot].at[pl.ds(r, 1)],
                                  sem.at[slot]).start()
        return carry

    lax.fori_loop(0, tm, start, 0)
    for slot in range(2):
        pltpu.make_async_copy(ys_hbm.at[pl.ds(0, tm)], bufs[slot], sem.at[slot]).wait()
    w = w_ref[...]
    o_ref[...] = x_ref[...] + (w[:, 0:1] * buf_a[...] + w[:, 1:2] * buf_b[...])


def _combine(x2d, wts, ys, dest_flat, base_tok, tm):
    t, d = x2d.shape
    base_blk = base_tok // tm
    grid_spec = pltpu.PrefetchScalarGridSpec(
        num_scalar_prefetch=1,
        grid=(t // tm,),
        in_specs=[pl.BlockSpec((tm, d), lambda i, dest: (i, 0)),
                  pl.BlockSpec((tm, LANES), lambda i, dest: (i + base_blk, 0)),
                  pl.BlockSpec(memory_space=pl.ANY)],
        out_specs=pl.BlockSpec((tm, d), lambda i, dest: (i, 0)),
        scratch_shapes=[pltpu.VMEM((tm, d), F32), pltpu.VMEM((tm, d), F32), pltpu.SemaphoreType.DMA((2,))],
    )
    return pl.pallas_call(
        functools.partial(_combine_kernel, tm=tm, base_tok=base_tok),
        out_shape=jax.ShapeDtypeStruct((t, d), F32),
        grid_spec=grid_spec,
        compiler_params=_cparams(("arbitrary",), 32),
        name="moe_combine",
    )(dest_flat, x2d, wts, ys)


def _route_plan(eid, tm, n_tiles):
    flat_e = eid.reshape(-1)
    onehot = (flat_e[:, None] == jnp.arange(N_EXPERTS, dtype=jnp.int32)[None, :]).astype(jnp.int32)
    csum = jnp.cumsum(onehot, axis=0)
    rank = jnp.sum(onehot * csum, axis=1) - 1
    counts = csum[-1]
    padded = ((counts + tm - 1) // tm) * tm
    ends = jnp.cumsum(padded)
    dest = (jnp.sum(onehot * (ends - padded)[None, :], axis=1) + rank).astype(jnp.int32)
    n_active = (ends[-1] // tm).astype(jnp.int32)
    tile = jnp.arange(n_tiles, dtype=jnp.int32)
    active = tile < n_active
    tile_e = jnp.sum((ends[None, :] <= (tile * tm)[:, None]).astype(jnp.int32), axis=1)
    last_e = jnp.sum(jnp.where(tile == n_active - 1, tile_e, 0))
    tile_e = jnp.where(active, tile_e, last_e).astype(jnp.int32)
    has_pad = jnp.any((ends[None, :] == ((tile + 1) * tm)[:, None]) & (padded != counts)[None, :], axis=1)
    zflag = (has_pad | ~active).astype(jnp.int32)
    return dest, tile_e, n_active.reshape(1), zflag


def kernel(x_prompt, x_sample, cache_attn_k, cache_attn_v, cache_conv, cache_mem_k, cache_mem_v, page_table, mem_prompt, norm_mix_g, w_in, b_gate, q_norm_g, k_norm_g, lambda_q1, lambda_k1, lambda_q2, lambda_k2, subln_g, conv_w, w_out, norm_mem_g, norm_x_g, w_xq, w_xkv, xq_norm_g, xk_norm_g, w_xo, norm_ffn_g, w_group, b_group, w_expert_router, b_expert_router, w_gate_up, w_down):
    depth = w_in.shape[0]
    bp, sp, _ = x_prompt.shape
    bs, ts, _ = x_sample.shape
    tp, tsn = bp * sp, bs * ts
    slopes = jnp.exp2(-8.0 * jnp.arange(1, N_HEADS + 1, dtype=F32) / N_HEADS) * LOG2E
    xp = x_prompt.reshape(tp, D_MODEL)
    xs = x_sample.reshape(tsn, D_MODEL)
    moe_tm = 256
    moe_tiles = (tp + tsn) * 2 // moe_tm + N_EXPERTS
    outs = [[] for _ in range(8)]
    for l in range(depth):
        lam_init = 0.8 - 0.6 * math.exp(-0.3 * l)
        lam = (jnp.exp(jnp.sum(lambda_q1[l] * lambda_k1[l])) - jnp.exp(jnp.sum(lambda_q2[l] * lambda_k2[l]))
               + lam_init).reshape(1).astype(F32)
        out_scale = 1.0 - lam_init
        w_in_bf = w_in[l].astype(BF16)
        w_out_bf = w_out[l].astype(BF16)
        w_xq_bf = w_xq[l].astype(BF16)
        w_xkv_bf = w_xkv[l].astype(BF16)
        w_xo_bf = w_xo[l].astype(BF16)

        hb = _rmsnorm_cast(xp, norm_mix_g[l], 512)
        q, kf, kb, vf, vb, z, b, ga, gb = _inproj(hb, w_in_bf, q_norm_g[l], k_norm_g[l], b_gate[l], BF16, 1024, 256)
        o = _prompt_attn(q.reshape(bp, sp, D_MODEL), kb.reshape(bp, sp, D_MODEL), vb.reshape(bp, sp, D_MODEL),
                         lam, slopes, subln_g[l], out_scale, 512)
        merged = _mix_prompt(o.reshape(tp, D_MODEL), ga, gb, b, z, conv_w[l], sp, 256)
        xp = _mm_res(merged, w_out_bf, xp, 512, 1024)
        outs[0].append(kf.reshape(bp, sp, N_HEADS, HEAD_W))
        outs[1].append(vf.reshape(bp, sp, N_HEADS, V_DIM))
        outs[2].append(z.reshape(bp, sp, D_MODEL)[:, sp - 2:, :])

        hb = _rmsnorm_cast(xs, norm_mix_g[l], 512)
        q, kf, kb, vf, vb, z, b, ga, gb = _inproj(hb, w_in_bf, q_norm_g[l], k_norm_g[l], b_gate[l], F32, 1024, 256)
        shp = (bs, ts, D_MODEL)
        o = _sample_attn(q.reshape(shp), kf.reshape(shp), vf.reshape(shp), cache_attn_k[l], cache_attn_v[l],
                         page_table, lam, slopes, subln_g[l], out_scale, 4)
        z3 = z.reshape(shp)
        merged = _mix_sample(o, ga.reshape(shp), gb.reshape(shp), b.reshape(shp), z3, cache_conv[l], conv_w[l], 32)
        xs = _mm_res(merged.reshape(tsn, D_MODEL), w_out_bf, xs, 512, 1024)
        outs[5].append(kf.reshape(bs, ts, N_HEADS, HEAD_W))
        outs[6].append(vf.reshape(bs, ts, N_HEADS, V_DIM))
        outs[7].append(z3[:, ts - 2:, :])

        mem_nb = _rmsnorm_cast(mem_prompt.reshape(bp * N_MEM, D_MODEL), norm_mem_g[l], 512)
        mkf, mkb, mvf, mvb = _mem_kv(mem_nb, w_xkv_bf, xk_norm_g[l])
        xp = _xattn_prompt(xp.reshape(bp, sp, D_MODEL), norm_x_g[l], w_xq_bf, xq_norm_g[l],
                           mkb.reshape(bp, N_MEM, X_WIDTH), mvb.reshape(bp, N_MEM, X_WIDTH), w_xo_bf,
                           512).reshape(tp, D_MODEL)
        xs = _xattn_sample(xs, norm_x_g[l], w_xq_bf, xq_norm_g[l], cache_mem_k[l], cache_mem_v[l], w_xo_bf, ts, 8)
        outs[3].append(mkf.reshape(bp, N_MEM, X_HEADS, X_HEAD_DIM))
        outs[4].append(mvf.reshape(bp, N_MEM, X_HEADS, X_HEAD_DIM))

        w_r = jnp.zeros((D_MODEL, LANES), F32).at[:, :N_GROUPS].set(w_group[l])
        w_r = w_r.at[:, N_GROUPS:N_GROUPS + N_EXPERTS].set(w_expert_router[l])
        b_r = jnp.zeros((1, LANES), F32).at[0, :N_GROUPS].set(b_group[l])
        b_r = b_r.at[0, N_GROUPS:N_GROUPS + N_EXPERTS].set(b_expert_router[l])
        hf, eid, wts = _router(xp, xs, norm_ffn_g[l], w_r, b_r, 512)
        dest, tile_e, n_active, zflag = _route_plan(eid[:, :2], moe_tm, moe_tiles)
        rows = _dispatch_rows(hf, dest, zflag, moe_tiles, moe_tm, 256)
        ys = _expert_ffn(rows, tile_e, n_active, w_gate_up[l], w_down[l], moe_tm)
        xp = _combine(xp, wts, ys, dest, 0, moe_tm)
        xs = _combine(xs, wts, ys, dest, tp, moe_tm)
    stack = lambda i: jnp.stack(outs[i])
    return (xp.reshape(bp, sp, D_MODEL), xs.reshape(bs, ts, D_MODEL), stack(0), stack(1), stack(2), stack(3),
            stack(4), stack(5), stack(6), stack(7))
```

```python
import functools
import math

import jax
import jax.numpy as jnp
from jax import lax
from jax.experimental import pallas as pl
from jax.experimental.pallas import tpu as pltpu

F32 = jnp.float32
BF16 = jnp.bfloat16

D_MODEL = 2048
N_HEADS = 8
HEAD_DIM = 128
V_DIM = 256
HEAD_W = 2 * HEAD_DIM
PAGE = 128
X_HEADS = 4
X_HEAD_DIM = 128
X_WIDTH = X_HEADS * X_HEAD_DIM
N_MEM = 256
N_GROUPS = 4
EXPERTS_PER_GROUP = 8
N_EXPERTS = N_GROUPS * EXPERTS_PER_GROUP
D_FF = 512
EPS = 1e-6
NEG = -1e30
LANES = 128
MIB = 1024 * 1024
LOG2E = math.log2(math.e)


def _cparams(sem, vmem_mib):
    return pltpu.CompilerParams(dimension_semantics=sem, vmem_limit_bytes=vmem_mib * MIB)


def _rms(x, g):
    ms = jnp.mean(x * x, axis=-1, keepdims=True)
    return x * lax.rsqrt(ms + EPS) * g


def _headnorm(acc, g, scale=None):
    outs = []
    for c in range(acc.shape[1] // HEAD_DIM):
        y = _rms(acc[:, c * HEAD_DIM:(c + 1) * HEAD_DIM], g)
        outs.append(y if scale is None else y * scale)
    return jnp.concatenate(outs, axis=1)


def _norm_kernel(x_ref, g_ref, o_ref):
    o_ref[...] = _rms(x_ref[...], g_ref[...]).astype(o_ref.dtype)


def _rmsnorm_cast(x2d, g, tm):
    t, d = x2d.shape
    return pl.pallas_call(
        _norm_kernel,
        out_shape=jax.ShapeDtypeStruct((t, d), BF16),
        grid=(t // tm,),
        in_specs=[pl.BlockSpec((tm, d), lambda i: (i, 0)),
                  pl.BlockSpec((1, d), lambda i: (0, 0))],
        out_specs=pl.BlockSpec((tm, d), lambda i: (i, 0)),
        compiler_params=_cparams(("parallel",), 32),
        name="rmsnorm_cast",
    )(x2d, g.reshape(1, d))


def _inproj_kernel(h_ref, wq, wk, wv, wc, wb, wu, wga, wgb, qg_ref, kg_ref, bga_ref, bgb_ref,
                   q_o, kf_o, kb_o, vf_o, vb_o, z_o, b_o, ga_o, gb_o, *, q_scale):
    h = h_ref[...]

    def mm(w):
        return jnp.dot(h, w[...], preferred_element_type=F32)

    q_o[...] = _headnorm(mm(wq), qg_ref[...], q_scale).astype(q_o.dtype)
    k = _headnorm(mm(wk), kg_ref[...])
    kf_o[...] = k
    kb_o[...] = k.astype(BF16)
    v = mm(wv)
    vf_o[...] = v
    vb_o[...] = v.astype(BF16)
    z_o[...] = mm(wc) * mm(wu)
    b_o[...] = mm(wb)
    ga_o[...] = jax.nn.sigmoid(mm(wga) + bga_ref[...])
    gb_o[...] = jax.nn.sigmoid(mm(wgb) + bgb_ref[...])


def _inproj(hb, w_in_bf, q_g, k_g, b_gate, q_dtype, tm, tn):
    t = hb.shape[0]
    nb = D_MODEL // tn
    region = {"q": 0, "k": 1, "v": 2, "c": 3, "b": 4, "u": 5, "ga": 6, "gb": 7}

    def wspec(r):
        return pl.BlockSpec((D_MODEL, tn), lambda i, j, r=r: (0, r * nb + j))

    tile = pl.BlockSpec((tm, tn), lambda i, j: (i, j))
    vec = pl.BlockSpec((1, HEAD_DIM), lambda i, j: (0, 0))
    bias = pl.BlockSpec((1, tn), lambda i, j: (0, j))
    f32o = jax.ShapeDtypeStruct((t, D_MODEL), F32)
    bfo = jax.ShapeDtypeStruct((t, D_MODEL), BF16)
    return pl.pallas_call(
        functools.partial(_inproj_kernel, q_scale=HEAD_DIM ** -0.5 * LOG2E),
        out_shape=(jax.ShapeDtypeStruct((t, D_MODEL), q_dtype), f32o, bfo, f32o, bfo, f32o, f32o, f32o, f32o),
        grid=(t // tm, nb),
        in_specs=[pl.BlockSpec((tm, D_MODEL), lambda i, j: (i, 0))]
        + [wspec(region[n]) for n in ("q", "k", "v", "c", "b", "u", "ga", "gb")]
        + [vec, vec, bias, bias],
        out_specs=(tile,) * 9,
        compiler_params=_cparams(("parallel", "arbitrary"), 48),
        name="inproj",
    )(hb, *([w_in_bf] * 8), q_g.reshape(1, HEAD_DIM), k_g.reshape(1, HEAD_DIM),
      b_gate[0:1], b_gate[1:2])


def _lane_tiles(x):
    return [x[:, c * LANES:(c + 1) * LANES] for c in range(x.shape[1] // LANES)]


def _lane_repeat(x, n):
    return x if n == 1 else jnp.concatenate([x] * n, axis=1)


def _online_update(s_list, v_list, m_prev, l_prev, acc_prev, c_list=None):
    if c_list is None:
        c_list = [None] * len(s_list)
    tops = []
    for s, c in zip(s_list, c_list):
        top = functools.reduce(jnp.maximum, _lane_tiles(s))
        tops.append(top if c is None else top + c)
    m_cur = jnp.max(functools.reduce(jnp.maximum, tops), axis=1, keepdims=True)
    m_next = jnp.maximum(m_prev, m_cur)
    alpha = jnp.exp2(m_prev - m_next)
    l_next = alpha * l_prev
    pv = None
    for s, v, c in zip(s_list, v_list, c_list):
        shift = m_next if c is None else m_next - c
        p = jnp.exp2(s - _lane_repeat(shift, s.shape[1] // LANES))
        l_next = l_next + functools.reduce(jnp.add, _lane_tiles(p))
        d = jnp.dot(p.astype(BF16), v, preferred_element_type=F32)
        pv = d if pv is None else pv + d
    acc_next = acc_prev * _lane_repeat(alpha, V_DIM // LANES) + pv
    return m_next, l_next, acc_next


def _diff_out(a1, l1, a2, l2, lam, g, out_scale):
    r1 = 1.0 / jnp.sum(l1, axis=1, keepdims=True)
    r2 = 1.0 / jnp.sum(l2, axis=1, keepdims=True)
    o = a1 * r1 - lam * (a2 * r2)
    return _rms(o, g) * out_scale


def _init_softmax_state(m_scr, l_scr, acc_scr):
    m_scr[...] = jnp.full(m_scr.shape, NEG, F32)
    l_scr[...] = jnp.zeros(l_scr.shape, F32)
    acc_scr[...] = jnp.zeros(acc_scr.shape, F32)


def _pattn_kernel(qi_ref, ki_ref, lam_ref, slopes_ref, q_ref, k_ref, v_ref, g_ref, o_ref, m_scr, l_scr, acc_scr,
                  *, t, hp, out_scale):
    h = pl.program_id(1)
    step = pl.program_id(2)
    qi = qi_ref[step]
    ki = ki_ref[step]

    @pl.when(ki == 0)
    def _():
        _init_softmax_state(m_scr, l_scr, acc_scr)

    def body(masked):
        kpos = ((ki - qi) * t + lax.broadcasted_iota(jnp.int32, (1, t), 1)).astype(F32)
        if masked:
            keep = (lax.broadcasted_iota(jnp.int32, (t, t), 1)
                    <= lax.broadcasted_iota(jnp.int32, (t, t), 0))
        state = []
        for i in range(hp):
            bias = slopes_ref[h * hp + i] * kpos
            v = v_ref[0, :, i * V_DIM:(i + 1) * V_DIM]
            for m in range(2):
                cols = slice(i * HEAD_W + m * HEAD_DIM, i * HEAD_W + (m + 1) * HEAD_DIM)
                s = lax.dot_general(q_ref[0, :, cols], k_ref[0, :, cols], (((1,), (1,)), ((), ())),
                                    preferred_element_type=F32)
                s = s + bias
                if masked:
                    s = jnp.where(keep, s, NEG)
                idx = 2 * i + m
                state.append(_online_update([s], [v], m_scr[idx], l_scr[idx], acc_scr[idx]))
        return state

    @pl.when(ki < qi)
    def _():
        for idx, (m_next, l_next, acc_next) in enumerate(body(False)):
            m_scr[idx] = m_next
            l_scr[idx] = l_next
            acc_scr[idx] = acc_next

    @pl.when(ki == qi)
    def _():
        state = body(True)
        lam = lam_ref[0]
        g = g_ref[...]
        o_ref[0] = jnp.concatenate(
            [_diff_out(state[2 * i][2], state[2 * i][1], state[2 * i + 1][2], state[2 * i + 1][1], lam, g, out_scale)
             for i in range(hp)], axis=1)


def _prompt_attn(qb, kb, vb, lam, slopes2, subln_g, out_scale, t, hp):
    b, s, _ = qb.shape
    n = s // t
    pairs = [(qi, ki) for qi in range(n) for ki in range(qi + 1)]
    qi_tab = jnp.asarray([p[0] for p in pairs], jnp.int32)
    ki_tab = jnp.asarray([p[1] for p in pairs], jnp.int32)
    smem = pl.BlockSpec(memory_space=pltpu.SMEM)
    grid_spec = pltpu.PrefetchScalarGridSpec(
        num_scalar_prefetch=2,
        grid=(b, N_HEADS // hp, len(pairs)),
        in_specs=[smem, smem,
                  pl.BlockSpec((1, t, hp * HEAD_W), lambda bi, h, st, qt, kt: (bi, qt[st], h)),
                  pl.BlockSpec((1, t, hp * HEAD_W), lambda bi, h, st, qt, kt: (bi, kt[st], h)),
                  pl.BlockSpec((1, t, hp * V_DIM), lambda bi, h, st, qt, kt: (bi, kt[st], h)),
                  pl.BlockSpec((1, V_DIM), lambda bi, h, st, qt, kt: (0, 0))],
        out_specs=pl.BlockSpec((1, t, hp * V_DIM), lambda bi, h, st, qt, kt: (bi, qt[st], h)),
        scratch_shapes=[pltpu.VMEM((2 * hp, t, LANES), F32), pltpu.VMEM((2 * hp, t, LANES), F32),
                        pltpu.VMEM((2 * hp, t, V_DIM), F32)],
    )
    return pl.pallas_call(
        functools.partial(_pattn_kernel, t=t, hp=hp, out_scale=out_scale),
        out_shape=jax.ShapeDtypeStruct((b, s, D_MODEL), F32),
        grid_spec=grid_spec,
        compiler_params=_cparams(("parallel", "parallel", "arbitrary"), 40),
        name="prompt_attn",
    )(qi_tab, ki_tab, lam, slopes2, qb, kb, vb, subln_g.reshape(1, V_DIM))


def _sattn_bias(slopes2, tq, n_tok, n_lanes, causal):
    rows = jnp.arange(2 * N_HEADS * tq)
    row_h = (rows // tq) % N_HEADS
    row_q = rows % tq
    lanes = jnp.arange(n_lanes)
    tok = lanes // N_HEADS
    ok = (row_h[:, None] == (lanes % N_HEADS)[None, :]) & (tok < n_tok)[None, :]
    if causal:
        ok = ok & (tok[None, :] <= row_q[:, None])
    return jnp.where(ok, slopes2[row_h][:, None] * tok[None, :].astype(F32), NEG).astype(F32)


def _sattn_kernel(pt_ref, lam_ref, q_ref, kn_ref, vn_ref, *rest, npg, n_steps, past, out_scale):
    del pt_ref
    k_refs = rest[:npg]
    v_refs = rest[npg:2 * npg]
    bias_ref, bias_new_ref, slope_ref, g_ref, o_ref, qbd_scr, m_scr, l_scr, acc_scr = rest[2 * npg:]
    p = pl.program_id(1)
    tq = q_ref.shape[1]
    half = N_HEADS * tq

    @pl.when(p == 0)
    def _():
        _init_softmax_state(m_scr, l_scr, acc_scr)
        q = q_ref[0]
        zero = jnp.zeros((half, HEAD_DIM), F32)
        q1, q2 = [jnp.concatenate([q[:, h * HEAD_W + m * HEAD_DIM:h * HEAD_W + (m + 1) * HEAD_DIM]
                                   for h in range(N_HEADS)], axis=0) for m in range(2)]
        qbd_scr[...] = jnp.concatenate([jnp.concatenate([q1, zero], axis=1),
                                        jnp.concatenate([zero, q2], axis=1)], axis=0).astype(BF16)

    nt = (((1,), (1,)), ((), ()))

    def body(last):
        qbd = qbd_scr[...]
        bias = bias_ref[...]
        slope = slope_ref[...]
        s_list, v_list, c_list = [], [], []
        for j in range(npg):
            s = lax.dot_general(qbd, k_refs[j][0].astype(BF16), nt, preferred_element_type=F32)
            s_list.append(s + bias)
            v_list.append(v_refs[j][0].astype(BF16))
            c_list.append(slope * ((p * npg + j) * PAGE - past).astype(F32))
        if last:
            pad = jnp.zeros((LANES - kn_ref.shape[1], HEAD_W), F32)
            kn = jnp.concatenate([kn_ref[0], pad], axis=0).astype(BF16)
            s = lax.dot_general(qbd, kn, nt, preferred_element_type=F32)
            s_list.append(s + bias_new_ref[...])
            v_list.append(jnp.concatenate([vn_ref[0], pad], axis=0).astype(BF16))
            c_list.append(None)
        return _online_update(s_list, v_list, m_scr[...], l_scr[...], acc_scr[...], c_list)

    @pl.when(p < n_steps - 1)
    def _():
        m_next, l_next, acc_next = body(False)
        m_scr[...] = m_next
        l_scr[...] = l_next
        acc_scr[...] = acc_next

    @pl.when(p == n_steps - 1)
    def _():
        _, l, acc = body(True)
        y = _diff_out(acc[:half], l[:half], acc[half:], l[half:], lam_ref[0], g_ref[...], out_scale)
        o_ref[0] = jnp.concatenate([y[h * tq:(h + 1) * tq] for h in range(N_HEADS)], axis=1)


def _sample_attn(qs, kn4, vn4, pool_k, pool_v, page_table, lam, slopes2, subln_g, out_scale, npg):
    db, tq, _ = qs.shape
    n_pages = page_table.shape[1]
    n_steps = n_pages // npg
    n_phys = pool_k.shape[0]
    rows = 2 * N_HEADS * tq
    page_rows = PAGE * N_HEADS
    pk = pool_k.reshape(n_phys, page_rows, HEAD_W)
    pv = pool_v.reshape(n_phys, page_rows, V_DIM)
    kn = kn4.reshape(db, tq * N_HEADS, HEAD_W)
    vn = vn4.reshape(db, tq * N_HEADS, V_DIM)
    bias = _sattn_bias(slopes2, tq, PAGE, page_rows, False)
    bias_new = _sattn_bias(slopes2, tq, tq, LANES, True)
    slope_rep = jnp.broadcast_to(slopes2[(jnp.arange(rows) // tq) % N_HEADS][:, None], (rows, LANES))
    smem = pl.BlockSpec(memory_space=pltpu.SMEM)
    tok = pl.BlockSpec((1, tq, D_MODEL), lambda b, p, pt: (b, 0, 0))
    new = pl.BlockSpec((1, tq * N_HEADS, HEAD_W), lambda b, p, pt: (b, 0, 0))
    const = lambda shp: pl.BlockSpec(shp, lambda b, p, pt: (0, 0))

    def page(j):
        return pl.BlockSpec((1, page_rows, HEAD_W), lambda b, p, pt, j=j: (pt[b, p * npg + j], 0, 0))

    grid_spec = pltpu.PrefetchScalarGridSpec(
        num_scalar_prefetch=1,
        grid=(db, n_steps),
        in_specs=[smem, tok, new, new] + [page(j) for j in range(npg)] * 2
        + [const((rows, page_rows)), const((rows, LANES)), const((rows, LANES)), const((1, V_DIM))],
        out_specs=tok,
        scratch_shapes=[pltpu.VMEM((rows, HEAD_W), BF16), pltpu.VMEM((rows, LANES), F32),
                        pltpu.VMEM((rows, LANES), F32), pltpu.VMEM((rows, V_DIM), F32)],
    )
    return pl.pallas_call(
        functools.partial(_sattn_kernel, npg=npg, n_steps=n_steps, past=n_pages * PAGE, out_scale=out_scale),
        out_shape=jax.ShapeDtypeStruct((db, tq, D_MODEL), F32),
        grid_spec=grid_spec,
        compiler_params=_cparams(("parallel", "arbitrary"), 48),
        name="sample_attn",
    )(page_table, lam, qs, kn, vn, *([pk] * npg), *([pv] * npg), bias, bias_new, slope_rep,
      subln_g.reshape(1, V_DIM))


def _conv_merge(o, ga, gb, b, z, z1, z2, cw):
    y = z2 * cw[0] + z1 * cw[1] + z * cw[2]
    return (ga * o + gb * (b * y)).astype(BF16)


def _mix_prompt_kernel(o_ref, ga_ref, gb_ref, b_ref, z_ref, zp_ref, cw_ref, out_ref, *, tiles_per_seq):
    i = pl.program_id(0)
    z = z_ref[...]
    tm = z.shape[0]
    zp = jnp.where(i % tiles_per_seq == 0, 0.0, zp_ref[...])
    pm1 = zp[7:8, :]
    pm2 = zp[6:7, :]
    row = lax.broadcasted_iota(jnp.int32, (tm, 1), 0)
    z1 = jnp.where(row == 0, pm1, pltpu.roll(z, 1, 0))
    z2 = jnp.where(row == 0, pm2, jnp.where(row == 1, pm1, pltpu.roll(z, 2, 0)))
    cw = cw_ref[...]
    out_ref[...] = _conv_merge(o_ref[...], ga_ref[...], gb_ref[...], b_ref[...], z, z1, z2,
                               (cw[0:1], cw[1:2], cw[2:3]))


def _mix_prompt(o, ga, gb, b, z, conv_w, seq, tm):
    t = z.shape[0]
    tile = pl.BlockSpec((tm, D_MODEL), lambda i: (i, 0))
    prev = pl.BlockSpec((8, D_MODEL), lambda i: (jnp.maximum(i * (tm // 8) - 1, 0), 0))
    return pl.pallas_call(
        functools.partial(_mix_prompt_kernel, tiles_per_seq=seq // tm),
        out_shape=jax.ShapeDtypeStruct((t, D_MODEL), BF16),
        grid=(t // tm,),
        in_specs=[tile, tile, tile, tile, tile, prev, pl.BlockSpec((3, D_MODEL), lambda i: (0, 0))],
        out_specs=tile,
        compiler_params=_cparams(("parallel",), 40),
        name="mix_prompt",
    )(o, ga, gb, b, z, z, conv_w)


def _mix_sample_kernel(o_ref, ga_ref, gb_ref, b_ref, z_ref, cc_ref, cw_ref, out_ref):
    z = z_ref[...]
    cc = cc_ref[...]
    c0 = cc[:, 0:1, :]
    c1 = cc[:, 1:2, :]
    row = lax.broadcasted_iota(jnp.int32, (1, z.shape[1], 1), 1)
    z1 = jnp.where(row == 0, c1, pltpu.roll(z, 1, 1))
    z2 = jnp.where(row == 0, c0, jnp.where(row == 1, c1, pltpu.roll(z, 2, 1)))
    cw = cw_ref[...]
    out_ref[...] = _conv_merge(o_ref[...], ga_ref[...], gb_ref[...], b_ref[...], z, z1, z2,
                               (cw[0:1][None], cw[1:2][None], cw[2:3][None]))


def _mix_sample(o, ga, gb, b, z, cache_conv, conv_w, g):
    db, ts, _ = z.shape
    tile = pl.BlockSpec((g, ts, D_MODEL), lambda i: (i, 0, 0))
    return pl.pallas_call(
        _mix_sample_kernel,
        out_shape=jax.ShapeDtypeStruct((db, ts, D_MODEL), BF16),
        grid=(db // g,),
        in_specs=[tile, tile, tile, tile, tile,
                  pl.BlockSpec((g, cache_conv.shape[1], D_MODEL), lambda i: (i, 0, 0)),
                  pl.BlockSpec((3, D_MODEL), lambda i: (0, 0))],
        out_specs=tile,
        compiler_params=_cparams(("parallel",), 40),
        name="mix_sample",
    )(o, ga, gb, b, z, cache_conv, conv_w)


def _mm_res_kernel(a_ref, w_ref, r_ref, o_ref):
    o_ref[...] = r_ref[...] + jnp.dot(a_ref[...], w_ref[...], preferred_element_type=F32)


def _mm_res(a, w, r, tm, tn):
    t, k = a.shape
    n = w.shape[1]
    return pl.pallas_call(
        _mm_res_kernel,
        out_shape=jax.ShapeDtypeStruct((t, n), F32),
        grid=(t // tm, n // tn),
        in_specs=[pl.BlockSpec((tm, k), lambda i, j: (i, 0)),
                  pl.BlockSpec((k, tn), lambda i, j: (0, j)),
                  pl.BlockSpec((tm, tn), lambda i, j: (i, j))],
        out_specs=pl.BlockSpec((tm, tn), lambda i, j: (i, j)),
        compiler_params=_cparams(("parallel", "parallel"), 48),
        name="outproj",
    )(a, w, r)


def _memkv_kernel(h_ref, wk_ref, wv_ref, g_ref, kf_o, kb_o, vf_o, vb_o):
    h = h_ref[...]
    k = _headnorm(jnp.dot(h, wk_ref[...], preferred_element_type=F32), g_ref[...])
    kf_o[...] = k
    kb_o[...] = k.astype(BF16)
    v = jnp.dot(h, wv_ref[...], preferred_element_type=F32)
    vf_o[...] = v
    vb_o[...] = v.astype(BF16)


def _mem_kv(mem_nb, w_xkv_bf, xk_g):
    t = mem_nb.shape[0]
    f32o = jax.ShapeDtypeStruct((t, X_WIDTH), F32)
    bfo = jax.ShapeDtypeStruct((t, X_WIDTH), BF16)
    full = lambda s: pl.BlockSpec(s, lambda i: (0,) * len(s))
    return pl.pallas_call(
        _memkv_kernel,
        out_shape=(f32o, bfo, f32o, bfo),
        grid=(1,),
        in_specs=[full((t, D_MODEL)),
                  pl.BlockSpec((D_MODEL, X_WIDTH), lambda i: (0, 0)),
                  pl.BlockSpec((D_MODEL, X_WIDTH), lambda i: (0, 1)),
                  full((1, X_HEAD_DIM))],
        out_specs=(full((t, X_WIDTH)),) * 4,
        compiler_params=_cparams(("arbitrary",), 32),
        name="mem_kv",
    )(mem_nb, w_xkv_bf, w_xkv_bf, xk_g.reshape(1, X_HEAD_DIM))


def _xattn_head(qh, kh, vh):
    s = lax.dot_general(qh, kh, (((1,), (1,)), ((), ())), preferred_element_type=F32)
    e = jnp.exp(s - jnp.max(s, axis=1, keepdims=True))
    p = e * (1.0 / jnp.sum(e, axis=1, keepdims=True))
    return jnp.dot(p.astype(BF16), vh, preferred_element_type=F32)


def _xattn_q(x, g_ref, wq_ref, qg_ref):
    hn = _rms(x, g_ref[...]).astype(BF16)
    q = jnp.dot(hn, wq_ref[...], preferred_element_type=F32)
    return _headnorm(q, qg_ref[...], X_HEAD_DIM ** -0.5)


def _xattn_prompt_kernel(x_ref, g_ref, wq_ref, qg_ref, mk_ref, mv_ref, wo_ref, o_ref):
    x = x_ref[0]
    q = _xattn_q(x, g_ref, wq_ref, qg_ref).astype(BF16)
    mk = mk_ref[0]
    mv = mv_ref[0]
    heads = []
    for g in range(X_HEADS):
        sl = slice(g * X_HEAD_DIM, (g + 1) * X_HEAD_DIM)
        heads.append(_xattn_head(q[:, sl], mk[:, sl], mv[:, sl]))
    o = jnp.concatenate(heads, axis=1).astype(BF16)
    o_ref[0] = x + jnp.dot(o, wo_ref[...], preferred_element_type=F32)


def _xattn_prompt(x, g, w_xq_bf, xq_g, mk_b, mv_b, w_xo_bf, tm):
    b, s, _ = x.shape
    tile = pl.BlockSpec((1, tm, D_MODEL), lambda bi, i: (bi, i, 0))
    mem = pl.BlockSpec((1, N_MEM, X_WIDTH), lambda bi, i: (bi, 0, 0))
    const = lambda shp: pl.BlockSpec(shp, lambda bi, i: (0, 0))
    return pl.pallas_call(
        _xattn_prompt_kernel,
        out_shape=jax.ShapeDtypeStruct((b, s, D_MODEL), F32),
        grid=(b, s // tm),
        in_specs=[tile, const((1, D_MODEL)), const((D_MODEL, X_WIDTH)), const((1, X_HEAD_DIM)),
                  mem, mem, const((X_WIDTH, D_MODEL))],
        out_specs=tile,
        compiler_params=_cparams(("parallel", "parallel"), 48),
        name="xattn_prompt",
    )(x, g.reshape(1, D_MODEL), w_xq_bf, xq_g.reshape(1, X_HEAD_DIM), mk_b, mv_b, w_xo_bf)


def _xattn_sample_kernel(x_ref, g_ref, wq_ref, qg_ref, mk_ref, mv_ref, mask_ref, wo_ref, o_ref, *, ts):
    x = x_ref[...]
    q = _xattn_q(x, g_ref, wq_ref, qg_ref)
    mask = mask_ref[...]
    nt = (((1,), (1,)), ((), ()))
    per_batch = []
    for bi in range(mk_ref.shape[0]):
        rows = slice(bi * ts, (bi + 1) * ts)
        qs = jnp.concatenate([q[rows, g * X_HEAD_DIM:(g + 1) * X_HEAD_DIM] for g in range(X_HEADS)], axis=0)
        s = lax.dot_general(qs.astype(BF16), mk_ref[bi].astype(BF16), nt, preferred_element_type=F32) + mask
        e = jnp.exp(s - jnp.max(s, axis=1, keepdims=True))
        o = jnp.dot(e.astype(BF16), mv_ref[bi].astype(BF16), preferred_element_type=F32)
        o = o * (1.0 / jnp.sum(e, axis=1, keepdims=True))
        per_batch.append(jnp.concatenate([o[g * ts:(g + 1) * ts] for g in range(X_HEADS)], axis=1))
    a = jnp.concatenate(per_batch, axis=0).astype(BF16)
    o_ref[...] = x + jnp.dot(a, wo_ref[...], preferred_element_type=F32)


def _xattn_sample(x2d, g, w_xq_bf, xq_g, cache_mk, cache_mv, w_xo_bf, ts, gb):
    t = x2d.shape[0]
    db = cache_mk.shape[0]
    mk = cache_mk.reshape(db, N_MEM * X_HEADS, X_HEAD_DIM)
    mv = cache_mv.reshape(db, N_MEM * X_HEADS, X_HEAD_DIM)
    tile = pl.BlockSpec((gb * ts, D_MODEL), lambda i: (i, 0))
    mem = pl.BlockSpec((gb, N_MEM * X_HEADS, X_HEAD_DIM), lambda i: (i, 0, 0))
    const = lambda shp: pl.BlockSpec(shp, lambda i: (0, 0))
    row_head = jnp.arange(X_HEADS * ts) // ts
    lane_head = jnp.arange(N_MEM * X_HEADS) % X_HEADS
    mask = jnp.where(row_head[:, None] == lane_head[None, :], 0.0, NEG).astype(F32)
    return pl.pallas_call(
        functools.partial(_xattn_sample_kernel, ts=ts),
        out_shape=jax.ShapeDtypeStruct((t, D_MODEL), F32),
        grid=(db // gb,),
        in_specs=[tile, const((1, D_MODEL)), const((D_MODEL, X_WIDTH)), const((1, X_HEAD_DIM)),
                  mem, mem, const((X_HEADS * ts, N_MEM * X_HEADS)), const((X_WIDTH, D_MODEL))],
        out_specs=tile,
        compiler_params=_cparams(("parallel",), 48),
        name="xattn_sample",
    )(x2d, g.reshape(1, D_MODEL), w_xq_bf, xq_g.reshape(1, X_HEAD_DIM), mk, mv, mask, w_xo_bf)


def _first_argmax(v, lane):
    vmax = jnp.max(v, axis=1, keepdims=True)
    idx = jnp.min(jnp.where(v == vmax, lane, LANES), axis=1, keepdims=True)
    return vmax, idx


def _router_kernel(xp_ref, xs_ref, g_ref, wr_ref, br_ref, h_o, eid_o, w_o, *, n_prompt_tiles):
    i = pl.program_id(0)

    def body(x_ref):
        h = _rms(x_ref[...], g_ref[...])
        h_o[...] = h
        logits = jnp.dot(h, wr_ref[...], preferred_element_type=F32,
                         precision=lax.Precision.HIGHEST) + br_ref[...]
        lane = lax.broadcasted_iota(jnp.int32, logits.shape, 1)
        ninf = -jnp.inf
        gl = jnp.where(lane < N_GROUPS, logits, ninf)
        gmax, gidx = _first_argmax(gl, lane)
        g_w = 1.0 / jnp.sum(jnp.exp(gl - gmax), axis=1, keepdims=True)
        lo = N_GROUPS + gidx * EXPERTS_PER_GROUP
        el = jnp.where((lane >= lo) & (lane < lo + EXPERTS_PER_GROUP), logits, ninf)
        v1, i1 = _first_argmax(el, lane)
        v2, i2 = _first_argmax(jnp.where(lane == i1, ninf, el), lane)
        e2 = jnp.exp(v2 - v1)
        w1 = g_w / (1.0 + e2)
        w2 = g_w * e2 / (1.0 + e2)
        eid_o[...] = jnp.where(lane == 0, i1 - N_GROUPS, jnp.where(lane == 1, i2 - N_GROUPS, 0))
        w_o[...] = jnp.where(lane == 0, w1, jnp.where(lane == 1, w2, 0.0))

    @pl.when(i < n_prompt_tiles)
    def _():
        body(xp_ref)

    @pl.when(i >= n_prompt_tiles)
    def _():
        body(xs_ref)


def _router(xp2d, xs2d, g, w_r, b_r, tm):
    tp, ts = xp2d.shape[0], xs2d.shape[0]
    npt, nst = tp // tm, ts // tm
    t = tp + ts
    tile = pl.BlockSpec((tm, D_MODEL), lambda i: (i, 0))
    const = lambda shp: pl.BlockSpec(shp, lambda i: (0, 0))
    return pl.pallas_call(
        functools.partial(_router_kernel, n_prompt_tiles=npt),
        out_shape=(jax.ShapeDtypeStruct((t, D_MODEL), F32),
                   jax.ShapeDtypeStruct((t, LANES), jnp.int32),
                   jax.ShapeDtypeStruct((t, LANES), F32)),
        grid=(npt + nst,),
        in_specs=[pl.BlockSpec((tm, D_MODEL), lambda i: (jnp.minimum(i, npt - 1), 0)),
                  pl.BlockSpec((tm, D_MODEL), lambda i: (jnp.maximum(i - npt, 0), 0)),
                  const((1, D_MODEL)), const((D_MODEL, LANES)), const((1, LANES))],
        out_specs=(tile, pl.BlockSpec((tm, LANES), lambda i: (i, 0)), pl.BlockSpec((tm, LANES), lambda i: (i, 0))),
        compiler_params=_cparams(("parallel",), 40),
        name="moe_router",
    )(xp2d, xs2d, g.reshape(1, D_MODEL), w_r, b_r)


def _dispatch_kernel(dest_ref, zflag_ref, h_ref, rows_hbm, zero_scr, sem, *, tmd, tm, n_tiles):
    i = pl.program_id(0)

    def zero_tile(t, start):
        @pl.when(zflag_ref[t] != 0)
        def _():
            cp = pltpu.make_async_copy(zero_scr, rows_hbm.at[pl.ds(pl.multiple_of(t * tm, tm), tm)], sem.at[2])
            if start:
                cp.start()
            else:
                cp.wait()

    @pl.when(i == 0)
    def _():
        zero_scr[...] = jnp.zeros(zero_scr.shape, F32)
        lax.fori_loop(0, n_tiles, lambda t, c: (zero_tile(t, True), c)[1], 0)
        lax.fori_loop(0, n_tiles, lambda t, c: (zero_tile(t, False), c)[1], 0)

    base = 2 * i * tmd

    def row_copy(r, slot):
        return pltpu.make_async_copy(h_ref.at[pl.ds(r, 1)], rows_hbm.at[pl.ds(dest_ref[base + 2 * r + slot], 1)],
                                     sem.at[slot])

    def start(r, carry):
        row_copy(r, 0).start()
        row_copy(r, 1).start()
        return carry

    lax.fori_loop(0, tmd, start, 0)
    for slot in range(2):
        pltpu.make_async_copy(h_ref, rows_hbm.at[pl.ds(0, tmd)], sem.at[slot]).wait()


def _dispatch_rows(hf, dest_flat, zflag, n_tiles, tm, tmd):
    t, d = hf.shape
    grid_spec = pltpu.PrefetchScalarGridSpec(
        num_scalar_prefetch=2,
        grid=(t // tmd,),
        in_specs=[pl.BlockSpec((tmd, d), lambda i, dest, zf: (i, 0))],
        out_specs=pl.BlockSpec(memory_space=pl.ANY),
        scratch_shapes=[pltpu.VMEM((tm, d), F32), pltpu.SemaphoreType.DMA((3,))],
    )
    return pl.pallas_call(
        functools.partial(_dispatch_kernel, tmd=tmd, tm=tm, n_tiles=n_tiles),
        out_shape=jax.ShapeDtypeStruct((n_tiles * tm, d), F32),
        grid_spec=grid_spec,
        compiler_params=_cparams(("arbitrary",), 32),
        name="moe_dispatch",
    )(dest_flat, zflag, hf)


def _ffn_kernel(te_ref, na_ref, x_ref, wgu_ref, wd_ref, o_ref, wgu_scr, wd_scr):
    t = pl.program_id(0)

    @pl.when(t < na_ref[0])
    def _():
        e = te_ref[t]
        e_prev = te_ref[jnp.maximum(t - 1, 0)]

        @pl.when((t == 0) | (e != e_prev))
        def _():
            wgu_scr[...] = wgu_ref[0].astype(BF16)
            wd_scr[...] = wd_ref[0].astype(BF16)

        gu = jnp.dot(x_ref[...].astype(BF16), wgu_scr[...], preferred_element_type=F32)
        gate = gu[:, :D_FF]
        act = (gate * jax.nn.sigmoid(gate)) * gu[:, D_FF:]
        o_ref[...] = jnp.dot(act.astype(BF16), wd_scr[...], preferred_element_type=F32)

    @pl.when(t >= na_ref[0])
    def _():
        o_ref[...] = jnp.zeros(o_ref.shape, F32)


def _expert_ffn(xs, tile_e, n_active, w_gu, w_down, tm):
    n_tiles = xs.shape[0] // tm
    grid_spec = pltpu.PrefetchScalarGridSpec(
        num_scalar_prefetch=2,
        grid=(n_tiles,),
        in_specs=[pl.BlockSpec((tm, D_MODEL), lambda t, te, na: (jnp.minimum(t, na[0] - 1), 0)),
                  pl.BlockSpec((1, D_MODEL, 2 * D_FF), lambda t, te, na: (te[t], 0, 0)),
                  pl.BlockSpec((1, D_FF, D_MODEL), lambda t, te, na: (te[t], 0, 0))],
        out_specs=pl.BlockSpec((tm, D_MODEL), lambda t, te, na: (t, 0)),
        scratch_shapes=[pltpu.VMEM((D_MODEL, 2 * D_FF), BF16), pltpu.VMEM((D_FF, D_MODEL), BF16)],
    )
    return pl.pallas_call(
        _ffn_kernel,
        out_shape=jax.ShapeDtypeStruct((n_tiles * tm, D_MODEL), F32),
        grid_spec=grid_spec,
        compiler_params=_cparams(("arbitrary",), 48),
        name="moe_ffn",
    )(tile_e, n_active, xs, w_gu, w_down)


def _combine_kernel(dest_ref, x_ref, w_ref, ys_hbm, o_ref, buf_a, buf_b, sem, *, tm, base_tok):
    base = 2 * (base_tok + pl.program_id(0) * tm)
    bufs = (buf_a, buf_b)

    def start(r, carry):
        for slot in range(2):
            pltpu.make_async_copy(ys_hbm.at[pl.ds(dest_ref[base + 2 * r + slot], 1)], bufs[slot].at[pl.ds(r, 1)],
                                  sem.at[slot]).start()
        return carry

    lax.fori_loop(0, tm, start, 0)
    for slot in range(2):
        pltpu.make_async_copy(ys_hbm.at[pl.ds(0, tm)], bufs[slot], sem.at[slot]).wait()
    w = w_ref[...]
    o_ref[...] = x_ref[...] + (w[:, 0:1] * buf_a[...] + w[:, 1:2] * buf_b[...])


def _combine(x2d, wts, ys, dest_flat, base_tok, tm):
    t, d = x2d.shape
    base_blk = base_tok // tm
    grid_spec = pltpu.PrefetchScalarGridSpec(
        num_scalar_prefetch=1,
        grid=(t // tm,),
        in_specs=[pl.BlockSpec((tm, d), lambda i, dest: (i, 0)),
                  pl.BlockSpec((tm, LANES), lambda i, dest: (i + base_blk, 0)),
                  pl.BlockSpec(memory_space=pl.ANY)],
        out_specs=pl.BlockSpec((tm, d), lambda i, dest: (i, 0)),
        scratch_shapes=[pltpu.VMEM((tm, d), F32), pltpu.VMEM((tm, d), F32), pltpu.SemaphoreType.DMA((2,))],
    )
    return pl.pallas_call(
        functools.partial(_combine_kernel, tm=tm, base_tok=base_tok),
        out_shape=jax.ShapeDtypeStruct((t, d), F32),
        grid_spec=grid_spec,
        compiler_params=_cparams(("arbitrary",), 32),
        name="moe_combine",
    )(dest_flat, x2d, wts, ys)


def _route_plan(eid, tm, n_tiles):
    flat_e = eid.reshape(-1)
    onehot = (flat_e[:, None] == jnp.arange(N_EXPERTS, dtype=jnp.int32)[None, :]).astype(jnp.int32)
    csum = jnp.cumsum(onehot, axis=0)
    rank = jnp.sum(onehot * csum, axis=1) - 1
    counts = csum[-1]
    padded = ((counts + tm - 1) // tm) * tm
    ends = jnp.cumsum(padded)
    dest = (jnp.sum(onehot * (ends - padded)[None, :], axis=1) + rank).astype(jnp.int32)
    n_active = (ends[-1] // tm).astype(jnp.int32)
    tile = jnp.arange(n_tiles, dtype=jnp.int32)
    active = tile < n_active
    tile_e = jnp.sum((ends[None, :] <= (tile * tm)[:, None]).astype(jnp.int32), axis=1)
    last_e = jnp.sum(jnp.where(tile == n_active - 1, tile_e, 0))
    tile_e = jnp.where(active, tile_e, last_e).astype(jnp.int32)
    has_pad = jnp.any((ends[None, :] == ((tile + 1) * tm)[:, None]) & (padded != counts)[None, :], axis=1)
    zflag = (has_pad | ~active).astype(jnp.int32)
    return dest, tile_e, n_active.reshape(1), zflag


def kernel(x_prompt, x_sample, cache_attn_k, cache_attn_v, cache_conv, cache_mem_k, cache_mem_v, page_table, mem_prompt, norm_mix_g, w_in, b_gate, q_norm_g, k_norm_g, lambda_q1, lambda_k1, lambda_q2, lambda_k2, subln_g, conv_w, w_out, norm_mem_g, norm_x_g, w_xq, w_xkv, xq_norm_g, xk_norm_g, w_xo, norm_ffn_g, w_group, b_group, w_expert_router, b_expert_router, w_gate_up, w_down):
    depth = w_in.shape[0]
    bp, sp, _ = x_prompt.shape
    bs, ts, _ = x_sample.shape
    tp, tsn = bp * sp, bs * ts
    slopes = jnp.exp2(-8.0 * jnp.arange(1, N_HEADS + 1, dtype=F32) / N_HEADS) * LOG2E
    xp = x_prompt.reshape(tp, D_MODEL)
    xs = x_sample.reshape(tsn, D_MODEL)
    moe_tm = 256
    moe_tiles = (tp + tsn) * 2 // moe_tm + N_EXPERTS
    outs = [[] for _ in range(8)]
    for l in range(depth):
        lam_init = 0.8 - 0.6 * math.exp(-0.3 * l)
        lam = (jnp.exp(jnp.sum(lambda_q1[l] * lambda_k1[l])) - jnp.exp(jnp.sum(lambda_q2[l] * lambda_k2[l]))
               + lam_init).reshape(1).astype(F32)
        out_scale = 1.0 - lam_init
        w_in_bf = w_in[l].astype(BF16)
        w_out_bf = w_out[l].astype(BF16)
        w_xq_bf = w_xq[l].astype(BF16)
        w_xkv_bf = w_xkv[l].astype(BF16)
        w_xo_bf = w_xo[l].astype(BF16)

        hb = _rmsnorm_cast(xp, norm_mix_g[l], 512)
        q, kf, kb, vf, vb, z, b, ga, gb = _inproj(hb, w_in_bf, q_norm_g[l], k_norm_g[l], b_gate[l], BF16, 1024, 256)
        o = _prompt_attn(q.reshape(bp, sp, D_MODEL), kb.reshape(bp, sp, D_MODEL), vb.reshape(bp, sp, D_MODEL),
                         lam, slopes, subln_g[l], out_scale, 512, 4)
        merged = _mix_prompt(o.reshape(tp, D_MODEL), ga, gb, b, z, conv_w[l], sp, 256)
        xp = _mm_res(merged, w_out_bf, xp, 512, 1024)
        outs[0].append(kf.reshape(bp, sp, N_HEADS, HEAD_W))
        outs[1].append(vf.reshape(bp, sp, N_HEADS, V_DIM))
        outs[2].append(z.reshape(bp, sp, D_MODEL)[:, sp - 2:, :])

        hb = _rmsnorm_cast(xs, norm_mix_g[l], 512)
        q, kf, kb, vf, vb, z, b, ga, gb = _inproj(hb, w_in_bf, q_norm_g[l], k_norm_g[l], b_gate[l], F32, 1024, 256)
        shp = (bs, ts, D_MODEL)
        kn4 = kf.reshape(bs, ts, N_HEADS, HEAD_W)
        vn4 = vf.reshape(bs, ts, N_HEADS, V_DIM)
        o = _sample_attn(q.reshape(shp), kn4, vn4, cache_attn_k[l], cache_attn_v[l],
                         page_table, lam, slopes, subln_g[l], out_scale, 4)
        z3 = z.reshape(shp)
        merged = _mix_sample(o, ga.reshape(shp), gb.reshape(shp), b.reshape(shp), z3, cache_conv[l], conv_w[l], 32)
        xs = _mm_res(merged.reshape(tsn, D_MODEL), w_out_bf, xs, 512, 1024)
        outs[5].append(kn4)
        outs[6].append(vn4)
        outs[7].append(z3[:, ts - 2:, :])

        mem_nb = _rmsnorm_cast(mem_prompt.reshape(bp * N_MEM, D_MODEL), norm_mem_g[l], 512)
        mkf, mkb, mvf, mvb = _mem_kv(mem_nb, w_xkv_bf, xk_norm_g[l])
        xp = _xattn_prompt(xp.reshape(bp, sp, D_MODEL), norm_x_g[l], w_xq_bf, xq_norm_g[l],
                           mkb.reshape(bp, N_MEM, X_WIDTH), mvb.reshape(bp, N_MEM, X_WIDTH), w_xo_bf,
                           512).reshape(tp, D_MODEL)
        xs = _xattn_sample(xs, norm_x_g[l], w_xq_bf, xq_norm_g[l], cache_mem_k[l], cache_mem_v[l], w_xo_bf, ts, 8)
        outs[3].append(mkf.reshape(bp, N_MEM, X_HEADS, X_HEAD_DIM))
        outs[4].append(mvf.reshape(bp, N_MEM, X_HEADS, X_HEAD_DIM))

        w_r = jnp.zeros((D_MODEL, LANES), F32).at[:, :N_GROUPS].set(w_group[l])
        w_r = w_r.at[:, N_GROUPS:N_GROUPS + N_EXPERTS].set(w_expert_router[l])
        b_r = jnp.zeros((1, LANES), F32).at[0, :N_GROUPS].set(b_group[l])
        b_r = b_r.at[0, N_GROUPS:N_GROUPS + N_EXPERTS].set(b_expert_router[l])
        hf, eid, wts = _router(xp, xs, norm_ffn_g[l], w_r, b_r, 512)
        dest, tile_e, n_active, zflag = _route_plan(eid[:, :2], moe_tm, moe_tiles)
        rows = _dispatch_rows(hf, dest, zflag, moe_tiles, moe_tm, 256)
        ys = _expert_ffn(rows, tile_e, n_active, w_gate_up[l], w_down[l], moe_tm)
        xp = _combine(xp, wts, ys, dest, 0, moe_tm)
        xs = _combine(xs, wts, ys, dest, tp, moe_tm)
    stack = lambda i: jnp.stack(outs[i])
    return (xp.reshape(bp, sp, D_MODEL), xs.reshape(bs, ts, D_MODEL), stack(0), stack(1), stack(2), stack(3),
            stack(4), stack(5), stack(6), stack(7))
```

```python
import functools
import math

import jax
import jax.numpy as jnp
from jax import lax
from jax.experimental import pallas as pl
from jax.experimental.pallas import tpu as pltpu

F32 = jnp.float32
BF16 = jnp.bfloat16

D_MODEL = 2048
N_HEADS = 8
HEAD_DIM = 128
V_DIM = 256
HEAD_W = 2 * HEAD_DIM
PAGE = 128
X_HEADS = 4
X_HEAD_DIM = 128
X_WIDTH = X_HEADS * X_HEAD_DIM
N_MEM = 256
N_GROUPS = 4
EXPERTS_PER_GROUP = 8
N_EXPERTS = N_GROUPS * EXPERTS_PER_GROUP
D_FF = 512
EPS = 1e-6
NEG = -1e30
LANES = 128
MIB = 1024 * 1024
LOG2E = math.log2(math.e)


def _cparams(sem, vmem_mib):
    return pltpu.CompilerParams(dimension_semantics=sem, vmem_limit_bytes=vmem_mib * MIB)


def _rms(x, g):
    ms = jnp.mean(x * x, axis=-1, keepdims=True)
    return x * lax.rsqrt(ms + EPS) * g


def _headnorm(acc, g, scale=None):
    outs = []
    for c in range(acc.shape[1] // HEAD_DIM):
        y = _rms(acc[:, c * HEAD_DIM:(c + 1) * HEAD_DIM], g)
        outs.append(y if scale is None else y * scale)
    return jnp.concatenate(outs, axis=1)


def _norm_kernel(x_ref, g_ref, o_ref):
    o_ref[...] = _rms(x_ref[...], g_ref[...]).astype(o_ref.dtype)


def _rmsnorm_cast(x2d, g, tm):
    t, d = x2d.shape
    return pl.pallas_call(
        _norm_kernel,
        out_shape=jax.ShapeDtypeStruct((t, d), BF16),
        grid=(t // tm,),
        in_specs=[pl.BlockSpec((tm, d), lambda i: (i, 0)),
                  pl.BlockSpec((1, d), lambda i: (0, 0))],
        out_specs=pl.BlockSpec((tm, d), lambda i: (i, 0)),
        compiler_params=_cparams(("parallel",), 32),
        name="rmsnorm_cast",
    )(x2d, g.reshape(1, d))


def _inproj_kernel(h_ref, wq, wk, wv, wc, wb, wu, wga, wgb, qg_ref, kg_ref, bga_ref, bgb_ref,
                   q_o, kf_o, kb_o, vf_o, vb_o, z_o, b_o, ga_o, gb_o, *, q_scale):
    h = h_ref[...]

    def mm(w):
        return jnp.dot(h, w[...], preferred_element_type=F32)

    q_o[...] = _headnorm(mm(wq), qg_ref[...], q_scale).astype(q_o.dtype)
    k = _headnorm(mm(wk), kg_ref[...])
    kf_o[...] = k
    kb_o[...] = k.astype(BF16)
    v = mm(wv)
    vf_o[...] = v
    vb_o[...] = v.astype(BF16)
    z_o[...] = mm(wc) * mm(wu)
    b_o[...] = mm(wb).astype(b_o.dtype)
    ga_o[...] = jax.nn.sigmoid(mm(wga) + bga_ref[...]).astype(ga_o.dtype)
    gb_o[...] = jax.nn.sigmoid(mm(wgb) + bgb_ref[...]).astype(gb_o.dtype)


def _inproj(hb, w_in_bf, q_g, k_g, b_gate, q_dtype, aux_dtype, tm, tn):
    t = hb.shape[0]
    nb = D_MODEL // tn
    region = {"q": 0, "k": 1, "v": 2, "c": 3, "b": 4, "u": 5, "ga": 6, "gb": 7}

    def wspec(r):
        return pl.BlockSpec((D_MODEL, tn), lambda i, j, r=r: (0, r * nb + j))

    tile = pl.BlockSpec((tm, tn), lambda i, j: (i, j))
    vec = pl.BlockSpec((1, HEAD_DIM), lambda i, j: (0, 0))
    bias = pl.BlockSpec((1, tn), lambda i, j: (0, j))
    f32o = jax.ShapeDtypeStruct((t, D_MODEL), F32)
    bfo = jax.ShapeDtypeStruct((t, D_MODEL), BF16)
    return pl.pallas_call(
        functools.partial(_inproj_kernel, q_scale=HEAD_DIM ** -0.5 * LOG2E),
        out_shape=(jax.ShapeDtypeStruct((t, D_MODEL), q_dtype), f32o, bfo, f32o, bfo, f32o)
        + (jax.ShapeDtypeStruct((t, D_MODEL), aux_dtype),) * 3,
        grid=(t // tm, nb),
        in_specs=[pl.BlockSpec((tm, D_MODEL), lambda i, j: (i, 0))]
        + [wspec(region[n]) for n in ("q", "k", "v", "c", "b", "u", "ga", "gb")]
        + [vec, vec, bias, bias],
        out_specs=(tile,) * 9,
        compiler_params=_cparams(("parallel", "arbitrary"), 48),
        name="inproj",
    )(hb, *([w_in_bf] * 8), q_g.reshape(1, HEAD_DIM), k_g.reshape(1, HEAD_DIM),
      b_gate[0:1], b_gate[1:2])


def _lane_tiles(x):
    return [x[:, c * LANES:(c + 1) * LANES] for c in range(x.shape[1] // LANES)]


def _lane_repeat(x, n):
    return x if n == 1 else jnp.concatenate([x] * n, axis=1)


def _online_update(s_list, v_list, m_prev, l_prev, acc_prev, c_list=None):
    if c_list is None:
        c_list = [None] * len(s_list)
    tops = []
    for s, c in zip(s_list, c_list):
        top = functools.reduce(jnp.maximum, _lane_tiles(s))
        tops.append(top if c is None else top + c)
    m_cur = jnp.max(functools.reduce(jnp.maximum, tops), axis=1, keepdims=True)
    m_next = jnp.maximum(m_prev, m_cur)
    alpha = jnp.exp2(m_prev - m_next)
    l_next = alpha * l_prev
    pv = None
    for s, v, c in zip(s_list, v_list, c_list):
        shift = m_next if c is None else m_next - c
        p = jnp.exp2(s - _lane_repeat(shift, s.shape[1] // LANES))
        l_next = l_next + functools.reduce(jnp.add, _lane_tiles(p))
        d = jnp.dot(p.astype(BF16), v, preferred_element_type=F32)
        pv = d if pv is None else pv + d
    acc_next = acc_prev * _lane_repeat(alpha, V_DIM // LANES) + pv
    return m_next, l_next, acc_next


def _diff_out(a1, l1, a2, l2, lam, g, out_scale):
    r1 = 1.0 / jnp.sum(l1, axis=1, keepdims=True)
    r2 = 1.0 / jnp.sum(l2, axis=1, keepdims=True)
    o = a1 * r1 - lam * (a2 * r2)
    return _rms(o, g) * out_scale


def _init_softmax_state(m_scr, l_scr, acc_scr):
    m_scr[...] = jnp.full(m_scr.shape, NEG, F32)
    l_scr[...] = jnp.zeros(l_scr.shape, F32)
    acc_scr[...] = jnp.zeros(acc_scr.shape, F32)


def _pattn_kernel(qi_ref, ki_ref, lam_ref, slopes_ref, q_ref, k_ref, v_ref, g_ref, o_ref, m_scr, l_scr, acc_scr,
                  *, t, hp, out_scale):
    h = pl.program_id(1)
    step = pl.program_id(2)
    qi = qi_ref[step]
    ki = ki_ref[step]

    @pl.when(ki == 0)
    def _():
        _init_softmax_state(m_scr, l_scr, acc_scr)

    def body(masked):
        kpos = ((ki - qi) * t + lax.broadcasted_iota(jnp.int32, (1, t), 1)).astype(F32)
        if masked:
            keep = (lax.broadcasted_iota(jnp.int32, (t, t), 1)
                    <= lax.broadcasted_iota(jnp.int32, (t, t), 0))
        state = []
        for i in range(hp):
            bias = slopes_ref[h * hp + i] * kpos
            v = v_ref[0, :, i * V_DIM:(i + 1) * V_DIM]
            for m in range(2):
                cols = slice(i * HEAD_W + m * HEAD_DIM, i * HEAD_W + (m + 1) * HEAD_DIM)
                s = lax.dot_general(q_ref[0, :, cols], k_ref[0, :, cols], (((1,), (1,)), ((), ())),
                                    preferred_element_type=F32)
                s = s + bias
                if masked:
                    s = jnp.where(keep, s, NEG)
                idx = 2 * i + m
                state.append(_online_update([s], [v], m_scr[idx], l_scr[idx], acc_scr[idx]))
        return state

    @pl.when(ki < qi)
    def _():
        for idx, (m_next, l_next, acc_next) in enumerate(body(False)):
            m_scr[idx] = m_next
            l_scr[idx] = l_next
            acc_scr[idx] = acc_next

    @pl.when(ki == qi)
    def _():
        state = body(True)
        lam = lam_ref[0]
        g = g_ref[...]
        o_ref[0] = jnp.concatenate(
            [_diff_out(state[2 * i][2], state[2 * i][1], state[2 * i + 1][2], state[2 * i + 1][1], lam, g, out_scale)
             for i in range(hp)], axis=1).astype(o_ref.dtype)


def _prompt_attn(qb, kb, vb, lam, slopes2, subln_g, out_scale, t, hp):
    b, s, _ = qb.shape
    n = s // t
    pairs = [(qi, ki) for qi in range(n) for ki in range(qi + 1)]
    qi_tab = jnp.asarray([p[0] for p in pairs], jnp.int32)
    ki_tab = jnp.asarray([p[1] for p in pairs], jnp.int32)
    smem = pl.BlockSpec(memory_space=pltpu.SMEM)
    grid_spec = pltpu.PrefetchScalarGridSpec(
        num_scalar_prefetch=2,
        grid=(b, N_HEADS // hp, len(pairs)),
        in_specs=[smem, smem,
                  pl.BlockSpec((1, t, hp * HEAD_W), lambda bi, h, st, qt, kt: (bi, qt[st], h)),
                  pl.BlockSpec((1, t, hp * HEAD_W), lambda bi, h, st, qt, kt: (bi, kt[st], h)),
                  pl.BlockSpec((1, t, hp * V_DIM), lambda bi, h, st, qt, kt: (bi, kt[st], h)),
                  pl.BlockSpec((1, V_DIM), lambda bi, h, st, qt, kt: (0, 0))],
        out_specs=pl.BlockSpec((1, t, hp * V_DIM), lambda bi, h, st, qt, kt: (bi, qt[st], h)),
        scratch_shapes=[pltpu.VMEM((2 * hp, t, LANES), F32), pltpu.VMEM((2 * hp, t, LANES), F32),
                        pltpu.VMEM((2 * hp, t, V_DIM), F32)],
    )
    return pl.pallas_call(
        functools.partial(_pattn_kernel, t=t, hp=hp, out_scale=out_scale),
        out_shape=jax.ShapeDtypeStruct((b, s, D_MODEL), BF16),
        grid_spec=grid_spec,
        compiler_params=_cparams(("parallel", "parallel", "arbitrary"), 40),
        name="prompt_attn",
    )(qi_tab, ki_tab, lam, slopes2, qb, kb, vb, subln_g.reshape(1, V_DIM))


def _sattn_bias(slopes2, tq, n_tok, n_lanes, causal):
    rows = jnp.arange(2 * N_HEADS * tq)
    row_h = (rows // tq) % N_HEADS
    row_q = rows % tq
    lanes = jnp.arange(n_lanes)
    tok = lanes // N_HEADS
    ok = (row_h[:, None] == (lanes % N_HEADS)[None, :]) & (tok < n_tok)[None, :]
    if causal:
        ok = ok & (tok[None, :] <= row_q[:, None])
    return jnp.where(ok, slopes2[row_h][:, None] * tok[None, :].astype(F32), NEG).astype(F32)


def _sattn_kernel(pt_ref, lam_ref, q_ref, kn_ref, vn_ref, *rest, npg, n_steps, past, out_scale):
    del pt_ref
    k_refs = rest[:npg]
    v_refs = rest[npg:2 * npg]
    bias_ref, bias_new_ref, slope_ref, g_ref, o_ref, qbd_scr, m_scr, l_scr, acc_scr = rest[2 * npg:]
    p = pl.program_id(1)
    tq = q_ref.shape[1]
    half = N_HEADS * tq

    @pl.when(p == 0)
    def _():
        _init_softmax_state(m_scr, l_scr, acc_scr)
        q = q_ref[0]
        zero = jnp.zeros((half, HEAD_DIM), F32)
        q1, q2 = [jnp.concatenate([q[:, h * HEAD_W + m * HEAD_DIM:h * HEAD_W + (m + 1) * HEAD_DIM]
                                   for h in range(N_HEADS)], axis=0) for m in range(2)]
        qbd_scr[...] = jnp.concatenate([jnp.concatenate([q1, zero], axis=1),
                                        jnp.concatenate([zero, q2], axis=1)], axis=0).astype(BF16)

    nt = (((1,), (1,)), ((), ()))

    def body(last):
        qbd = qbd_scr[...]
        bias = bias_ref[...]
        slope = slope_ref[...]
        s_list, v_list, c_list = [], [], []
        for j in range(npg):
            s = lax.dot_general(qbd, k_refs[j][0].astype(BF16), nt, preferred_element_type=F32)
            s_list.append(s + bias)
            v_list.append(v_refs[j][0].astype(BF16))
            c_list.append(slope * ((p * npg + j) * PAGE - past).astype(F32))
        if last:
            pad = jnp.zeros((LANES - kn_ref.shape[1], HEAD_W), F32)
            kn = jnp.concatenate([kn_ref[0], pad], axis=0).astype(BF16)
            s = lax.dot_general(qbd, kn, nt, preferred_element_type=F32)
            s_list.append(s + bias_new_ref[...])
            v_list.append(jnp.concatenate([vn_ref[0], pad], axis=0).astype(BF16))
            c_list.append(None)
        return _online_update(s_list, v_list, m_scr[...], l_scr[...], acc_scr[...], c_list)

    @pl.when(p < n_steps - 1)
    def _():
        m_next, l_next, acc_next = body(False)
        m_scr[...] = m_next
        l_scr[...] = l_next
        acc_scr[...] = acc_next

    @pl.when(p == n_steps - 1)
    def _():
        _, l, acc = body(True)
        y = _diff_out(acc[:half], l[:half], acc[half:], l[half:], lam_ref[0], g_ref[...], out_scale)
        o_ref[0] = jnp.concatenate([y[h * tq:(h + 1) * tq] for h in range(N_HEADS)], axis=1)


def _sample_attn(qs, kn4, vn4, pool_k, pool_v, page_table, lam, slopes2, subln_g, out_scale, npg):
    db, tq, _ = qs.shape
    n_pages = page_table.shape[1]
    n_steps = n_pages // npg
    n_phys = pool_k.shape[0]
    rows = 2 * N_HEADS * tq
    page_rows = PAGE * N_HEADS
    pk = pool_k.reshape(n_phys, page_rows, HEAD_W)
    pv = pool_v.reshape(n_phys, page_rows, V_DIM)
    kn = kn4.reshape(db, tq * N_HEADS, HEAD_W)
    vn = vn4.reshape(db, tq * N_HEADS, V_DIM)
    bias = _sattn_bias(slopes2, tq, PAGE, page_rows, False)
    bias_new = _sattn_bias(slopes2, tq, tq, LANES, True)
    slope_rep = jnp.broadcast_to(slopes2[(jnp.arange(rows) // tq) % N_HEADS][:, None], (rows, LANES))
    smem = pl.BlockSpec(memory_space=pltpu.SMEM)
    tok = pl.BlockSpec((1, tq, D_MODEL), lambda b, p, pt: (b, 0, 0))
    new = pl.BlockSpec((1, tq * N_HEADS, HEAD_W), lambda b, p, pt: (b, 0, 0))
    const = lambda shp: pl.BlockSpec(shp, lambda b, p, pt: (0, 0))

    def page(j):
        return pl.BlockSpec((1, page_rows, HEAD_W), lambda b, p, pt, j=j: (pt[b, p * npg + j], 0, 0))

    grid_spec = pltpu.PrefetchScalarGridSpec(
        num_scalar_prefetch=1,
        grid=(db, n_steps),
        in_specs=[smem, tok, new, new] + [page(j) for j in range(npg)] * 2
        + [const((rows, page_rows)), const((rows, LANES)), const((rows, LANES)), const((1, V_DIM))],
        out_specs=tok,
        scratch_shapes=[pltpu.VMEM((rows, HEAD_W), BF16), pltpu.VMEM((rows, LANES), F32),
                        pltpu.VMEM((rows, LANES), F32), pltpu.VMEM((rows, V_DIM), F32)],
    )
    return pl.pallas_call(
        functools.partial(_sattn_kernel, npg=npg, n_steps=n_steps, past=n_pages * PAGE, out_scale=out_scale),
        out_shape=jax.ShapeDtypeStruct((db, tq, D_MODEL), F32),
        grid_spec=grid_spec,
        compiler_params=_cparams(("parallel", "arbitrary"), 48),
        name="sample_attn",
    )(page_table, lam, qs, kn, vn, *([pk] * npg), *([pv] * npg), bias, bias_new, slope_rep,
      subln_g.reshape(1, V_DIM))


def _conv_merge(o, ga, gb, b, z, z1, z2, cw):
    y = z2 * cw[0] + z1 * cw[1] + z * cw[2]
    return (ga.astype(F32) * o.astype(F32) + gb.astype(F32) * (b.astype(F32) * y)).astype(BF16)


def _mix_outproj_kernel(o_ref, ga_ref, gb_ref, b_ref, z_ref, zp_ref, cw_ref, w_ref, r_ref, out_ref, merged_scr,
                        *, tiles_per_seq):
    i = pl.program_id(0)

    @pl.when(pl.program_id(1) == 0)
    def _():
        z = z_ref[...]
        tm = z.shape[0]
        zp = jnp.where(i % tiles_per_seq == 0, 0.0, zp_ref[...])
        pm1 = zp[7:8, :]
        pm2 = zp[6:7, :]
        row = lax.broadcasted_iota(jnp.int32, (tm, 1), 0)
        z1 = jnp.where(row == 0, pm1, pltpu.roll(z, 1, 0))
        z2 = jnp.where(row == 0, pm2, jnp.where(row == 1, pm1, pltpu.roll(z, 2, 0)))
        cw = cw_ref[...]
        merged_scr[...] = _conv_merge(o_ref[...], ga_ref[...], gb_ref[...], b_ref[...], z, z1, z2,
                                      (cw[0:1], cw[1:2], cw[2:3]))

    out_ref[...] = r_ref[...] + jnp.dot(merged_scr[...], w_ref[...], preferred_element_type=F32)


def _mix_outproj_prompt(o, ga, gb, b, z, conv_w, w, r, seq, tm, tn):
    t = z.shape[0]
    n = w.shape[1]
    row = pl.BlockSpec((tm, D_MODEL), lambda i, j: (i, 0))
    prev = pl.BlockSpec((8, D_MODEL), lambda i, j: (jnp.maximum(i * (tm // 8) - 1, 0), 0))
    tile = pl.BlockSpec((tm, tn), lambda i, j: (i, j))
    return pl.pallas_call(
        functools.partial(_mix_outproj_kernel, tiles_per_seq=seq // tm),
        out_shape=jax.ShapeDtypeStruct((t, n), F32),
        grid=(t // tm, n // tn),
        in_specs=[row, row, row, row, row, prev, pl.BlockSpec((3, D_MODEL), lambda i, j: (0, 0)),
                  pl.BlockSpec((D_MODEL, tn), lambda i, j: (0, j)), tile],
        out_specs=tile,
        scratch_shapes=[pltpu.VMEM((tm, D_MODEL), BF16)],
        compiler_params=_cparams(("parallel", "arbitrary"), 48),
        name="mix_outproj",
    )(o, ga, gb, b, z, z, conv_w, w, r)


def _mix_sample_kernel(o_ref, ga_ref, gb_ref, b_ref, z_ref, cc_ref, cw_ref, out_ref):
    z = z_ref[...]
    cc = cc_ref[...]
    c0 = cc[:, 0:1, :]
    c1 = cc[:, 1:2, :]
    row = lax.broadcasted_iota(jnp.int32, (1, z.shape[1], 1), 1)
    z1 = jnp.where(row == 0, c1, pltpu.roll(z, 1, 1))
    z2 = jnp.where(row == 0, c0, jnp.where(row == 1, c1, pltpu.roll(z, 2, 1)))
    cw = cw_ref[...]
    out_ref[...] = _conv_merge(o_ref[...], ga_ref[...], gb_ref[...], b_ref[...], z, z1, z2,
                               (cw[0:1][None], cw[1:2][None], cw[2:3][None]))


def _mix_sample(o, ga, gb, b, z, cache_conv, conv_w, g):
    db, ts, _ = z.shape
    tile = pl.BlockSpec((g, ts, D_MODEL), lambda i: (i, 0, 0))
    return pl.pallas_call(
        _mix_sample_kernel,
        out_shape=jax.ShapeDtypeStruct((db, ts, D_MODEL), BF16),
        grid=(db // g,),
        in_specs=[tile, tile, tile, tile, tile,
                  pl.BlockSpec((g, cache_conv.shape[1], D_MODEL), lambda i: (i, 0, 0)),
                  pl.BlockSpec((3, D_MODEL), lambda i: (0, 0))],
        out_specs=tile,
        compiler_params=_cparams(("parallel",), 40),
        name="mix_sample",
    )(o, ga, gb, b, z, cache_conv, conv_w)


def _mm_res_kernel(a_ref, w_ref, r_ref, o_ref):
    o_ref[...] = r_ref[...] + jnp.dot(a_ref[...], w_ref[...], preferred_element_type=F32)


def _mm_res(a, w, r, tm, tn):
    t, k = a.shape
    n = w.shape[1]
    return pl.pallas_call(
        _mm_res_kernel,
        out_shape=jax.ShapeDtypeStruct((t, n), F32),
        grid=(t // tm, n // tn),
        in_specs=[pl.BlockSpec((tm, k), lambda i, j: (i, 0)),
                  pl.BlockSpec((k, tn), lambda i, j: (0, j)),
                  pl.BlockSpec((tm, tn), lambda i, j: (i, j))],
        out_specs=pl.BlockSpec((tm, tn), lambda i, j: (i, j)),
        compiler_params=_cparams(("parallel", "parallel"), 48),
        name="outproj",
    )(a, w, r)


def _memkv_kernel(h_ref, wk_ref, wv_ref, g_ref, kf_o, kb_o, vf_o, vb_o):
    h = h_ref[...]
    k = _headnorm(jnp.dot(h, wk_ref[...], preferred_element_type=F32), g_ref[...])
    kf_o[...] = k
    kb_o[...] = k.astype(BF16)
    v = jnp.dot(h, wv_ref[...], preferred_element_type=F32)
    vf_o[...] = v
    vb_o[...] = v.astype(BF16)


def _mem_kv(mem_nb, w_xkv_bf, xk_g):
    t = mem_nb.shape[0]
    f32o = jax.ShapeDtypeStruct((t, X_WIDTH), F32)
    bfo = jax.ShapeDtypeStruct((t, X_WIDTH), BF16)
    full = lambda s: pl.BlockSpec(s, lambda i: (0,) * len(s))
    return pl.pallas_call(
        _memkv_kernel,
        out_shape=(f32o, bfo, f32o, bfo),
        grid=(1,),
        in_specs=[full((t, D_MODEL)),
                  pl.BlockSpec((D_MODEL, X_WIDTH), lambda i: (0, 0)),
                  pl.BlockSpec((D_MODEL, X_WIDTH), lambda i: (0, 1)),
                  full((1, X_HEAD_DIM))],
        out_specs=(full((t, X_WIDTH)),) * 4,
        compiler_params=_cparams(("arbitrary",), 32),
        name="mem_kv",
    )(mem_nb, w_xkv_bf, w_xkv_bf, xk_g.reshape(1, X_HEAD_DIM))


def _xattn_head(qh, kh, vh):
    s = lax.dot_general(qh, kh, (((1,), (1,)), ((), ())), preferred_element_type=F32)
    e = jnp.exp(s - jnp.max(s, axis=1, keepdims=True))
    p = e * (1.0 / jnp.sum(e, axis=1, keepdims=True))
    return jnp.dot(p.astype(BF16), vh, preferred_element_type=F32)


def _xattn_q(x, g_ref, wq_ref, qg_ref):
    hn = _rms(x, g_ref[...]).astype(BF16)
    q = jnp.dot(hn, wq_ref[...], preferred_element_type=F32)
    return _headnorm(q, qg_ref[...], X_HEAD_DIM ** -0.5)


def _xattn_prompt_kernel(x_ref, g_ref, wq_ref, qg_ref, mk_ref, mv_ref, wo_ref, o_ref):
    x = x_ref[0]
    q = _xattn_q(x, g_ref, wq_ref, qg_ref).astype(BF16)
    mk = mk_ref[0]
    mv = mv_ref[0]
    heads = []
    for g in range(X_HEADS):
        sl = slice(g * X_HEAD_DIM, (g + 1) * X_HEAD_DIM)
        heads.append(_xattn_head(q[:, sl], mk[:, sl], mv[:, sl]))
    o = jnp.concatenate(heads, axis=1).astype(BF16)
    o_ref[0] = x + jnp.dot(o, wo_ref[...], preferred_element_type=F32)


def _xattn_prompt(x, g, w_xq_bf, xq_g, mk_b, mv_b, w_xo_bf, tm):
    b, s, _ = x.shape
    tile = pl.BlockSpec((1, tm, D_MODEL), lambda bi, i: (bi, i, 0))
    mem = pl.BlockSpec((1, N_MEM, X_WIDTH), lambda bi, i: (bi, 0, 0))
    const = lambda shp: pl.BlockSpec(shp, lambda bi, i: (0, 0))
    return pl.pallas_call(
        _xattn_prompt_kernel,
        out_shape=jax.ShapeDtypeStruct((b, s, D_MODEL), F32),
        grid=(b, s // tm),
        in_specs=[tile, const((1, D_MODEL)), const((D_MODEL, X_WIDTH)), const((1, X_HEAD_DIM)),
                  mem, mem, const((X_WIDTH, D_MODEL))],
        out_specs=tile,
        compiler_params=_cparams(("parallel", "parallel"), 48),
        name="xattn_prompt",
    )(x, g.reshape(1, D_MODEL), w_xq_bf, xq_g.reshape(1, X_HEAD_DIM), mk_b, mv_b, w_xo_bf)


def _xattn_sample_kernel(x_ref, g_ref, wq_ref, qg_ref, mk_ref, mv_ref, mask_ref, wo_ref, o_ref, *, ts):
    x = x_ref[...]
    q = _xattn_q(x, g_ref, wq_ref, qg_ref)
    mask = mask_ref[...]
    nt = (((1,), (1,)), ((), ()))
    per_batch = []
    for bi in range(mk_ref.shape[0]):
        rows = slice(bi * ts, (bi + 1) * ts)
        qs = jnp.concatenate([q[rows, g * X_HEAD_DIM:(g + 1) * X_HEAD_DIM] for g in range(X_HEADS)], axis=0)
        s = lax.dot_general(qs.astype(BF16), mk_ref[bi].astype(BF16), nt, preferred_element_type=F32) + mask
        e = jnp.exp(s - jnp.max(s, axis=1, keepdims=True))
        o = jnp.dot(e.astype(BF16), mv_ref[bi].astype(BF16), preferred_element_type=F32)
        o = o * (1.0 / jnp.sum(e, axis=1, keepdims=True))
        per_batch.append(jnp.concatenate([o[g * ts:(g + 1) * ts] for g in range(X_HEADS)], axis=1))
    a = jnp.concatenate(per_batch, axis=0).astype(BF16)
    o_ref[...] = x + jnp.dot(a, wo_ref[...], preferred_element_type=F32)


def _xattn_sample(x2d, g, w_xq_bf, xq_g, cache_mk, cache_mv, w_xo_bf, ts, gb):
    t = x2d.shape[0]
    db = cache_mk.shape[0]
    mk = cache_mk.reshape(db, N_MEM * X_HEADS, X_HEAD_DIM)
    mv = cache_mv.reshape(db, N_MEM * X_HEADS, X_HEAD_DIM)
    tile = pl.BlockSpec((gb * ts, D_MODEL), lambda i: (i, 0))
    mem = pl.BlockSpec((gb, N_MEM * X_HEADS, X_HEAD_DIM), lambda i: (i, 0, 0))
    const = lambda shp: pl.BlockSpec(shp, lambda i: (0, 0))
    row_head = jnp.arange(X_HEADS * ts) // ts
    lane_head = jnp.arange(N_MEM * X_HEADS) % X_HEADS
    mask = jnp.where(row_head[:, None] == lane_head[None, :], 0.0, NEG).astype(F32)
    return pl.pallas_call(
        functools.partial(_xattn_sample_kernel, ts=ts),
        out_shape=jax.ShapeDtypeStruct((t, D_MODEL), F32),
        grid=(db // gb,),
        in_specs=[tile, const((1, D_MODEL)), const((D_MODEL, X_WIDTH)), const((1, X_HEAD_DIM)),
                  mem, mem, const((X_HEADS * ts, N_MEM * X_HEADS)), const((X_WIDTH, D_MODEL))],
        out_specs=tile,
        compiler_params=_cparams(("parallel",), 48),
        name="xattn_sample",
    )(x2d, g.reshape(1, D_MODEL), w_xq_bf, xq_g.reshape(1, X_HEAD_DIM), mk, mv, mask, w_xo_bf)


def _first_argmax(v, lane):
    vmax = jnp.max(v, axis=1, keepdims=True)
    idx = jnp.min(jnp.where(v == vmax, lane, LANES), axis=1, keepdims=True)
    return vmax, idx


def _router_kernel(xp_ref, xs_ref, g_ref, wr_ref, br_ref, h_o, eid_o, w_o, *, n_prompt_tiles):
    i = pl.program_id(0)

    def body(x_ref):
        h = _rms(x_ref[...], g_ref[...])
        h_o[...] = h
        logits = jnp.dot(h, wr_ref[...], preferred_element_type=F32,
                         precision=lax.Precision.HIGHEST) + br_ref[...]
        lane = lax.broadcasted_iota(jnp.int32, logits.shape, 1)
        ninf = -jnp.inf
        gl = jnp.where(lane < N_GROUPS, logits, ninf)
        gmax, gidx = _first_argmax(gl, lane)
        g_w = 1.0 / jnp.sum(jnp.exp(gl - gmax), axis=1, keepdims=True)
        lo = N_GROUPS + gidx * EXPERTS_PER_GROUP
        el = jnp.where((lane >= lo) & (lane < lo + EXPERTS_PER_GROUP), logits, ninf)
        v1, i1 = _first_argmax(el, lane)
        v2, i2 = _first_argmax(jnp.where(lane == i1, ninf, el), lane)
        e2 = jnp.exp(v2 - v1)
        w1 = g_w / (1.0 + e2)
        w2 = g_w * e2 / (1.0 + e2)
        eid_o[...] = jnp.where(lane == 0, i1 - N_GROUPS, jnp.where(lane == 1, i2 - N_GROUPS, 0))
        w_o[...] = jnp.where(lane == 0, w1, jnp.where(lane == 1, w2, 0.0))

    @pl.when(i < n_prompt_tiles)
    def _():
        body(xp_ref)

    @pl.when(i >= n_prompt_tiles)
    def _():
        body(xs_ref)


def _router(xp2d, xs2d, g, w_r, b_r, tm):
    tp, ts = xp2d.shape[0], xs2d.shape[0]
    npt, nst = tp // tm, ts // tm
    t = tp + ts
    tile = pl.BlockSpec((tm, D_MODEL), lambda i: (i, 0))
    const = lambda shp: pl.BlockSpec(shp, lambda i: (0, 0))
    return pl.pallas_call(
        functools.partial(_router_kernel, n_prompt_tiles=npt),
        out_shape=(jax.ShapeDtypeStruct((t, D_MODEL), F32),
                   jax.ShapeDtypeStruct((t, LANES), jnp.int32),
                   jax.ShapeDtypeStruct((t, LANES), F32)),
        grid=(npt + nst,),
        in_specs=[pl.BlockSpec((tm, D_MODEL), lambda i: (jnp.minimum(i, npt - 1), 0)),
                  pl.BlockSpec((tm, D_MODEL), lambda i: (jnp.maximum(i - npt, 0), 0)),
                  const((1, D_MODEL)), const((D_MODEL, LANES)), const((1, LANES))],
        out_specs=(tile, pl.BlockSpec((tm, LANES), lambda i: (i, 0)), pl.BlockSpec((tm, LANES), lambda i: (i, 0))),
        compiler_params=_cparams(("parallel",), 40),
        name="moe_router",
    )(xp2d, xs2d, g.reshape(1, D_MODEL), w_r, b_r)


def _dispatch_kernel(dest_ref, zflag_ref, h_ref, rows_hbm, zero_scr, sem, *, tmd, tm, n_tiles):
    i = pl.program_id(0)

    def zero_tile(t, start):
        @pl.when(zflag_ref[t] != 0)
        def _():
            cp = pltpu.make_async_copy(zero_scr, rows_hbm.at[pl.ds(pl.multiple_of(t * tm, tm), tm)], sem.at[2])
            if start:
                cp.start()
            else:
                cp.wait()

    @pl.when(i == 0)
    def _():
        zero_scr[...] = jnp.zeros(zero_scr.shape, F32)
        lax.fori_loop(0, n_tiles, lambda t, c: (zero_tile(t, True), c)[1], 0)
        lax.fori_loop(0, n_tiles, lambda t, c: (zero_tile(t, False), c)[1], 0)

    base = 2 * i * tmd

    def row_copy(r, slot):
        return pltpu.make_async_copy(h_ref.at[pl.ds(r, 1)], rows_hbm.at[pl.ds(dest_ref[base + 2 * r + slot], 1)],
                                     sem.at[slot])

    def start(r, carry):
        row_copy(r, 0).start()
        row_copy(r, 1).start()
        return carry

    lax.fori_loop(0, tmd, start, 0)
    for slot in range(2):
        pltpu.make_async_copy(h_ref, rows_hbm.at[pl.ds(0, tmd)], sem.at[slot]).wait()


def _dispatch_rows(hf, dest_flat, zflag, n_tiles, tm, tmd):
    t, d = hf.shape
    grid_spec = pltpu.PrefetchScalarGridSpec(
        num_scalar_prefetch=2,
        grid=(t // tmd,),
        in_specs=[pl.BlockSpec((tmd, d), lambda i, dest, zf: (i, 0))],
        out_specs=pl.BlockSpec(memory_space=pl.ANY),
        scratch_shapes=[pltpu.VMEM((tm, d), F32), pltpu.SemaphoreType.DMA((3,))],
    )
    return pl.pallas_call(
        functools.partial(_dispatch_kernel, tmd=tmd, tm=tm, n_tiles=n_tiles),
        out_shape=jax.ShapeDtypeStruct((n_tiles * tm, d), F32),
        grid_spec=grid_spec,
        compiler_params=_cparams(("arbitrary",), 32),
        name="moe_dispatch",
    )(dest_flat, zflag, hf)


def _ffn_kernel(te_ref, na_ref, x_ref, wgu_ref, wd_ref, o_ref, wgu_scr, wd_scr):
    t = pl.program_id(0)

    @pl.when(t < na_ref[0])
    def _():
        e = te_ref[t]
        e_prev = te_ref[jnp.maximum(t - 1, 0)]

        @pl.when((t == 0) | (e != e_prev))
        def _():
            wgu_scr[...] = wgu_ref[0].astype(BF16)
            wd_scr[...] = wd_ref[0].astype(BF16)

        gu = jnp.dot(x_ref[...].astype(BF16), wgu_scr[...], preferred_element_type=F32)
        gate = gu[:, :D_FF]
        act = (gate * jax.nn.sigmoid(gate)) * gu[:, D_FF:]
        o_ref[...] = jnp.dot(act.astype(BF16), wd_scr[...], preferred_element_type=F32)

    @pl.when(t >= na_ref[0])
    def _():
        o_ref[...] = jnp.zeros(o_ref.shape, F32)


def _expert_ffn(xs, tile_e, n_active, w_gu, w_down, tm):
    n_tiles = xs.shape[0] // tm
    grid_spec = pltpu.PrefetchScalarGridSpec(
        num_scalar_prefetch=2,
        grid=(n_tiles,),
        in_specs=[pl.BlockSpec((tm, D_MODEL), lambda t, te, na: (jnp.minimum(t, na[0] - 1), 0)),
                  pl.BlockSpec((1, D_MODEL, 2 * D_FF), lambda t, te, na: (te[t], 0, 0)),
                  pl.BlockSpec((1, D_FF, D_MODEL), lambda t, te, na: (te[t], 0, 0))],
        out_specs=pl.BlockSpec((tm, D_MODEL), lambda t, te, na: (t, 0)),
        scratch_shapes=[pltpu.VMEM((D_MODEL, 2 * D_FF), BF16), pltpu.VMEM((D_FF, D_MODEL), BF16)],
    )
    return pl.pallas_call(
        _ffn_kernel,
        out_shape=jax.ShapeDtypeStruct((n_tiles * tm, D_MODEL), F32),
        grid_spec=grid_spec,
        compiler_params=_cparams(("arbitrary",), 48),
        name="moe_ffn",
    )(tile_e, n_active, xs, w_gu, w_down)


def _combine_kernel(dest_ref, x_ref, w_ref, ys_hbm, o_ref, buf_a, buf_b, sem, *, tm, base_tok):
    base = 2 * (base_tok + pl.program_id(0) * tm)
    bufs = (buf_a, buf_b)

    def start(r, carry):
        for slot in range(2):
            pltpu.make_async_copy(ys_hbm.at[pl.ds(dest_ref[base + 2 * r + slot], 1)], bufs[slot].at[pl.ds(r, 1)],
                                  sem.at[slot]).start()
        return carry

    lax.fori_loop(0, tm, start, 0)
    for slot in range(2):
        pltpu.make_async_copy(ys_hbm.at[pl.ds(0, tm)], bufs[slot], sem.at[slot]).wait()
    w = w_ref[...]
    o_ref[...] = x_ref[...] + (w[:, 0:1] * buf_a[...] + w[:, 1:2] * buf_b[...])


def _combine(x2d, wts, ys, dest_flat, base_tok, tm):
    t, d = x2d.shape
    base_blk = base_tok // tm
    grid_spec = pltpu.PrefetchScalarGridSpec(
        num_scalar_prefetch=1,
        grid=(t // tm,),
        in_specs=[pl.BlockSpec((tm, d), lambda i, dest: (i, 0)),
                  pl.BlockSpec((tm, LANES), lambda i, dest: (i + base_blk, 0)),
                  pl.BlockSpec(memory_space=pl.ANY)],
        out_specs=pl.BlockSpec((tm, d), lambda i, dest: (i, 0)),
        scratch_shapes=[pltpu.VMEM((tm, d), F32), pltpu.VMEM((tm, d), F32), pltpu.SemaphoreType.DMA((2,))],
    )
    return pl.pallas_call(
        functools.partial(_combine_kernel, tm=tm, base_tok=base_tok),
        out_shape=jax.ShapeDtypeStruct((t, d), F32),
        grid_spec=grid_spec,
        compiler_params=_cparams(("arbitrary",), 32),
        name="moe_combine",
    )(dest_flat, x2d, wts, ys)


def _route_plan(eid, tm, n_tiles):
    flat_e = eid.reshape(-1)
    onehot = (flat_e[:, None] == jnp.arange(N_EXPERTS, dtype=jnp.int32)[None, :]).astype(jnp.int32)
    csum = jnp.cumsum(onehot, axis=0)
    rank = jnp.sum(onehot * csum, axis=1) - 1
    counts = csum[-1]
    padded = ((counts + tm - 1) // tm) * tm
    ends = jnp.cumsum(padded)
    dest = (jnp.sum(onehot * (ends - padded)[None, :], axis=1) + rank).astype(jnp.int32)
    n_active = (ends[-1] // tm).astype(jnp.int32)
    tile = jnp.arange(n_tiles, dtype=jnp.int32)
    active = tile < n_active
    tile_e = jnp.sum((ends[None, :] <= (tile * tm)[:, None]).astype(jnp.int32), axis=1)
    last_e = jnp.sum(jnp.where(tile == n_active - 1, tile_e, 0))
    tile_e = jnp.where(active, tile_e, last_e).astype(jnp.int32)
    has_pad = jnp.any((ends[None, :] == ((tile + 1) * tm)[:, None]) & (padded != counts)[None, :], axis=1)
    zflag = (has_pad | ~active).astype(jnp.int32)
    return dest, tile_e, n_active.reshape(1), zflag


def kernel(x_prompt, x_sample, cache_attn_k, cache_attn_v, cache_conv, cache_mem_k, cache_mem_v, page_table, mem_prompt, norm_mix_g, w_in, b_gate, q_norm_g, k_norm_g, lambda_q1, lambda_k1, lambda_q2, lambda_k2, subln_g, conv_w, w_out, norm_mem_g, norm_x_g, w_xq, w_xkv, xq_norm_g, xk_norm_g, w_xo, norm_ffn_g, w_group, b_group, w_expert_router, b_expert_router, w_gate_up, w_down):
    depth = w_in.shape[0]
    bp, sp, _ = x_prompt.shape
    bs, ts, _ = x_sample.shape
    tp, tsn = bp * sp, bs * ts
    slopes = jnp.exp2(-8.0 * jnp.arange(1, N_HEADS + 1, dtype=F32) / N_HEADS) * LOG2E
    xp = x_prompt.reshape(tp, D_MODEL)
    xs = x_sample.reshape(tsn, D_MODEL)
    moe_tm = 256
    moe_tiles = (tp + tsn) * 2 // moe_tm + N_EXPERTS
    outs = [[] for _ in range(8)]
    for l in range(depth):
        lam_init = 0.8 - 0.6 * math.exp(-0.3 * l)
        lam = (jnp.exp(jnp.sum(lambda_q1[l] * lambda_k1[l])) - jnp.exp(jnp.sum(lambda_q2[l] * lambda_k2[l]))
               + lam_init).reshape(1).astype(F32)
        out_scale = 1.0 - lam_init
        w_in_bf = w_in[l].astype(BF16)
        w_out_bf = w_out[l].astype(BF16)
        w_xq_bf = w_xq[l].astype(BF16)
        w_xkv_bf = w_xkv[l].astype(BF16)
        w_xo_bf = w_xo[l].astype(BF16)

        hb = _rmsnorm_cast(xp, norm_mix_g[l], 512)
        q, kf, kb, vf, vb, z, b, ga, gb = _inproj(hb, w_in_bf, q_norm_g[l], k_norm_g[l], b_gate[l], BF16, BF16,
                                                  1024, 256)
        o = _prompt_attn(q.reshape(bp, sp, D_MODEL), kb.reshape(bp, sp, D_MODEL), vb.reshape(bp, sp, D_MODEL),
                         lam, slopes, subln_g[l], out_scale, 512, 4)
        xp = _mix_outproj_prompt(o.reshape(tp, D_MODEL), ga, gb, b, z, conv_w[l], w_out_bf, xp, sp, 512, 1024)
        outs[0].append(kf.reshape(bp, sp, N_HEADS, HEAD_W))
        outs[1].append(vf.reshape(bp, sp, N_HEADS, V_DIM))
        outs[2].append(z.reshape(bp, sp, D_MODEL)[:, sp - 2:, :])

        hb = _rmsnorm_cast(xs, norm_mix_g[l], 512)
        q, kf, kb, vf, vb, z, b, ga, gb = _inproj(hb, w_in_bf, q_norm_g[l], k_norm_g[l], b_gate[l], F32, F32,
                                                  1024, 256)
        shp = (bs, ts, D_MODEL)
        kn4 = kf.reshape(bs, ts, N_HEADS, HEAD_W)
        vn4 = vf.reshape(bs, ts, N_HEADS, V_DIM)
        o = _sample_attn(q.reshape(shp), kn4, vn4, cache_attn_k[l], cache_attn_v[l],
                         page_table, lam, slopes, subln_g[l], out_scale, 8)
        z3 = z.reshape(shp)
        merged = _mix_sample(o, ga.reshape(shp), gb.reshape(shp), b.reshape(shp), z3, cache_conv[l], conv_w[l], 32)
        xs = _mm_res(merged.reshape(tsn, D_MODEL), w_out_bf, xs, 512, 1024)
        outs[5].append(kn4)
        outs[6].append(vn4)
        outs[7].append(z3[:, ts - 2:, :])

        mem_nb = _rmsnorm_cast(mem_prompt.reshape(bp * N_MEM, D_MODEL), norm_mem_g[l], 512)
        mkf, mkb, mvf, mvb = _mem_kv(mem_nb, w_xkv_bf, xk_norm_g[l])
        xp = _xattn_prompt(xp.reshape(bp, sp, D_MODEL), norm_x_g[l], w_xq_bf, xq_norm_g[l],
                           mkb.reshape(bp, N_MEM, X_WIDTH), mvb.reshape(bp, N_MEM, X_WIDTH), w_xo_bf,
                           512).reshape(tp, D_MODEL)
        xs = _xattn_sample(xs, norm_x_g[l], w_xq_bf, xq_norm_g[l], cache_mem_k[l], cache_mem_v[l], w_xo_bf, ts, 8)
        outs[3].append(mkf.reshape(bp, N_MEM, X_HEADS, X_HEAD_DIM))
        outs[4].append(mvf.reshape(bp, N_MEM, X_HEADS, X_HEAD_DIM))

        w_r = jnp.zeros((D_MODEL, LANES), F32).at[:, :N_GROUPS].set(w_group[l])
        w_r = w_r.at[:, N_GROUPS:N_GROUPS + N_EXPERTS].set(w_expert_router[l])
        b_r = jnp.zeros((1, LANES), F32).at[0, :N_GROUPS].set(b_group[l])
        b_r = b_r.at[0, N_GROUPS:N_GROUPS + N_EXPERTS].set(b_expert_router[l])
        hf, eid, wts = _router(xp, xs, norm_ffn_g[l], w_r, b_r, 512)
        dest, tile_e, n_active, zflag = _route_plan(eid[:, :2], moe_tm, moe_tiles)
        rows = _dispatch_rows(hf, dest, zflag, moe_tiles, moe_tm, 256)
        ys = _expert_ffn(rows, tile_e, n_active, w_gate_up[l], w_down[l], moe_tm)
        xp = _combine(xp, wts, ys, dest, 0, moe_tm)
        xs = _combine(xs, wts, ys, dest, tp, moe_tm)
    stack = lambda i: jnp.stack(outs[i])
    return (xp.reshape(bp, sp, D_MODEL), xs.reshape(bs, ts, D_MODEL), stack(0), stack(1), stack(2), stack(3),
            stack(4), stack(5), stack(6), stack(7))
```

```python
import functools
import math

import jax
import jax.numpy as jnp
from jax import lax
from jax.experimental import pallas as pl
from jax.experimental.pallas import tpu as pltpu

F32 = jnp.float32
BF16 = jnp.bfloat16

D_MODEL = 2048
N_HEADS = 8
HEAD_DIM = 128
V_DIM = 256
HEAD_W = 2 * HEAD_DIM
PAGE = 128
X_HEADS = 4
X_HEAD_DIM = 128
X_WIDTH = X_HEADS * X_HEAD_DIM
N_MEM = 256
N_GROUPS = 4
EXPERTS_PER_GROUP = 8
N_EXPERTS = N_GROUPS * EXPERTS_PER_GROUP
D_FF = 512
EPS = 1e-6
NEG = -1e30
LANES = 128
MIB = 1024 * 1024
LOG2E = math.log2(math.e)


def _cparams(sem, vmem_mib):
    return pltpu.CompilerParams(dimension_semantics=sem, vmem_limit_bytes=vmem_mib * MIB)


def _rms(x, g):
    ms = jnp.mean(x * x, axis=-1, keepdims=True)
    return x * lax.rsqrt(ms + EPS) * g


def _headnorm(acc, g, scale=None):
    outs = []
    for c in range(acc.shape[1] // HEAD_DIM):
        y = _rms(acc[:, c * HEAD_DIM:(c + 1) * HEAD_DIM], g)
        outs.append(y if scale is None else y * scale)
    return jnp.concatenate(outs, axis=1)


def _norm_kernel(x_ref, g_ref, o_ref):
    o_ref[...] = _rms(x_ref[...], g_ref[...]).astype(o_ref.dtype)


def _rmsnorm_cast(x2d, g, tm):
    t, d = x2d.shape
    return pl.pallas_call(
        _norm_kernel,
        out_shape=jax.ShapeDtypeStruct((t, d), BF16),
        grid=(t // tm,),
        in_specs=[pl.BlockSpec((tm, d), lambda i: (i, 0)),
                  pl.BlockSpec((1, d), lambda i: (0, 0))],
        out_specs=pl.BlockSpec((tm, d), lambda i: (i, 0)),
        compiler_params=_cparams(("parallel",), 32),
        name="rmsnorm_cast",
    )(x2d, g.reshape(1, d))


def _inproj_kernel(h_ref, wq, wk, wv, wc, wb, wu, wga, wgb, qg_ref, kg_ref, bga_ref, bgb_ref,
                   q_o, kf_o, kb_o, vf_o, vb_o, z_o, b_o, ga_o, gb_o, *, q_scale):
    h = h_ref[...]

    def mm(w):
        return jnp.dot(h, w[...], preferred_element_type=F32)

    q_o[...] = _headnorm(mm(wq), qg_ref[...], q_scale).astype(q_o.dtype)
    k = _headnorm(mm(wk), kg_ref[...])
    kf_o[...] = k
    kb_o[...] = k.astype(BF16)
    v = mm(wv)
    vf_o[...] = v
    vb_o[...] = v.astype(BF16)
    z_o[...] = mm(wc) * mm(wu)
    b_o[...] = mm(wb).astype(b_o.dtype)
    ga_o[...] = jax.nn.sigmoid(mm(wga) + bga_ref[...]).astype(ga_o.dtype)
    gb_o[...] = jax.nn.sigmoid(mm(wgb) + bgb_ref[...]).astype(gb_o.dtype)


def _inproj(hb, w_in_bf, q_g, k_g, b_gate, q_dtype, aux_dtype, tm, tn):
    t = hb.shape[0]
    nb = D_MODEL // tn
    region = {"q": 0, "k": 1, "v": 2, "c": 3, "b": 4, "u": 5, "ga": 6, "gb": 7}

    def wspec(r):
        return pl.BlockSpec((D_MODEL, tn), lambda i, j, r=r: (0, r * nb + j))

    tile = pl.BlockSpec((tm, tn), lambda i, j: (i, j))
    vec = pl.BlockSpec((1, HEAD_DIM), lambda i, j: (0, 0))
    bias = pl.BlockSpec((1, tn), lambda i, j: (0, j))
    f32o = jax.ShapeDtypeStruct((t, D_MODEL), F32)
    bfo = jax.ShapeDtypeStruct((t, D_MODEL), BF16)
    return pl.pallas_call(
        functools.partial(_inproj_kernel, q_scale=HEAD_DIM ** -0.5 * LOG2E),
        out_shape=(jax.ShapeDtypeStruct((t, D_MODEL), q_dtype), f32o, bfo, f32o, bfo, f32o)
        + (jax.ShapeDtypeStruct((t, D_MODEL), aux_dtype),) * 3,
        grid=(t // tm, nb),
        in_specs=[pl.BlockSpec((tm, D_MODEL), lambda i, j: (i, 0))]
        + [wspec(region[n]) for n in ("q", "k", "v", "c", "b", "u", "ga", "gb")]
        + [vec, vec, bias, bias],
        out_specs=(tile,) * 9,
        compiler_params=_cparams(("parallel", "arbitrary"), 48),
        name="inproj",
    )(hb, *([w_in_bf] * 8), q_g.reshape(1, HEAD_DIM), k_g.reshape(1, HEAD_DIM),
      b_gate[0:1], b_gate[1:2])


def _lane_tiles(x):
    return [x[:, c * LANES:(c + 1) * LANES] for c in range(x.shape[1] // LANES)]


def _lane_repeat(x, n):
    return x if n == 1 else jnp.concatenate([x] * n, axis=1)


def _online_update(s_list, v_list, m_prev, l_prev, acc_prev, c_list=None):
    if c_list is None:
        c_list = [None] * len(s_list)
    tops = []
    for s, c in zip(s_list, c_list):
        top = functools.reduce(jnp.maximum, _lane_tiles(s))
        tops.append(top if c is None else top + c)
    m_cur = jnp.max(functools.reduce(jnp.maximum, tops), axis=1, keepdims=True)
    m_next = jnp.maximum(m_prev, m_cur)
    alpha = jnp.exp2(m_prev - m_next)
    l_next = alpha * l_prev
    pv = None
    for s, v, c in zip(s_list, v_list, c_list):
        shift = m_next if c is None else m_next - c
        p = jnp.exp2(s - _lane_repeat(shift, s.shape[1] // LANES))
        l_next = l_next + functools.reduce(jnp.add, _lane_tiles(p))
        d = v(p.astype(BF16)) if callable(v) else jnp.dot(p.astype(BF16), v, preferred_element_type=F32)
        pv = d if pv is None else pv + d
    acc_next = acc_prev * _lane_repeat(alpha, V_DIM // LANES) + pv
    return m_next, l_next, acc_next


def _diff_out(a1, l1, a2, l2, lam, g, out_scale):
    r1 = 1.0 / jnp.sum(l1, axis=1, keepdims=True)
    r2 = 1.0 / jnp.sum(l2, axis=1, keepdims=True)
    o = a1 * r1 - lam * (a2 * r2)
    return _rms(o, g) * out_scale


def _init_softmax_state(m_scr, l_scr, acc_scr):
    m_scr[...] = jnp.full(m_scr.shape, NEG, F32)
    l_scr[...] = jnp.zeros(l_scr.shape, F32)
    acc_scr[...] = jnp.zeros(acc_scr.shape, F32)


def _pattn_kernel(qi_ref, ki_ref, lam_ref, slopes_ref, q_ref, k_ref, v_ref, g_ref, o_ref, m_scr, l_scr, acc_scr,
                  *, t, hp, out_scale):
    h = pl.program_id(1)
    step = pl.program_id(2)
    qi = qi_ref[step]
    ki = ki_ref[step]

    @pl.when(ki == 0)
    def _():
        _init_softmax_state(m_scr, l_scr, acc_scr)

    def body(masked):
        kpos = ((ki - qi) * t + lax.broadcasted_iota(jnp.int32, (1, t), 1)).astype(F32)
        if masked:
            keep = (lax.broadcasted_iota(jnp.int32, (t, t), 1)
                    <= lax.broadcasted_iota(jnp.int32, (t, t), 0))
        state = []
        for i in range(hp):
            bias = slopes_ref[h * hp + i] * kpos
            v = v_ref[0, :, i * V_DIM:(i + 1) * V_DIM]
            for m in range(2):
                cols = slice(i * HEAD_W + m * HEAD_DIM, i * HEAD_W + (m + 1) * HEAD_DIM)
                s = lax.dot_general(q_ref[0, :, cols], k_ref[0, :, cols], (((1,), (1,)), ((), ())),
                                    preferred_element_type=F32)
                s = s + bias
                if masked:
                    s = jnp.where(keep, s, NEG)
                idx = 2 * i + m
                state.append(_online_update([s], [v], m_scr[idx], l_scr[idx], acc_scr[idx]))
        return state

    @pl.when(ki < qi)
    def _():
        for idx, (m_next, l_next, acc_next) in enumerate(body(False)):
            m_scr[idx] = m_next
            l_scr[idx] = l_next
            acc_scr[idx] = acc_next

    @pl.when(ki == qi)
    def _():
        state = body(True)
        lam = lam_ref[0]
        g = g_ref[...]
        o_ref[0] = jnp.concatenate(
            [_diff_out(state[2 * i][2], state[2 * i][1], state[2 * i + 1][2], state[2 * i + 1][1], lam, g, out_scale)
             for i in range(hp)], axis=1).astype(o_ref.dtype)


def _prompt_attn(qb, kb, vb, lam, slopes2, subln_g, out_scale, t, hp):
    b, s, _ = qb.shape
    n = s // t
    pairs = [(qi, ki) for qi in range(n) for ki in range(qi + 1)]
    qi_tab = jnp.asarray([p[0] for p in pairs], jnp.int32)
    ki_tab = jnp.asarray([p[1] for p in pairs], jnp.int32)
    smem = pl.BlockSpec(memory_space=pltpu.SMEM)
    grid_spec = pltpu.PrefetchScalarGridSpec(
        num_scalar_prefetch=2,
        grid=(b, N_HEADS // hp, len(pairs)),
        in_specs=[smem, smem,
                  pl.BlockSpec((1, t, hp * HEAD_W), lambda bi, h, st, qt, kt: (bi, qt[st], h)),
                  pl.BlockSpec((1, t, hp * HEAD_W), lambda bi, h, st, qt, kt: (bi, kt[st], h)),
                  pl.BlockSpec((1, t, hp * V_DIM), lambda bi, h, st, qt, kt: (bi, kt[st], h)),
                  pl.BlockSpec((1, V_DIM), lambda bi, h, st, qt, kt: (0, 0))],
        out_specs=pl.BlockSpec((1, t, hp * V_DIM), lambda bi, h, st, qt, kt: (bi, qt[st], h)),
        scratch_shapes=[pltpu.VMEM((2 * hp, t, LANES), F32), pltpu.VMEM((2 * hp, t, LANES), F32),
                        pltpu.VMEM((2 * hp, t, V_DIM), F32)],
    )
    return pl.pallas_call(
        functools.partial(_pattn_kernel, t=t, hp=hp, out_scale=out_scale),
        out_shape=jax.ShapeDtypeStruct((b, s, D_MODEL), BF16),
        grid_spec=grid_spec,
        compiler_params=_cparams(("parallel", "parallel", "arbitrary"), 40),
        name="prompt_attn",
    )(qi_tab, ki_tab, lam, slopes2, qb, kb, vb, subln_g.reshape(1, V_DIM))


def _sattn_bias(slopes2, tq):
    rows = jnp.arange(2 * N_HEADS * tq)
    row_h = (rows // tq) % N_HEADS
    slope = slopes2[row_h][:, None]
    lanes = jnp.arange(PAGE * N_HEADS)
    page = jnp.where(row_h[:, None] == (lanes % N_HEADS)[None, :], slope * (lanes // N_HEADS)[None, :].astype(F32), NEG)
    tok = jnp.arange(LANES)
    new = jnp.where(tok[None, :] <= (rows % tq)[:, None], slope * tok[None, :].astype(F32), NEG)
    return page.astype(F32), new.astype(F32)


def _sattn_kernel(pt_ref, lam_ref, q_ref, kn_ref, vn_ref, *rest, npg, n_steps, past, out_scale):
    del pt_ref
    k_refs = rest[:npg]
    v_refs = rest[npg:2 * npg]
    bias_ref, bias_new_ref, slope_ref, g_ref, o_ref, qbd_scr, qfull_scr, m_scr, l_scr, acc_scr = rest[2 * npg:]
    p = pl.program_id(1)
    tq = q_ref.shape[1]
    half = N_HEADS * tq

    @pl.when(p == 0)
    def _():
        _init_softmax_state(m_scr, l_scr, acc_scr)
        q = q_ref[0]
        zero = jnp.zeros((half, HEAD_DIM), F32)
        q1, q2 = [jnp.concatenate([q[:, h * HEAD_W + m * HEAD_DIM:h * HEAD_W + (m + 1) * HEAD_DIM]
                                   for h in range(N_HEADS)], axis=0) for m in range(2)]
        qbd_scr[...] = jnp.concatenate([jnp.concatenate([q1, zero], axis=1),
                                        jnp.concatenate([zero, q2], axis=1)], axis=0).astype(BF16)
        q_rep = jnp.concatenate([q] * N_HEADS, axis=0)
        lane = lax.broadcasted_iota(jnp.int32, q_rep.shape, 1)
        own_head = lane // HEAD_W == lax.broadcasted_iota(jnp.int32, q_rep.shape, 0) // tq
        qfull_scr[...] = jnp.concatenate(
            [jnp.where(own_head & ((lane // HEAD_DIM) % 2 == m), q_rep, 0.0) for m in range(2)],
            axis=0).astype(BF16)

    nt = (((1,), (1,)), ((), ()))

    def body(last):
        qbd = qbd_scr[...]
        bias = bias_ref[...]
        slope = slope_ref[...]
        s_list, v_list, c_list = [], [], []
        for j in range(npg):
            s = lax.dot_general(qbd, k_refs[j][0].astype(BF16), nt, preferred_element_type=F32)
            s_list.append(s + bias)
            v_list.append(v_refs[j][0].astype(BF16))
            c_list.append(slope * ((p * npg + j) * PAGE - past).astype(F32))
        if last:
            pad = jnp.zeros((LANES - tq, D_MODEL), F32)
            kn = jnp.concatenate([kn_ref[0], pad], axis=0).astype(BF16)
            vn = jnp.concatenate([vn_ref[0], pad], axis=0).astype(BF16)
            s = lax.dot_general(qfull_scr[...], kn, nt, preferred_element_type=F32)
            s_list.append(s + bias_new_ref[...])

            def new_values(pb):
                full = jnp.dot(pb, vn, preferred_element_type=F32)
                return jnp.concatenate([full[(m * N_HEADS + h) * tq:(m * N_HEADS + h + 1) * tq,
                                             h * V_DIM:(h + 1) * V_DIM]
                                        for m in range(2) for h in range(N_HEADS)], axis=0)

            v_list.append(new_values)
            c_list.append(None)
        return _online_update(s_list, v_list, m_scr[...], l_scr[...], acc_scr[...], c_list)

    @pl.when(p < n_steps - 1)
    def _():
        m_next, l_next, acc_next = body(False)
        m_scr[...] = m_next
        l_scr[...] = l_next
        acc_scr[...] = acc_next

    @pl.when(p == n_steps - 1)
    def _():
        _, l, acc = body(True)
        y = _diff_out(acc[:half], l[:half], acc[half:], l[half:], lam_ref[0], g_ref[...], out_scale)
        o_ref[0] = jnp.concatenate([y[h * tq:(h + 1) * tq] for h in range(N_HEADS)], axis=1)


def _sample_attn(qs, kn, vn, pool_k, pool_v, page_table, lam, slopes2, subln_g, out_scale, npg):
    db, tq, _ = qs.shape
    n_pages = page_table.shape[1]
    n_steps = n_pages // npg
    n_phys = pool_k.shape[0]
    rows = 2 * N_HEADS * tq
    page_rows = PAGE * N_HEADS
    pk = pool_k.reshape(n_phys, page_rows, HEAD_W)
    pv = pool_v.reshape(n_phys, page_rows, V_DIM)
    bias, bias_new = _sattn_bias(slopes2, tq)
    slope_rep = jnp.broadcast_to(slopes2[(jnp.arange(rows) // tq) % N_HEADS][:, None], (rows, LANES))
    smem = pl.BlockSpec(memory_space=pltpu.SMEM)
    tok = pl.BlockSpec((1, tq, D_MODEL), lambda b, p, pt: (b, 0, 0))
    const = lambda shp: pl.BlockSpec(shp, lambda b, p, pt: (0, 0))

    def page(j):
        return pl.BlockSpec((1, page_rows, HEAD_W), lambda b, p, pt, j=j: (pt[b, p * npg + j], 0, 0))

    grid_spec = pltpu.PrefetchScalarGridSpec(
        num_scalar_prefetch=1,
        grid=(db, n_steps),
        in_specs=[smem, tok, tok, tok] + [page(j) for j in range(npg)] * 2
        + [const((rows, page_rows)), const((rows, LANES)), const((rows, LANES)), const((1, V_DIM))],
        out_specs=tok,
        scratch_shapes=[pltpu.VMEM((rows, HEAD_W), BF16), pltpu.VMEM((rows, D_MODEL), BF16),
                        pltpu.VMEM((rows, LANES), F32), pltpu.VMEM((rows, LANES), F32),
                        pltpu.VMEM((rows, V_DIM), F32)],
    )
    return pl.pallas_call(
        functools.partial(_sattn_kernel, npg=npg, n_steps=n_steps, past=n_pages * PAGE, out_scale=out_scale),
        out_shape=jax.ShapeDtypeStruct((db, tq, D_MODEL), F32),
        grid_spec=grid_spec,
        compiler_params=_cparams(("parallel", "arbitrary"), 48),
        name="sample_attn",
    )(page_table, lam, qs, kn, vn, *([pk] * npg), *([pv] * npg), bias, bias_new, slope_rep,
      subln_g.reshape(1, V_DIM))


def _conv_merge(o, ga, gb, b, z, z1, z2, cw):
    y = z2 * cw[0] + z1 * cw[1] + z * cw[2]
    return (ga.astype(F32) * o.astype(F32) + gb.astype(F32) * (b.astype(F32) * y)).astype(BF16)


def _mix_outproj_kernel(o_ref, ga_ref, gb_ref, b_ref, z_ref, zp_ref, cw_ref, w_ref, r_ref, out_ref, merged_scr,
                        *, tiles_per_seq):
    i = pl.program_id(0)

    @pl.when(pl.program_id(1) == 0)
    def _():
        z = z_ref[...]
        tm = z.shape[0]
        zp = jnp.where(i % tiles_per_seq == 0, 0.0, zp_ref[...])
        pm1 = zp[7:8, :]
        pm2 = zp[6:7, :]
        row = lax.broadcasted_iota(jnp.int32, (tm, 1), 0)
        z1 = jnp.where(row == 0, pm1, pltpu.roll(z, 1, 0))
        z2 = jnp.where(row == 0, pm2, jnp.where(row == 1, pm1, pltpu.roll(z, 2, 0)))
        cw = cw_ref[...]
        merged_scr[...] = _conv_merge(o_ref[...], ga_ref[...], gb_ref[...], b_ref[...], z, z1, z2,
                                      (cw[0:1], cw[1:2], cw[2:3]))

    out_ref[...] = r_ref[...] + jnp.dot(merged_scr[...], w_ref[...], preferred_element_type=F32)


def _mix_outproj_prompt(o, ga, gb, b, z, conv_w, w, r, seq, tm, tn):
    t = z.shape[0]
    n = w.shape[1]
    row = pl.BlockSpec((tm, D_MODEL), lambda i, j: (i, 0))
    prev = pl.BlockSpec((8, D_MODEL), lambda i, j: (jnp.maximum(i * (tm // 8) - 1, 0), 0))
    tile = pl.BlockSpec((tm, tn), lambda i, j: (i, j))
    return pl.pallas_call(
        functools.partial(_mix_outproj_kernel, tiles_per_seq=seq // tm),
        out_shape=jax.ShapeDtypeStruct((t, n), F32),
        grid=(t // tm, n // tn),
        in_specs=[row, row, row, row, row, prev, pl.BlockSpec((3, D_MODEL), lambda i, j: (0, 0)),
                  pl.BlockSpec((D_MODEL, tn), lambda i, j: (0, j)), tile],
        out_specs=tile,
        scratch_shapes=[pltpu.VMEM((tm, D_MODEL), BF16)],
        compiler_params=_cparams(("parallel", "arbitrary"), 48),
        name="mix_outproj",
    )(o, ga, gb, b, z, z, conv_w, w, r)


def _mix_sample_kernel(o_ref, ga_ref, gb_ref, b_ref, z_ref, cc_ref, cw_ref, out_ref):
    z = z_ref[...]
    cc = cc_ref[...]
    c0 = cc[:, 0:1, :]
    c1 = cc[:, 1:2, :]
    row = lax.broadcasted_iota(jnp.int32, (1, z.shape[1], 1), 1)
    z1 = jnp.where(row == 0, c1, pltpu.roll(z, 1, 1))
    z2 = jnp.where(row == 0, c0, jnp.where(row == 1, c1, pltpu.roll(z, 2, 1)))
    cw = cw_ref[...]
    out_ref[...] = _conv_merge(o_ref[...], ga_ref[...], gb_ref[...], b_ref[...], z, z1, z2,
                               (cw[0:1][None], cw[1:2][None], cw[2:3][None]))


def _mix_sample(o, ga, gb, b, z, cache_conv, conv_w, g):
    db, ts, _ = z.shape
    tile = pl.BlockSpec((g, ts, D_MODEL), lambda i: (i, 0, 0))
    return pl.pallas_call(
        _mix_sample_kernel,
        out_shape=jax.ShapeDtypeStruct((db, ts, D_MODEL), BF16),
        grid=(db // g,),
        in_specs=[tile, tile, tile, tile, tile,
                  pl.BlockSpec((g, cache_conv.shape[1], D_MODEL), lambda i: (i, 0, 0)),
                  pl.BlockSpec((3, D_MODEL), lambda i: (0, 0))],
        out_specs=tile,
        compiler_params=_cparams(("parallel",), 40),
        name="mix_sample",
    )(o, ga, gb, b, z, cache_conv, conv_w)


def _mm_res_kernel(a_ref, w_ref, r_ref, o_ref):
    o_ref[...] = r_ref[...] + jnp.dot(a_ref[...], w_ref[...], preferred_element_type=F32)


def _mm_res(a, w, r, tm, tn):
    t, k = a.shape
    n = w.shape[1]
    return pl.pallas_call(
        _mm_res_kernel,
        out_shape=jax.ShapeDtypeStruct((t, n), F32),
        grid=(t // tm, n // tn),
        in_specs=[pl.BlockSpec((tm, k), lambda i, j: (i, 0)),
                  pl.BlockSpec((k, tn), lambda i, j: (0, j)),
                  pl.BlockSpec((tm, tn), lambda i, j: (i, j))],
        out_specs=pl.BlockSpec((tm, tn), lambda i, j: (i, j)),
        compiler_params=_cparams(("parallel", "parallel"), 48),
        name="outproj",
    )(a, w, r)


def _memkv_kernel(h_ref, wk_ref, wv_ref, g_ref, kf_o, kb_o, vf_o, vb_o):
    h = h_ref[...]
    k = _headnorm(jnp.dot(h, wk_ref[...], preferred_element_type=F32), g_ref[...])
    kf_o[...] = k
    kb_o[...] = k.astype(BF16)
    v = jnp.dot(h, wv_ref[...], preferred_element_type=F32)
    vf_o[...] = v
    vb_o[...] = v.astype(BF16)


def _mem_kv(mem_nb, w_xkv_bf, xk_g):
    t = mem_nb.shape[0]
    f32o = jax.ShapeDtypeStruct((t, X_WIDTH), F32)
    bfo = jax.ShapeDtypeStruct((t, X_WIDTH), BF16)
    full = lambda s: pl.BlockSpec(s, lambda i: (0,) * len(s))
    return pl.pallas_call(
        _memkv_kernel,
        out_shape=(f32o, bfo, f32o, bfo),
        grid=(1,),
        in_specs=[full((t, D_MODEL)),
                  pl.BlockSpec((D_MODEL, X_WIDTH), lambda i: (0, 0)),
                  pl.BlockSpec((D_MODEL, X_WIDTH), lambda i: (0, 1)),
                  full((1, X_HEAD_DIM))],
        out_specs=(full((t, X_WIDTH)),) * 4,
        compiler_params=_cparams(("arbitrary",), 32),
        name="mem_kv",
    )(mem_nb, w_xkv_bf, w_xkv_bf, xk_g.reshape(1, X_HEAD_DIM))


def _xattn_head(qh, kh, vh):
    s = lax.dot_general(qh, kh, (((1,), (1,)), ((), ())), preferred_element_type=F32)
    e = jnp.exp(s - jnp.max(s, axis=1, keepdims=True))
    p = e * (1.0 / jnp.sum(e, axis=1, keepdims=True))
    return jnp.dot(p.astype(BF16), vh, preferred_element_type=F32)


def _xattn_q(x, g_ref, wq_ref, qg_ref):
    hn = _rms(x, g_ref[...]).astype(BF16)
    q = jnp.dot(hn, wq_ref[...], preferred_element_type=F32)
    return _headnorm(q, qg_ref[...], X_HEAD_DIM ** -0.5)


def _xattn_prompt_kernel(x_ref, g_ref, wq_ref, qg_ref, mk_ref, mv_ref, wo_ref, o_ref):
    x = x_ref[0]
    q = _xattn_q(x, g_ref, wq_ref, qg_ref).astype(BF16)
    mk = mk_ref[0]
    mv = mv_ref[0]
    heads = []
    for g in range(X_HEADS):
        sl = slice(g * X_HEAD_DIM, (g + 1) * X_HEAD_DIM)
        heads.append(_xattn_head(q[:, sl], mk[:, sl], mv[:, sl]))
    o = jnp.concatenate(heads, axis=1).astype(BF16)
    o_ref[0] = x + jnp.dot(o, wo_ref[...], preferred_element_type=F32)


def _xattn_prompt(x, g, w_xq_bf, xq_g, mk_b, mv_b, w_xo_bf, tm):
    b, s, _ = x.shape
    tile = pl.BlockSpec((1, tm, D_MODEL), lambda bi, i: (bi, i, 0))
    mem = pl.BlockSpec((1, N_MEM, X_WIDTH), lambda bi, i: (bi, 0, 0))
    const = lambda shp: pl.BlockSpec(shp, lambda bi, i: (0, 0))
    return pl.pallas_call(
        _xattn_prompt_kernel,
        out_shape=jax.ShapeDtypeStruct((b, s, D_MODEL), F32),
        grid=(b, s // tm),
        in_specs=[tile, const((1, D_MODEL)), const((D_MODEL, X_WIDTH)), const((1, X_HEAD_DIM)),
                  mem, mem, const((X_WIDTH, D_MODEL))],
        out_specs=tile,
        compiler_params=_cparams(("parallel", "parallel"), 48),
        name="xattn_prompt",
    )(x, g.reshape(1, D_MODEL), w_xq_bf, xq_g.reshape(1, X_HEAD_DIM), mk_b, mv_b, w_xo_bf)


def _xattn_sample_kernel(x_ref, g_ref, wq_ref, qg_ref, mk_ref, mv_ref, mask_ref, wo_ref, o_ref, *, ts):
    x = x_ref[...]
    q = _xattn_q(x, g_ref, wq_ref, qg_ref)
    mask = mask_ref[...]
    nt = (((1,), (1,)), ((), ()))
    per_batch = []
    for bi in range(mk_ref.shape[0]):
        rows = slice(bi * ts, (bi + 1) * ts)
        qs = jnp.concatenate([q[rows, g * X_HEAD_DIM:(g + 1) * X_HEAD_DIM] for g in range(X_HEADS)], axis=0)
        s = lax.dot_general(qs.astype(BF16), mk_ref[bi].astype(BF16), nt, preferred_element_type=F32) + mask
        e = jnp.exp(s - jnp.max(s, axis=1, keepdims=True))
        o = jnp.dot(e.astype(BF16), mv_ref[bi].astype(BF16), preferred_element_type=F32)
        o = o * (1.0 / jnp.sum(e, axis=1, keepdims=True))
        per_batch.append(jnp.concatenate([o[g * ts:(g + 1) * ts] for g in range(X_HEADS)], axis=1))
    a = jnp.concatenate(per_batch, axis=0).astype(BF16)
    o_ref[...] = x + jnp.dot(a, wo_ref[...], preferred_element_type=F32)


def _xattn_sample(x2d, g, w_xq_bf, xq_g, cache_mk, cache_mv, w_xo_bf, ts, gb):
    t = x2d.shape[0]
    db = cache_mk.shape[0]
    mk = cache_mk.reshape(db, N_MEM * X_HEADS, X_HEAD_DIM)
    mv = cache_mv.reshape(db, N_MEM * X_HEADS, X_HEAD_DIM)
    tile = pl.BlockSpec((gb * ts, D_MODEL), lambda i: (i, 0))
    mem = pl.BlockSpec((gb, N_MEM * X_HEADS, X_HEAD_DIM), lambda i: (i, 0, 0))
    const = lambda shp: pl.BlockSpec(shp, lambda i: (0, 0))
    row_head = jnp.arange(X_HEADS * ts) // ts
    lane_head = jnp.arange(N_MEM * X_HEADS) % X_HEADS
    mask = jnp.where(row_head[:, None] == lane_head[None, :], 0.0, NEG).astype(F32)
    return pl.pallas_call(
        functools.partial(_xattn_sample_kernel, ts=ts),
        out_shape=jax.ShapeDtypeStruct((t, D_MODEL), F32),
        grid=(db // gb,),
        in_specs=[tile, const((1, D_MODEL)), const((D_MODEL, X_WIDTH)), const((1, X_HEAD_DIM)),
                  mem, mem, const((X_HEADS * ts, N_MEM * X_HEADS)), const((X_WIDTH, D_MODEL))],
        out_specs=tile,
        compiler_params=_cparams(("parallel",), 48),
        name="xattn_sample",
    )(x2d, g.reshape(1, D_MODEL), w_xq_bf, xq_g.reshape(1, X_HEAD_DIM), mk, mv, mask, w_xo_bf)


def _first_argmax(v, lane):
    vmax = jnp.max(v, axis=1, keepdims=True)
    idx = jnp.min(jnp.where(v == vmax, lane, LANES), axis=1, keepdims=True)
    return vmax, idx


def _router_kernel(xp_ref, xs_ref, g_ref, wr_ref, br_ref, h_o, eid_o, w_o, *, n_prompt_tiles):
    i = pl.program_id(0)

    def body(x_ref):
        h = _rms(x_ref[...], g_ref[...])
        h_o[...] = h
        logits = jnp.dot(h, wr_ref[...], preferred_element_type=F32,
                         precision=lax.Precision.HIGHEST) + br_ref[...]
        lane = lax.broadcasted_iota(jnp.int32, logits.shape, 1)
        ninf = -jnp.inf
        gl = jnp.where(lane < N_GROUPS, logits, ninf)
        gmax, gidx = _first_argmax(gl, lane)
        g_w = 1.0 / jnp.sum(jnp.exp(gl - gmax), axis=1, keepdims=True)
        lo = N_GROUPS + gidx * EXPERTS_PER_GROUP
        el = jnp.where((lane >= lo) & (lane < lo + EXPERTS_PER_GROUP), logits, ninf)
        v1, i1 = _first_argmax(el, lane)
        v2, i2 = _first_argmax(jnp.where(lane == i1, ninf, el), lane)
        e2 = jnp.exp(v2 - v1)
        w1 = g_w / (1.0 + e2)
        w2 = g_w * e2 / (1.0 + e2)
        eid_o[...] = jnp.where(lane == 0, i1 - N_GROUPS, jnp.where(lane == 1, i2 - N_GROUPS, 0))
        w_o[...] = jnp.where(lane == 0, w1, jnp.where(lane == 1, w2, 0.0))

    @pl.when(i < n_prompt_tiles)
    def _():
        body(xp_ref)

    @pl.when(i >= n_prompt_tiles)
    def _():
        body(xs_ref)


def _router(xp2d, xs2d, g, w_r, b_r, tm):
    tp, ts = xp2d.shape[0], xs2d.shape[0]
    npt, nst = tp // tm, ts // tm
    t = tp + ts
    tile = pl.BlockSpec((tm, D_MODEL), lambda i: (i, 0))
    const = lambda shp: pl.BlockSpec(shp, lambda i: (0, 0))
    return pl.pallas_call(
        functools.partial(_router_kernel, n_prompt_tiles=npt),
        out_shape=(jax.ShapeDtypeStruct((t, D_MODEL), F32),
                   jax.ShapeDtypeStruct((t, LANES), jnp.int32),
                   jax.ShapeDtypeStruct((t, LANES), F32)),
        grid=(npt + nst,),
        in_specs=[pl.BlockSpec((tm, D_MODEL), lambda i: (jnp.minimum(i, npt - 1), 0)),
                  pl.BlockSpec((tm, D_MODEL), lambda i: (jnp.maximum(i - npt, 0), 0)),
                  const((1, D_MODEL)), const((D_MODEL, LANES)), const((1, LANES))],
        out_specs=(tile, pl.BlockSpec((tm, LANES), lambda i: (i, 0)), pl.BlockSpec((tm, LANES), lambda i: (i, 0))),
        compiler_params=_cparams(("parallel",), 40),
        name="moe_router",
    )(xp2d, xs2d, g.reshape(1, D_MODEL), w_r, b_r)


def _dispatch_kernel(dest_ref, zflag_ref, h_ref, rows_hbm, zero_scr, sem, *, tmd, tm, n_tiles):
    i = pl.program_id(0)

    def zero_tile(t, start):
        @pl.when(zflag_ref[t] != 0)
        def _():
            cp = pltpu.make_async_copy(zero_scr, rows_hbm.at[pl.ds(pl.multiple_of(t * tm, tm), tm)], sem.at[2])
            if start:
                cp.start()
            else:
                cp.wait()

    @pl.when(i == 0)
    def _():
        zero_scr[...] = jnp.zeros(zero_scr.shape, zero_scr.dtype)
        lax.fori_loop(0, n_tiles, lambda t, c: (zero_tile(t, True), c)[1], 0)
        lax.fori_loop(0, n_tiles, lambda t, c: (zero_tile(t, False), c)[1], 0)

    base = 2 * i * tmd

    def row_copy(r, slot):
        return pltpu.make_async_copy(h_ref.at[pl.ds(r, 1)], rows_hbm.at[pl.ds(dest_ref[base + 2 * r + slot], 1)],
                                     sem.at[slot])

    def start(r, carry):
        row_copy(r, 0).start(priority=0)
        row_copy(r, 1).start(priority=1)
        return carry

    lax.fori_loop(0, tmd, start, 0)
    for slot in range(2):
        pltpu.make_async_copy(h_ref, rows_hbm.at[pl.ds(0, tmd)], sem.at[slot]).wait()


def _dispatch_rows(hf, dest_flat, zflag, n_tiles, tm, tmd):
    t, d = hf.shape
    grid_spec = pltpu.PrefetchScalarGridSpec(
        num_scalar_prefetch=2,
        grid=(t // tmd,),
        in_specs=[pl.BlockSpec((tmd, d), lambda i, dest, zf: (i, 0))],
        out_specs=pl.BlockSpec(memory_space=pl.ANY),
        scratch_shapes=[pltpu.VMEM((tm, d), hf.dtype), pltpu.SemaphoreType.DMA((3,))],
    )
    return pl.pallas_call(
        functools.partial(_dispatch_kernel, tmd=tmd, tm=tm, n_tiles=n_tiles),
        out_shape=jax.ShapeDtypeStruct((n_tiles * tm, d), hf.dtype),
        grid_spec=grid_spec,
        compiler_params=_cparams(("arbitrary",), 32),
        name="moe_dispatch",
    )(dest_flat, zflag, hf)


def _ffn_kernel(te_ref, ne_ref, na_ref, x_ref, wgu_hbm, wd_hbm, o_ref, wgu_land, wd_land, wgu_scr, wd_scr, sem):
    t = pl.program_id(0)

    def weight_copies(e):
        return (pltpu.make_async_copy(wgu_hbm.at[e], wgu_land, sem.at[0]),
                pltpu.make_async_copy(wd_hbm.at[e], wd_land, sem.at[1]))

    @pl.when(t < na_ref[0])
    def _():
        e = te_ref[t]

        @pl.when(t == 0)
        def _():
            for cp in weight_copies(e):
                cp.start()

        @pl.when((t == 0) | (e != te_ref[jnp.maximum(t - 1, 0)]))
        def _():
            for cp in weight_copies(e):
                cp.wait()
            wgu_scr[...] = wgu_land[...].astype(BF16)
            wd_scr[...] = wd_land[...].astype(BF16)

            @pl.when(ne_ref[t] >= 0)
            def _():
                for cp in weight_copies(ne_ref[t]):
                    cp.start()

        gu = jnp.dot(x_ref[...].astype(BF16), wgu_scr[...], preferred_element_type=F32)
        gate = gu[:, :D_FF]
        act = (gate * jax.nn.sigmoid(gate)) * gu[:, D_FF:]
        o_ref[...] = jnp.dot(act.astype(BF16), wd_scr[...], preferred_element_type=F32)

    @pl.when(t >= na_ref[0])
    def _():
        o_ref[...] = jnp.zeros(o_ref.shape, o_ref.dtype)


def _expert_ffn(xs, tile_e, next_e, n_active, w_gu, w_down, tm):
    n_tiles = xs.shape[0] // tm
    grid_spec = pltpu.PrefetchScalarGridSpec(
        num_scalar_prefetch=3,
        grid=(n_tiles,),
        in_specs=[pl.BlockSpec((tm, D_MODEL), lambda t, te, ne, na: (jnp.minimum(t, na[0] - 1), 0)),
                  pl.BlockSpec(memory_space=pl.ANY), pl.BlockSpec(memory_space=pl.ANY)],
        out_specs=pl.BlockSpec((tm, D_MODEL), lambda t, te, ne, na: (t, 0)),
        scratch_shapes=[pltpu.VMEM((D_MODEL, 2 * D_FF), F32), pltpu.VMEM((D_FF, D_MODEL), F32),
                        pltpu.VMEM((D_MODEL, 2 * D_FF), BF16), pltpu.VMEM((D_FF, D_MODEL), BF16),
                        pltpu.SemaphoreType.DMA((2,))],
    )
    return pl.pallas_call(
        _ffn_kernel,
        out_shape=jax.ShapeDtypeStruct((n_tiles * tm, D_MODEL), F32),
        grid_spec=grid_spec,
        compiler_params=_cparams(("arbitrary",), 48),
        name="moe_ffn",
    )(tile_e, next_e, n_active, xs, w_gu, w_down)


def _combine_kernel(dest_ref, x_ref, w_ref, ys_hbm, o_ref, buf_a, buf_b, sem, *, tm, base_tok):
    base = 2 * (base_tok + pl.program_id(0) * tm)
    bufs = (buf_a, buf_b)

    def start(r, carry):
        for slot in range(2):
            pltpu.make_async_copy(ys_hbm.at[pl.ds(dest_ref[base + 2 * r + slot], 1)], bufs[slot].at[pl.ds(r, 1)],
                                  sem.at[slot]).start(priority=slot)
        return carry

    lax.fori_loop(0, tm, start, 0)
    for slot in range(2):
        pltpu.make_async_copy(ys_hbm.at[pl.ds(0, tm)], bufs[slot], sem.at[slot]).wait()
    w = w_ref[...]
    o_ref[...] = x_ref[...] + (w[:, 0:1] * buf_a[...] + w[:, 1:2] * buf_b[...])


def _combine(x2d, wts, ys, dest_flat, base_tok, tm):
    t, d = x2d.shape
    base_blk = base_tok // tm
    grid_spec = pltpu.PrefetchScalarGridSpec(
        num_scalar_prefetch=1,
        grid=(t // tm,),
        in_specs=[pl.BlockSpec((tm, d), lambda i, dest: (i, 0)),
                  pl.BlockSpec((tm, LANES), lambda i, dest: (i + base_blk, 0)),
                  pl.BlockSpec(memory_space=pl.ANY)],
        out_specs=pl.BlockSpec((tm, d), lambda i, dest: (i, 0)),
        scratch_shapes=[pltpu.VMEM((tm, d), F32), pltpu.VMEM((tm, d), F32), pltpu.SemaphoreType.DMA((2,))],
    )
    return pl.pallas_call(
        functools.partial(_combine_kernel, tm=tm, base_tok=base_tok),
        out_shape=jax.ShapeDtypeStruct((t, d), F32),
        grid_spec=grid_spec,
        compiler_params=_cparams(("arbitrary",), 32),
        name="moe_combine",
    )(dest_flat, x2d, wts, ys)


def _route_plan(eid, tm, n_tiles):
    flat_e = eid.reshape(-1)
    onehot = (flat_e[:, None] == jnp.arange(N_EXPERTS, dtype=jnp.int32)[None, :]).astype(jnp.int32)
    csum = jnp.cumsum(onehot, axis=0)
    rank = jnp.sum(onehot * csum, axis=1) - 1
    counts = csum[-1]
    padded = ((counts + tm - 1) // tm) * tm
    ends = jnp.cumsum(padded)
    dest = (jnp.sum(onehot * (ends - padded)[None, :], axis=1) + rank).astype(jnp.int32)
    n_active = (ends[-1] // tm).astype(jnp.int32)
    tile = jnp.arange(n_tiles, dtype=jnp.int32)
    active = tile < n_active
    tile_e = jnp.sum((ends[None, :] <= (tile * tm)[:, None]).astype(jnp.int32), axis=1)
    last_e = jnp.sum(jnp.where(tile == n_active - 1, tile_e, 0))
    tile_e = jnp.where(active, tile_e, last_e).astype(jnp.int32)
    has_pad = jnp.any((ends[None, :] == ((tile + 1) * tm)[:, None]) & (padded != counts)[None, :], axis=1)
    zflag = (has_pad | ~active).astype(jnp.int32)
    later = active[None, :] & (tile_e[None, :] > tile_e[:, None])
    next_e = jnp.min(jnp.where(later, tile_e[None, :], N_EXPERTS), axis=1)
    next_e = jnp.where(next_e < N_EXPERTS, next_e, -1).astype(jnp.int32)
    return dest, tile_e, next_e, n_active.reshape(1), zflag


def kernel(x_prompt, x_sample, cache_attn_k, cache_attn_v, cache_conv, cache_mem_k, cache_mem_v, page_table, mem_prompt, norm_mix_g, w_in, b_gate, q_norm_g, k_norm_g, lambda_q1, lambda_k1, lambda_q2, lambda_k2, subln_g, conv_w, w_out, norm_mem_g, norm_x_g, w_xq, w_xkv, xq_norm_g, xk_norm_g, w_xo, norm_ffn_g, w_group, b_group, w_expert_router, b_expert_router, w_gate_up, w_down):
    depth = w_in.shape[0]
    bp, sp, _ = x_prompt.shape
    bs, ts, _ = x_sample.shape
    tp, tsn = bp * sp, bs * ts
    slopes = jnp.exp2(-8.0 * jnp.arange(1, N_HEADS + 1, dtype=F32) / N_HEADS) * LOG2E
    xp = x_prompt.reshape(tp, D_MODEL)
    xs = x_sample.reshape(tsn, D_MODEL)
    moe_tm = 256
    moe_tiles = (tp + tsn) * 2 // moe_tm + N_EXPERTS
    outs = [[] for _ in range(8)]
    for l in range(depth):
        lam_init = 0.8 - 0.6 * math.exp(-0.3 * l)
        lam = (jnp.exp(jnp.sum(lambda_q1[l] * lambda_k1[l])) - jnp.exp(jnp.sum(lambda_q2[l] * lambda_k2[l]))
               + lam_init).reshape(1).astype(F32)
        out_scale = 1.0 - lam_init
        w_in_bf = w_in[l].astype(BF16)
        w_out_bf = w_out[l].astype(BF16)
        w_xq_bf = w_xq[l].astype(BF16)
        w_xkv_bf = w_xkv[l].astype(BF16)
        w_xo_bf = w_xo[l].astype(BF16)

        hb = _rmsnorm_cast(xp, norm_mix_g[l], 512)
        q, kf, kb, vf, vb, z, b, ga, gb = _inproj(hb, w_in_bf, q_norm_g[l], k_norm_g[l], b_gate[l], BF16, BF16,
                                                  1024, 256)
        o = _prompt_attn(q.reshape(bp, sp, D_MODEL), kb.reshape(bp, sp, D_MODEL), vb.reshape(bp, sp, D_MODEL),
                         lam, slopes, subln_g[l], out_scale, 512, 4)
        xp = _mix_outproj_prompt(o.reshape(tp, D_MODEL), ga, gb, b, z, conv_w[l], w_out_bf, xp, sp, 256, D_MODEL)
        outs[0].append(kf.reshape(bp, sp, N_HEADS, HEAD_W))
        outs[1].append(vf.reshape(bp, sp, N_HEADS, V_DIM))
        outs[2].append(z.reshape(bp, sp, D_MODEL)[:, sp - 2:, :])

        hb = _rmsnorm_cast(xs, norm_mix_g[l], 512)
        q, kf, kb, vf, vb, z, b, ga, gb = _inproj(hb, w_in_bf, q_norm_g[l], k_norm_g[l], b_gate[l], F32, F32,
                                                  1024, 256)
        shp = (bs, ts, D_MODEL)
        o = _sample_attn(q.reshape(shp), kf.reshape(shp), vf.reshape(shp), cache_attn_k[l], cache_attn_v[l],
                         page_table, lam, slopes, subln_g[l], out_scale, 8)
        z3 = z.reshape(shp)
        merged = _mix_sample(o, ga.reshape(shp), gb.reshape(shp), b.reshape(shp), z3, cache_conv[l], conv_w[l], 32)
        xs = _mm_res(merged.reshape(tsn, D_MODEL), w_out_bf, xs, 512, 1024)
        outs[5].append(kf.reshape(bs, ts, N_HEADS, HEAD_W))
        outs[6].append(vf.reshape(bs, ts, N_HEADS, V_DIM))
        outs[7].append(z3[:, ts - 2:, :])

        mem_nb = _rmsnorm_cast(mem_prompt.reshape(bp * N_MEM, D_MODEL), norm_mem_g[l], 512)
        mkf, mkb, mvf, mvb = _mem_kv(mem_nb, w_xkv_bf, xk_norm_g[l])
        xp = _xattn_prompt(xp.reshape(bp, sp, D_MODEL), norm_x_g[l], w_xq_bf, xq_norm_g[l],
                           mkb.reshape(bp, N_MEM, X_WIDTH), mvb.reshape(bp, N_MEM, X_WIDTH), w_xo_bf,
                           512).reshape(tp, D_MODEL)
        xs = _xattn_sample(xs, norm_x_g[l], w_xq_bf, xq_norm_g[l], cache_mem_k[l], cache_mem_v[l], w_xo_bf, ts, 8)
        outs[3].append(mkf.reshape(bp, N_MEM, X_HEADS, X_HEAD_DIM))
        outs[4].append(mvf.reshape(bp, N_MEM, X_HEADS, X_HEAD_DIM))

        w_r = jnp.zeros((D_MODEL, LANES), F32).at[:, :N_GROUPS].set(w_group[l])
        w_r = w_r.at[:, N_GROUPS:N_GROUPS + N_EXPERTS].set(w_expert_router[l])
        b_r = jnp.zeros((1, LANES), F32).at[0, :N_GROUPS].set(b_group[l])
        b_r = b_r.at[0, N_GROUPS:N_GROUPS + N_EXPERTS].set(b_expert_router[l])
        hf, eid, wts = _router(xp, xs, norm_ffn_g[l], w_r, b_r, 512)
        dest, tile_e, next_e, n_active, zflag = _route_plan(eid[:, :2], moe_tm, moe_tiles)
        rows = _dispatch_rows(hf, dest, zflag, moe_tiles, moe_tm, 256)
        ys = _expert_ffn(rows, tile_e, next_e, n_active, w_gate_up[l], w_down[l], moe_tm)
        xp = _combine(xp, wts, ys, dest, 0, moe_tm)
        xs = _combine(xs, wts, ys, dest, tp, moe_tm)
    stack = lambda i: jnp.stack(outs[i])
    return (xp.reshape(bp, sp, D_MODEL), xs.reshape(bs, ts, D_MODEL), stack(0), stack(1), stack(2), stack(3),
            stack(4), stack(5), stack(6), stack(7))
```

```python
import functools
import math

import jax
import jax.numpy as jnp
from jax import lax
from jax.experimental import pallas as pl
from jax.experimental.pallas import tpu as pltpu

F32 = jnp.float32
BF16 = jnp.bfloat16

D_MODEL = 2048
N_HEADS = 8
HEAD_DIM = 128
V_DIM = 256
HEAD_W = 2 * HEAD_DIM
PAGE = 128
X_HEADS = 4
X_HEAD_DIM = 128
X_WIDTH = X_HEADS * X_HEAD_DIM
N_MEM = 256
N_GROUPS = 4
EXPERTS_PER_GROUP = 8
N_EXPERTS = N_GROUPS * EXPERTS_PER_GROUP
D_FF = 512
EPS = 1e-6
NEG = -1e30
LANES = 128
MIB = 1024 * 1024
LOG2E = math.log2(math.e)


def _cparams(sem, vmem_mib):
    return pltpu.CompilerParams(dimension_semantics=sem, vmem_limit_bytes=vmem_mib * MIB)


def _rms(x, g):
    ms = jnp.mean(x * x, axis=-1, keepdims=True)
    return x * lax.rsqrt(ms + EPS) * g


def _headnorm(acc, g, scale=None):
    outs = []
    for c in range(acc.shape[1] // HEAD_DIM):
        y = _rms(acc[:, c * HEAD_DIM:(c + 1) * HEAD_DIM], g)
        outs.append(y if scale is None else y * scale)
    return jnp.concatenate(outs, axis=1)


def _norm_kernel(x_ref, g_ref, o_ref):
    o_ref[...] = _rms(x_ref[...], g_ref[...]).astype(o_ref.dtype)


def _rmsnorm_cast(x2d, g, tm):
    t, d = x2d.shape
    return pl.pallas_call(
        _norm_kernel,
        out_shape=jax.ShapeDtypeStruct((t, d), BF16),
        grid=(t // tm,),
        in_specs=[pl.BlockSpec((tm, d), lambda i: (i, 0)),
                  pl.BlockSpec((1, d), lambda i: (0, 0))],
        out_specs=pl.BlockSpec((tm, d), lambda i: (i, 0)),
        compiler_params=_cparams(("parallel",), 32),
        name="rmsnorm_cast",
    )(x2d, g.reshape(1, d))


def _inproj_kernel(h_ref, wq, wk, wv, wc, wb, wu, wga, wgb, qg_ref, kg_ref, bga_ref, bgb_ref,
                   q_o, kf_o, kb_o, vf_o, vb_o, z_o, b_o, ga_o, gb_o, *, q_scale):
    h = h_ref[...]

    def mm(w):
        return jnp.dot(h, w[...], preferred_element_type=F32)

    q_o[...] = _headnorm(mm(wq), qg_ref[...], q_scale).astype(q_o.dtype)
    k = _headnorm(mm(wk), kg_ref[...])
    kf_o[...] = k
    kb_o[...] = k.astype(BF16)
    v = mm(wv)
    vf_o[...] = v
    vb_o[...] = v.astype(BF16)
    z_o[...] = mm(wc) * mm(wu)
    b_o[...] = mm(wb).astype(b_o.dtype)
    ga_o[...] = jax.nn.sigmoid(mm(wga) + bga_ref[...]).astype(ga_o.dtype)
    gb_o[...] = jax.nn.sigmoid(mm(wgb) + bgb_ref[...]).astype(gb_o.dtype)


def _inproj(hb, w_in_bf, q_g, k_g, b_gate, q_dtype, aux_dtype, tm, tn):
    t = hb.shape[0]
    nb = D_MODEL // tn
    region = {"q": 0, "k": 1, "v": 2, "c": 3, "b": 4, "u": 5, "ga": 6, "gb": 7}

    def wspec(r):
        return pl.BlockSpec((D_MODEL, tn), lambda i, j, r=r: (0, r * nb + j))

    tile = pl.BlockSpec((tm, tn), lambda i, j: (i, j))
    vec = pl.BlockSpec((1, HEAD_DIM), lambda i, j: (0, 0))
    bias = pl.BlockSpec((1, tn), lambda i, j: (0, j))
    f32o = jax.ShapeDtypeStruct((t, D_MODEL), F32)
    bfo = jax.ShapeDtypeStruct((t, D_MODEL), BF16)
    return pl.pallas_call(
        functools.partial(_inproj_kernel, q_scale=HEAD_DIM ** -0.5 * LOG2E),
        out_shape=(jax.ShapeDtypeStruct((t, D_MODEL), q_dtype), f32o, bfo, f32o, bfo, f32o)
        + (jax.ShapeDtypeStruct((t, D_MODEL), aux_dtype),) * 3,
        grid=(t // tm, nb),
        in_specs=[pl.BlockSpec((tm, D_MODEL), lambda i, j: (i, 0))]
        + [wspec(region[n]) for n in ("q", "k", "v", "c", "b", "u", "ga", "gb")]
        + [vec, vec, bias, bias],
        out_specs=(tile,) * 9,
        compiler_params=_cparams(("parallel", "arbitrary"), 48),
        name="inproj",
    )(hb, *([w_in_bf] * 8), q_g.reshape(1, HEAD_DIM), k_g.reshape(1, HEAD_DIM),
      b_gate[0:1], b_gate[1:2])


def _lane_tiles(x):
    return [x[:, c * LANES:(c + 1) * LANES] for c in range(x.shape[1] // LANES)]


def _lane_repeat(x, n):
    return x if n == 1 else jnp.concatenate([x] * n, axis=1)


def _online_update(s_list, v_list, m_prev, l_prev, acc_prev, c_list=None):
    if c_list is None:
        c_list = [None] * len(s_list)
    tops = []
    for s, c in zip(s_list, c_list):
        top = functools.reduce(jnp.maximum, _lane_tiles(s))
        tops.append(top if c is None else top + c)
    m_cur = jnp.max(functools.reduce(jnp.maximum, tops), axis=1, keepdims=True)
    m_next = jnp.maximum(m_prev, m_cur)
    alpha = jnp.exp2(m_prev - m_next)
    l_next = alpha * l_prev
    pv = None
    for s, v, c in zip(s_list, v_list, c_list):
        shift = m_next if c is None else m_next - c
        p = jnp.exp2(s - _lane_repeat(shift, s.shape[1] // LANES))
        l_next = l_next + functools.reduce(jnp.add, _lane_tiles(p))
        d = v(p.astype(BF16)) if callable(v) else jnp.dot(p.astype(BF16), v, preferred_element_type=F32)
        pv = d if pv is None else pv + d
    acc_next = acc_prev * _lane_repeat(alpha, V_DIM // LANES) + pv
    return m_next, l_next, acc_next


def _diff_out(a1, l1, a2, l2, lam, g, out_scale):
    r1 = 1.0 / jnp.sum(l1, axis=1, keepdims=True)
    r2 = 1.0 / jnp.sum(l2, axis=1, keepdims=True)
    o = a1 * r1 - lam * (a2 * r2)
    return _rms(o, g) * out_scale


def _init_softmax_state(m_scr, l_scr, acc_scr):
    m_scr[...] = jnp.full(m_scr.shape, NEG, F32)
    l_scr[...] = jnp.zeros(l_scr.shape, F32)
    acc_scr[...] = jnp.zeros(acc_scr.shape, F32)


def _pattn_kernel(qi_ref, ki_ref, lam_ref, slopes_ref, q_ref, k_ref, v_ref, g_ref, o_ref, m_scr, l_scr, acc_scr,
                  *, t, hp, out_scale):
    h = pl.program_id(1)
    step = pl.program_id(2)
    qi = qi_ref[step]
    ki = ki_ref[step]

    @pl.when(ki == 0)
    def _():
        _init_softmax_state(m_scr, l_scr, acc_scr)

    def body(masked):
        kpos = ((ki - qi) * t + lax.broadcasted_iota(jnp.int32, (1, t), 1)).astype(F32)
        if masked:
            keep = (lax.broadcasted_iota(jnp.int32, (t, t), 1)
                    <= lax.broadcasted_iota(jnp.int32, (t, t), 0))
        state = []
        for i in range(hp):
            bias = slopes_ref[h * hp + i] * kpos
            v = v_ref[0, :, i * V_DIM:(i + 1) * V_DIM]
            for m in range(2):
                cols = slice(i * HEAD_W + m * HEAD_DIM, i * HEAD_W + (m + 1) * HEAD_DIM)
                s = lax.dot_general(q_ref[0, :, cols], k_ref[0, :, cols], (((1,), (1,)), ((), ())),
                                    preferred_element_type=F32)
                s = s + bias
                if masked:
                    s = jnp.where(keep, s, NEG)
                idx = 2 * i + m
                state.append(_online_update([s], [v], m_scr[idx], l_scr[idx], acc_scr[idx]))
        return state

    @pl.when(ki < qi)
    def _():
        for idx, (m_next, l_next, acc_next) in enumerate(body(False)):
            m_scr[idx] = m_next
            l_scr[idx] = l_next
            acc_scr[idx] = acc_next

    @pl.when(ki == qi)
    def _():
        state = body(True)
        lam = lam_ref[0]
        g = g_ref[...]
        o_ref[0] = jnp.concatenate(
            [_diff_out(state[2 * i][2], state[2 * i][1], state[2 * i + 1][2], state[2 * i + 1][1], lam, g, out_scale)
             for i in range(hp)], axis=1).astype(o_ref.dtype)


def _prompt_attn(qb, kb, vb, lam, slopes2, subln_g, out_scale, t, hp):
    b, s, _ = qb.shape
    n = s // t
    pairs = [(qi, ki) for qi in range(n) for ki in range(qi + 1)]
    qi_tab = jnp.asarray([p[0] for p in pairs], jnp.int32)
    ki_tab = jnp.asarray([p[1] for p in pairs], jnp.int32)
    smem = pl.BlockSpec(memory_space=pltpu.SMEM)
    grid_spec = pltpu.PrefetchScalarGridSpec(
        num_scalar_prefetch=2,
        grid=(b, N_HEADS // hp, len(pairs)),
        in_specs=[smem, smem,
                  pl.BlockSpec((1, t, hp * HEAD_W), lambda bi, h, st, qt, kt: (bi, qt[st], h)),
                  pl.BlockSpec((1, t, hp * HEAD_W), lambda bi, h, st, qt, kt: (bi, kt[st], h)),
                  pl.BlockSpec((1, t, hp * V_DIM), lambda bi, h, st, qt, kt: (bi, kt[st], h)),
                  pl.BlockSpec((1, V_DIM), lambda bi, h, st, qt, kt: (0, 0))],
        out_specs=pl.BlockSpec((1, t, hp * V_DIM), lambda bi, h, st, qt, kt: (bi, qt[st], h)),
        scratch_shapes=[pltpu.VMEM((2 * hp, t, LANES), F32), pltpu.VMEM((2 * hp, t, LANES), F32),
                        pltpu.VMEM((2 * hp, t, V_DIM), F32)],
    )
    return pl.pallas_call(
        functools.partial(_pattn_kernel, t=t, hp=hp, out_scale=out_scale),
        out_shape=jax.ShapeDtypeStruct((b, s, D_MODEL), BF16),
        grid_spec=grid_spec,
        compiler_params=_cparams(("parallel", "parallel", "arbitrary"), 40),
        name="prompt_attn",
    )(qi_tab, ki_tab, lam, slopes2, qb, kb, vb, subln_g.reshape(1, V_DIM))


def _sattn_bias(slopes2, tq):
    rows = jnp.arange(2 * N_HEADS * tq)
    row_h = (rows // tq) % N_HEADS
    slope = slopes2[row_h][:, None]
    lanes = jnp.arange(PAGE * N_HEADS)
    page = jnp.where(row_h[:, None] == (lanes % N_HEADS)[None, :], slope * (lanes // N_HEADS)[None, :].astype(F32), NEG)
    tok = jnp.arange(LANES)
    new = jnp.where(tok[None, :] <= (rows % tq)[:, None], slope * tok[None, :].astype(F32), NEG)
    return page.astype(F32), new.astype(F32)


def _sattn_kernel(pt_ref, lam_ref, q_ref, kn_ref, vn_ref, pk_hbm, pv_hbm, bias_ref, bias_new_ref, slope_ref, g_ref,
                  o_ref, k_buf, v_buf, qbd_scr, qfull_scr, m_scr, l_scr, acc_scr, sem,
                  *, npg, n_steps, n_slots, past, out_scale):
    p = pl.program_id(1)
    tq = q_ref.shape[1]
    half = N_HEADS * tq

    g = pl.program_id(0) * n_steps + p
    total = pl.num_programs(0) * n_steps

    def page_copies(step):
        slot = step % n_slots
        cps = []
        for j in range(npg):
            page = pt_ref[step // n_steps, (step % n_steps) * npg + j]
            cps.append(pltpu.make_async_copy(pk_hbm.at[page], k_buf.at[slot, j], sem.at[0, slot]))
            cps.append(pltpu.make_async_copy(pv_hbm.at[page], v_buf.at[slot, j], sem.at[1, slot]))
        return cps

    @pl.when(g == 0)
    def _():
        for first in range(n_slots - 1):
            for cp in page_copies(jnp.int32(first)):
                cp.start()

    @pl.when(g + (n_slots - 1) < total)
    def _():
        for cp in page_copies(g + (n_slots - 1)):
            cp.start()

    for cp in page_copies(g):
        cp.wait()
    slot = g % n_slots

    @pl.when(p == 0)
    def _():
        _init_softmax_state(m_scr, l_scr, acc_scr)
        q = q_ref[0]
        zero = jnp.zeros((half, HEAD_DIM), F32)
        q1, q2 = [jnp.concatenate([q[:, h * HEAD_W + m * HEAD_DIM:h * HEAD_W + (m + 1) * HEAD_DIM]
                                   for h in range(N_HEADS)], axis=0) for m in range(2)]
        qbd_scr[...] = jnp.concatenate([jnp.concatenate([q1, zero], axis=1),
                                        jnp.concatenate([zero, q2], axis=1)], axis=0).astype(BF16)
        q_rep = jnp.concatenate([q] * N_HEADS, axis=0)
        lane = lax.broadcasted_iota(jnp.int32, q_rep.shape, 1)
        own_head = lane // HEAD_W == lax.broadcasted_iota(jnp.int32, q_rep.shape, 0) // tq
        qfull_scr[...] = jnp.concatenate(
            [jnp.where(own_head & ((lane // HEAD_DIM) % 2 == m), q_rep, 0.0) for m in range(2)],
            axis=0).astype(BF16)

    nt = (((1,), (1,)), ((), ()))

    def body(last):
        qbd = qbd_scr[...]
        bias = bias_ref[...]
        slope = slope_ref[...]
        s_list, v_list, c_list = [], [], []
        for j in range(npg):
            s = lax.dot_general(qbd, k_buf[slot, j].astype(BF16), nt, preferred_element_type=F32)
            s_list.append(s + bias)
            v_list.append(v_buf[slot, j].astype(BF16))
            c_list.append(slope * ((p * npg + j) * PAGE - past).astype(F32))
        if last:
            pad = jnp.zeros((LANES - tq, D_MODEL), F32)
            kn = jnp.concatenate([kn_ref[0], pad], axis=0).astype(BF16)
            vn = jnp.concatenate([vn_ref[0], pad], axis=0).astype(BF16)
            s = lax.dot_general(qfull_scr[...], kn, nt, preferred_element_type=F32)
            s_list.append(s + bias_new_ref[...])

            def new_values(pb):
                full = jnp.dot(pb, vn, preferred_element_type=F32)
                return jnp.concatenate([full[(m * N_HEADS + h) * tq:(m * N_HEADS + h + 1) * tq,
                                             h * V_DIM:(h + 1) * V_DIM]
                                        for m in range(2) for h in range(N_HEADS)], axis=0)

            v_list.append(new_values)
            c_list.append(None)
        return _online_update(s_list, v_list, m_scr[...], l_scr[...], acc_scr[...], c_list)

    @pl.when(p < n_steps - 1)
    def _():
        m_next, l_next, acc_next = body(False)
        m_scr[...] = m_next
        l_scr[...] = l_next
        acc_scr[...] = acc_next

    @pl.when(p == n_steps - 1)
    def _():
        _, l, acc = body(True)
        y = _diff_out(acc[:half], l[:half], acc[half:], l[half:], lam_ref[0], g_ref[...], out_scale)
        o_ref[0] = jnp.concatenate([y[h * tq:(h + 1) * tq] for h in range(N_HEADS)], axis=1)


def _sample_attn(qs, kn, vn, pool_k, pool_v, page_table, lam, slopes2, subln_g, out_scale, npg, n_slots):
    db, tq, _ = qs.shape
    n_pages = page_table.shape[1]
    n_steps = n_pages // npg
    n_phys = pool_k.shape[0]
    rows = 2 * N_HEADS * tq
    page_rows = PAGE * N_HEADS
    pk = pool_k.reshape(n_phys, page_rows, HEAD_W)
    pv = pool_v.reshape(n_phys, page_rows, V_DIM)
    bias, bias_new = _sattn_bias(slopes2, tq)
    slope_rep = jnp.broadcast_to(slopes2[(jnp.arange(rows) // tq) % N_HEADS][:, None], (rows, LANES))
    smem = pl.BlockSpec(memory_space=pltpu.SMEM)
    tok = pl.BlockSpec((1, tq, D_MODEL), lambda b, p, pt: (b, 0, 0))
    const = lambda shp: pl.BlockSpec(shp, lambda b, p, pt: (0, 0))

    hbm = pl.BlockSpec(memory_space=pl.ANY)
    grid_spec = pltpu.PrefetchScalarGridSpec(
        num_scalar_prefetch=1,
        grid=(db, n_steps),
        in_specs=[smem, tok, tok, tok, hbm, hbm,
                  const((rows, page_rows)), const((rows, LANES)), const((rows, LANES)), const((1, V_DIM))],
        out_specs=tok,
        scratch_shapes=[pltpu.VMEM((n_slots, npg, page_rows, HEAD_W), F32),
                        pltpu.VMEM((n_slots, npg, page_rows, V_DIM), F32),
                        pltpu.VMEM((rows, HEAD_W), BF16), pltpu.VMEM((rows, D_MODEL), BF16),
                        pltpu.VMEM((rows, LANES), F32), pltpu.VMEM((rows, LANES), F32),
                        pltpu.VMEM((rows, V_DIM), F32), pltpu.SemaphoreType.DMA((2, n_slots))],
    )
    return pl.pallas_call(
        functools.partial(_sattn_kernel, npg=npg, n_steps=n_steps, n_slots=n_slots, past=n_pages * PAGE,
                          out_scale=out_scale),
        out_shape=jax.ShapeDtypeStruct((db, tq, D_MODEL), F32),
        grid_spec=grid_spec,
        compiler_params=_cparams(("arbitrary", "arbitrary"), 48),
        name="sample_attn",
    )(page_table, lam, qs, kn, vn, pk, pv, bias, bias_new, slope_rep, subln_g.reshape(1, V_DIM))


def _conv_merge(o, ga, gb, b, z, z1, z2, cw):
    y = z2 * cw[0] + z1 * cw[1] + z * cw[2]
    return (ga.astype(F32) * o.astype(F32) + gb.astype(F32) * (b.astype(F32) * y)).astype(BF16)


def _mix_outproj_kernel(o_ref, ga_ref, gb_ref, b_ref, z_ref, zp_ref, cw_ref, w_ref, r_ref, out_ref, merged_scr,
                        *, tiles_per_seq):
    i = pl.program_id(0)

    @pl.when(pl.program_id(1) == 0)
    def _():
        z = z_ref[...]
        tm = z.shape[0]
        zp = jnp.where(i % tiles_per_seq == 0, 0.0, zp_ref[...])
        pm1 = zp[7:8, :]
        pm2 = zp[6:7, :]
        row = lax.broadcasted_iota(jnp.int32, (tm, 1), 0)
        z1 = jnp.where(row == 0, pm1, pltpu.roll(z, 1, 0))
        z2 = jnp.where(row == 0, pm2, jnp.where(row == 1, pm1, pltpu.roll(z, 2, 0)))
        cw = cw_ref[...]
        merged_scr[...] = _conv_merge(o_ref[...], ga_ref[...], gb_ref[...], b_ref[...], z, z1, z2,
                                      (cw[0:1], cw[1:2], cw[2:3]))

    out_ref[...] = r_ref[...] + jnp.dot(merged_scr[...], w_ref[...], preferred_element_type=F32)


def _mix_outproj_prompt(o, ga, gb, b, z, conv_w, w, r, seq, tm, tn):
    t = z.shape[0]
    n = w.shape[1]
    row = pl.BlockSpec((tm, D_MODEL), lambda i, j: (i, 0))
    prev = pl.BlockSpec((8, D_MODEL), lambda i, j: (jnp.maximum(i * (tm // 8) - 1, 0), 0))
    tile = pl.BlockSpec((tm, tn), lambda i, j: (i, j))
    return pl.pallas_call(
        functools.partial(_mix_outproj_kernel, tiles_per_seq=seq // tm),
        out_shape=jax.ShapeDtypeStruct((t, n), F32),
        grid=(t // tm, n // tn),
        in_specs=[row, row, row, row, row, prev, pl.BlockSpec((3, D_MODEL), lambda i, j: (0, 0)),
                  pl.BlockSpec((D_MODEL, tn), lambda i, j: (0, j)), tile],
        out_specs=tile,
        scratch_shapes=[pltpu.VMEM((tm, D_MODEL), BF16)],
        compiler_params=_cparams(("parallel", "arbitrary"), 48),
        name="mix_outproj",
    )(o, ga, gb, b, z, z, conv_w, w, r)


def _mix_sample_kernel(o_ref, ga_ref, gb_ref, b_ref, z_ref, cc_ref, cw_ref, out_ref):
    z = z_ref[...]
    cc = cc_ref[...]
    c0 = cc[:, 0:1, :]
    c1 = cc[:, 1:2, :]
    row = lax.broadcasted_iota(jnp.int32, (1, z.shape[1], 1), 1)
    z1 = jnp.where(row == 0, c1, pltpu.roll(z, 1, 1))
    z2 = jnp.where(row == 0, c0, jnp.where(row == 1, c1, pltpu.roll(z, 2, 1)))
    cw = cw_ref[...]
    out_ref[...] = _conv_merge(o_ref[...], ga_ref[...], gb_ref[...], b_ref[...], z, z1, z2,
                               (cw[0:1][None], cw[1:2][None], cw[2:3][None]))


def _mix_sample(o, ga, gb, b, z, cache_conv, conv_w, g):
    db, ts, _ = z.shape
    tile = pl.BlockSpec((g, ts, D_MODEL), lambda i: (i, 0, 0))
    return pl.pallas_call(
        _mix_sample_kernel,
        out_shape=jax.ShapeDtypeStruct((db, ts, D_MODEL), BF16),
        grid=(db // g,),
        in_specs=[tile, tile, tile, tile, tile,
                  pl.BlockSpec((g, cache_conv.shape[1], D_MODEL), lambda i: (i, 0, 0)),
                  pl.BlockSpec((3, D_MODEL), lambda i: (0, 0))],
        out_specs=tile,
        compiler_params=_cparams(("parallel",), 40),
        name="mix_sample",
    )(o, ga, gb, b, z, cache_conv, conv_w)


def _mm_res_kernel(a_ref, w_ref, r_ref, o_ref):
    o_ref[...] = r_ref[...] + jnp.dot(a_ref[...], w_ref[...], preferred_element_type=F32)


def _mm_res(a, w, r, tm, tn):
    t, k = a.shape
    n = w.shape[1]
    return pl.pallas_call(
        _mm_res_kernel,
        out_shape=jax.ShapeDtypeStruct((t, n), F32),
        grid=(t // tm, n // tn),
        in_specs=[pl.BlockSpec((tm, k), lambda i, j: (i, 0)),
                  pl.BlockSpec((k, tn), lambda i, j: (0, j)),
                  pl.BlockSpec((tm, tn), lambda i, j: (i, j))],
        out_specs=pl.BlockSpec((tm, tn), lambda i, j: (i, j)),
        compiler_params=_cparams(("parallel", "parallel"), 48),
        name="outproj",
    )(a, w, r)


def _memkv_kernel(h_ref, wk_ref, wv_ref, g_ref, kf_o, kb_o, vf_o, vb_o):
    h = h_ref[...]
    k = _headnorm(jnp.dot(h, wk_ref[...], preferred_element_type=F32), g_ref[...])
    kf_o[...] = k
    kb_o[...] = k.astype(BF16)
    v = jnp.dot(h, wv_ref[...], preferred_element_type=F32)
    vf_o[...] = v
    vb_o[...] = v.astype(BF16)


def _mem_kv(mem_nb, w_xkv_bf, xk_g):
    t = mem_nb.shape[0]
    f32o = jax.ShapeDtypeStruct((t, X_WIDTH), F32)
    bfo = jax.ShapeDtypeStruct((t, X_WIDTH), BF16)
    full = lambda s: pl.BlockSpec(s, lambda i: (0,) * len(s))
    return pl.pallas_call(
        _memkv_kernel,
        out_shape=(f32o, bfo, f32o, bfo),
        grid=(1,),
        in_specs=[full((t, D_MODEL)),
                  pl.BlockSpec((D_MODEL, X_WIDTH), lambda i: (0, 0)),
                  pl.BlockSpec((D_MODEL, X_WIDTH), lambda i: (0, 1)),
                  full((1, X_HEAD_DIM))],
        out_specs=(full((t, X_WIDTH)),) * 4,
        compiler_params=_cparams(("arbitrary",), 32),
        name="mem_kv",
    )(mem_nb, w_xkv_bf, w_xkv_bf, xk_g.reshape(1, X_HEAD_DIM))


def _xattn_head(qh, kh, vh):
    s = lax.dot_general(qh, kh, (((1,), (1,)), ((), ())), preferred_element_type=F32)
    e = jnp.exp(s - jnp.max(s, axis=1, keepdims=True))
    p = e * (1.0 / jnp.sum(e, axis=1, keepdims=True))
    return jnp.dot(p.astype(BF16), vh, preferred_element_type=F32)


def _xattn_q(x, g_ref, wq_ref, qg_ref):
    hn = _rms(x, g_ref[...]).astype(BF16)
    q = jnp.dot(hn, wq_ref[...], preferred_element_type=F32)
    return _headnorm(q, qg_ref[...], X_HEAD_DIM ** -0.5)


def _xattn_prompt_kernel(x_ref, g_ref, wq_ref, qg_ref, mk_ref, mv_ref, wo_ref, o_ref):
    x = x_ref[0]
    q = _xattn_q(x, g_ref, wq_ref, qg_ref).astype(BF16)
    mk = mk_ref[0]
    mv = mv_ref[0]
    heads = []
    for g in range(X_HEADS):
        sl = slice(g * X_HEAD_DIM, (g + 1) * X_HEAD_DIM)
        heads.append(_xattn_head(q[:, sl], mk[:, sl], mv[:, sl]))
    o = jnp.concatenate(heads, axis=1).astype(BF16)
    o_ref[0] = x + jnp.dot(o, wo_ref[...], preferred_element_type=F32)


def _xattn_prompt(x, g, w_xq_bf, xq_g, mk_b, mv_b, w_xo_bf, tm):
    b, s, _ = x.shape
    tile = pl.BlockSpec((1, tm, D_MODEL), lambda bi, i: (bi, i, 0))
    mem = pl.BlockSpec((1, N_MEM, X_WIDTH), lambda bi, i: (bi, 0, 0))
    const = lambda shp: pl.BlockSpec(shp, lambda bi, i: (0, 0))
    return pl.pallas_call(
        _xattn_prompt_kernel,
        out_shape=jax.ShapeDtypeStruct((b, s, D_MODEL), F32),
        grid=(b, s // tm),
        in_specs=[tile, const((1, D_MODEL)), const((D_MODEL, X_WIDTH)), const((1, X_HEAD_DIM)),
                  mem, mem, const((X_WIDTH, D_MODEL))],
        out_specs=tile,
        compiler_params=_cparams(("parallel", "parallel"), 48),
        name="xattn_prompt",
    )(x, g.reshape(1, D_MODEL), w_xq_bf, xq_g.reshape(1, X_HEAD_DIM), mk_b, mv_b, w_xo_bf)


def _xattn_sample_kernel(x_ref, g_ref, wq_ref, qg_ref, mk_ref, mv_ref, mask_ref, wo_ref, o_ref, *, ts):
    x = x_ref[...]
    q = _xattn_q(x, g_ref, wq_ref, qg_ref)
    mask = mask_ref[...]
    nt = (((1,), (1,)), ((), ()))
    per_batch = []
    for bi in range(mk_ref.shape[0]):
        rows = slice(bi * ts, (bi + 1) * ts)
        qs = jnp.concatenate([q[rows, g * X_HEAD_DIM:(g + 1) * X_HEAD_DIM] for g in range(X_HEADS)], axis=0)
        s = lax.dot_general(qs.astype(BF16), mk_ref[bi].astype(BF16), nt, preferred_element_type=F32) + mask
        e = jnp.exp(s - jnp.max(s, axis=1, keepdims=True))
        o = jnp.dot(e.astype(BF16), mv_ref[bi].astype(BF16), preferred_element_type=F32)
        o = o * (1.0 / jnp.sum(e, axis=1, keepdims=True))
        per_batch.append(jnp.concatenate([o[g * ts:(g + 1) * ts] for g in range(X_HEADS)], axis=1))
    a = jnp.concatenate(per_batch, axis=0).astype(BF16)
    o_ref[...] = x + jnp.dot(a, wo_ref[...], preferred_element_type=F32)


def _xattn_sample(x2d, g, w_xq_bf, xq_g, cache_mk, cache_mv, w_xo_bf, ts, gb):
    t = x2d.shape[0]
    db = cache_mk.shape[0]
    mk = cache_mk.reshape(db, N_MEM * X_HEADS, X_HEAD_DIM)
    mv = cache_mv.reshape(db, N_MEM * X_HEADS, X_HEAD_DIM)
    tile = pl.BlockSpec((gb * ts, D_MODEL), lambda i: (i, 0))
    mem = pl.BlockSpec((gb, N_MEM * X_HEADS, X_HEAD_DIM), lambda i: (i, 0, 0))
    const = lambda shp: pl.BlockSpec(shp, lambda i: (0, 0))
    row_head = jnp.arange(X_HEADS * ts) // ts
    lane_head = jnp.arange(N_MEM * X_HEADS) % X_HEADS
    mask = jnp.where(row_head[:, None] == lane_head[None, :], 0.0, NEG).astype(F32)
    return pl.pallas_call(
        functools.partial(_xattn_sample_kernel, ts=ts),
        out_shape=jax.ShapeDtypeStruct((t, D_MODEL), F32),
        grid=(db // gb,),
        in_specs=[tile, const((1, D_MODEL)), const((D_MODEL, X_WIDTH)), const((1, X_HEAD_DIM)),
                  mem, mem, const((X_HEADS * ts, N_MEM * X_HEADS)), const((X_WIDTH, D_MODEL))],
        out_specs=tile,
        compiler_params=_cparams(("parallel",), 48),
        name="xattn_sample",
    )(x2d, g.reshape(1, D_MODEL), w_xq_bf, xq_g.reshape(1, X_HEAD_DIM), mk, mv, mask, w_xo_bf)


def _first_argmax(v, lane):
    vmax = jnp.max(v, axis=1, keepdims=True)
    idx = jnp.min(jnp.where(v == vmax, lane, LANES), axis=1, keepdims=True)
    return vmax, idx


def _router_kernel(xp_ref, xs_ref, g_ref, wr_ref, br_ref, h_o, eid_o, w_o, *, n_prompt_tiles):
    i = pl.program_id(0)

    def body(x_ref):
        h = _rms(x_ref[...], g_ref[...])
        h_o[...] = h
        logits = jnp.dot(h, wr_ref[...], preferred_element_type=F32,
                         precision=lax.Precision.HIGHEST) + br_ref[...]
        lane = lax.broadcasted_iota(jnp.int32, logits.shape, 1)
        ninf = -jnp.inf
        gl = jnp.where(lane < N_GROUPS, logits, ninf)
        gmax, gidx = _first_argmax(gl, lane)
        g_w = 1.0 / jnp.sum(jnp.exp(gl - gmax), axis=1, keepdims=True)
        lo = N_GROUPS + gidx * EXPERTS_PER_GROUP
        el = jnp.where((lane >= lo) & (lane < lo + EXPERTS_PER_GROUP), logits, ninf)
        v1, i1 = _first_argmax(el, lane)
        v2, i2 = _first_argmax(jnp.where(lane == i1, ninf, el), lane)
        e2 = jnp.exp(v2 - v1)
        w1 = g_w / (1.0 + e2)
        w2 = g_w * e2 / (1.0 + e2)
        eid_o[...] = jnp.where(lane == 0, i1 - N_GROUPS, jnp.where(lane == 1, i2 - N_GROUPS, 0))
        w_o[...] = jnp.where(lane == 0, w1, jnp.where(lane == 1, w2, 0.0))

    @pl.when(i < n_prompt_tiles)
    def _():
        body(xp_ref)

    @pl.when(i >= n_prompt_tiles)
    def _():
        body(xs_ref)


def _router(xp2d, xs2d, g, w_r, b_r, tm):
    tp, ts = xp2d.shape[0], xs2d.shape[0]
    npt, nst = tp // tm, ts // tm
    t = tp + ts
    tile = pl.BlockSpec((tm, D_MODEL), lambda i: (i, 0))
    const = lambda shp: pl.BlockSpec(shp, lambda i: (0, 0))
    return pl.pallas_call(
        functools.partial(_router_kernel, n_prompt_tiles=npt),
        out_shape=(jax.ShapeDtypeStruct((t, D_MODEL), F32),
                   jax.ShapeDtypeStruct((t, LANES), jnp.int32),
                   jax.ShapeDtypeStruct((t, LANES), F32)),
        grid=(npt + nst,),
        in_specs=[pl.BlockSpec((tm, D_MODEL), lambda i: (jnp.minimum(i, npt - 1), 0)),
                  pl.BlockSpec((tm, D_MODEL), lambda i: (jnp.maximum(i - npt, 0), 0)),
                  const((1, D_MODEL)), const((D_MODEL, LANES)), const((1, LANES))],
        out_specs=(tile, pl.BlockSpec((tm, LANES), lambda i: (i, 0)), pl.BlockSpec((tm, LANES), lambda i: (i, 0))),
        compiler_params=_cparams(("parallel",), 40),
        name="moe_router",
    )(xp2d, xs2d, g.reshape(1, D_MODEL), w_r, b_r)


def _dispatch_kernel(dest_ref, zflag_ref, h_ref, rows_hbm, zero_scr, sem, *, tmd, tm, n_tiles):
    i = pl.program_id(0)

    def zero_tile(t, start):
        @pl.when(zflag_ref[t] != 0)
        def _():
            cp = pltpu.make_async_copy(zero_scr, rows_hbm.at[pl.ds(pl.multiple_of(t * tm, tm), tm)], sem.at[2])
            if start:
                cp.start()
            else:
                cp.wait()

    @pl.when(i == 0)
    def _():
        zero_scr[...] = jnp.zeros(zero_scr.shape, zero_scr.dtype)
        lax.fori_loop(0, n_tiles, lambda t, c: (zero_tile(t, True), c)[1], 0)
        lax.fori_loop(0, n_tiles, lambda t, c: (zero_tile(t, False), c)[1], 0)

    base = 2 * i * tmd

    def row_copy(r, slot):
        return pltpu.make_async_copy(h_ref.at[pl.ds(r, 1)], rows_hbm.at[pl.ds(dest_ref[base + 2 * r + slot], 1)],
                                     sem.at[slot])

    def start(r, carry):
        row_copy(r, 0).start(priority=0)
        row_copy(r, 1).start(priority=1)
        return carry

    lax.fori_loop(0, tmd, start, 0)
    for slot in range(2):
        pltpu.make_async_copy(h_ref, rows_hbm.at[pl.ds(0, tmd)], sem.at[slot]).wait()


def _dispatch_rows(hf, dest_flat, zflag, n_tiles, tm, tmd):
    t, d = hf.shape
    grid_spec = pltpu.PrefetchScalarGridSpec(
        num_scalar_prefetch=2,
        grid=(t // tmd,),
        in_specs=[pl.BlockSpec((tmd, d), lambda i, dest, zf: (i, 0))],
        out_specs=pl.BlockSpec(memory_space=pl.ANY),
        scratch_shapes=[pltpu.VMEM((tm, d), hf.dtype), pltpu.SemaphoreType.DMA((3,))],
    )
    return pl.pallas_call(
        functools.partial(_dispatch_kernel, tmd=tmd, tm=tm, n_tiles=n_tiles),
        out_shape=jax.ShapeDtypeStruct((n_tiles * tm, d), hf.dtype),
        grid_spec=grid_spec,
        compiler_params=_cparams(("arbitrary",), 32),
        name="moe_dispatch",
    )(dest_flat, zflag, hf)


def _ffn_kernel(te_ref, ne_ref, na_ref, x_ref, wgu_hbm, wd_hbm, o_ref, wgu_land, wd_land, wgu_scr, wd_scr, sem):
    t = pl.program_id(0)

    def weight_copies(e):
        return (pltpu.make_async_copy(wgu_hbm.at[e], wgu_land, sem.at[0]),
                pltpu.make_async_copy(wd_hbm.at[e], wd_land, sem.at[1]))

    @pl.when(t < na_ref[0])
    def _():
        e = te_ref[t]

        @pl.when(t == 0)
        def _():
            for cp in weight_copies(e):
                cp.start()

        @pl.when((t == 0) | (e != te_ref[jnp.maximum(t - 1, 0)]))
        def _():
            for cp in weight_copies(e):
                cp.wait()
            wgu_scr[...] = wgu_land[...].astype(BF16)
            wd_scr[...] = wd_land[...].astype(BF16)

            @pl.when(ne_ref[t] >= 0)
            def _():
                for cp in weight_copies(ne_ref[t]):
                    cp.start()

        gu = jnp.dot(x_ref[...].astype(BF16), wgu_scr[...], preferred_element_type=F32)
        gate = gu[:, :D_FF]
        act = (gate * jax.nn.sigmoid(gate)) * gu[:, D_FF:]
        o_ref[...] = jnp.dot(act.astype(BF16), wd_scr[...], preferred_element_type=F32)

    @pl.when(t >= na_ref[0])
    def _():
        o_ref[...] = jnp.zeros(o_ref.shape, o_ref.dtype)


def _expert_ffn(xs, tile_e, next_e, n_active, w_gu, w_down, tm):
    n_tiles = xs.shape[0] // tm
    grid_spec = pltpu.PrefetchScalarGridSpec(
        num_scalar_prefetch=3,
        grid=(n_tiles,),
        in_specs=[pl.BlockSpec((tm, D_MODEL), lambda t, te, ne, na: (jnp.minimum(t, na[0] - 1), 0)),
                  pl.BlockSpec(memory_space=pl.ANY), pl.BlockSpec(memory_space=pl.ANY)],
        out_specs=pl.BlockSpec((tm, D_MODEL), lambda t, te, ne, na: (t, 0)),
        scratch_shapes=[pltpu.VMEM((D_MODEL, 2 * D_FF), F32), pltpu.VMEM((D_FF, D_MODEL), F32),
                        pltpu.VMEM((D_MODEL, 2 * D_FF), BF16), pltpu.VMEM((D_FF, D_MODEL), BF16),
                        pltpu.SemaphoreType.DMA((2,))],
    )
    return pl.pallas_call(
        _ffn_kernel,
        out_shape=jax.ShapeDtypeStruct((n_tiles * tm, D_MODEL), F32),
        grid_spec=grid_spec,
        compiler_params=_cparams(("arbitrary",), 48),
        name="moe_ffn",
    )(tile_e, next_e, n_active, xs, w_gu, w_down)


def _combine_kernel(dest_ref, x_ref, w_ref, ys_hbm, o_ref, buf_a, buf_b, sem, *, tm, base_tok):
    base = 2 * (base_tok + pl.program_id(0) * tm)
    bufs = (buf_a, buf_b)

    def start(r, carry):
        for slot in range(2):
            pltpu.make_async_copy(ys_hbm.at[pl.ds(dest_ref[base + 2 * r + slot], 1)], bufs[slot].at[pl.ds(r, 1)],
                                  sem.at[slot]).start(priority=slot)
        return carry

    lax.fori_loop(0, tm, start, 0)
    for slot in range(2):
        pltpu.make_async_copy(ys_hbm.at[pl.ds(0, tm)], bufs[slot], sem.at[slot]).wait()
    w = w_ref[...]
    o_ref[...] = x_ref[...] + (w[:, 0:1] * buf_a[...] + w[:, 1:2] * buf_b[...])


def _combine(x2d, wts, ys, dest_flat, base_tok, tm):
    t, d = x2d.shape
    base_blk = base_tok // tm
    grid_spec = pltpu.PrefetchScalarGridSpec(
        num_scalar_prefetch=1,
        grid=(t // tm,),
        in_specs=[pl.BlockSpec((tm, d), lambda i, dest: (i, 0)),
                  pl.BlockSpec((tm, LANES), lambda i, dest: (i + base_blk, 0)),
                  pl.BlockSpec(memory_space=pl.ANY)],
        out_specs=pl.BlockSpec((tm, d), lambda i, dest: (i, 0)),
        scratch_shapes=[pltpu.VMEM((tm, d), F32), pltpu.VMEM((tm, d), F32), pltpu.SemaphoreType.DMA((2,))],
    )
    return pl.pallas_call(
        functools.partial(_combine_kernel, tm=tm, base_tok=base_tok),
        out_shape=jax.ShapeDtypeStruct((t, d), F32),
        grid_spec=grid_spec,
        compiler_params=_cparams(("arbitrary",), 32),
        name="moe_combine",
    )(dest_flat, x2d, wts, ys)


def _route_plan(eid, tm, n_tiles):
    flat_e = eid.reshape(-1)
    onehot = (flat_e[:, None] == jnp.arange(N_EXPERTS, dtype=jnp.int32)[None, :]).astype(jnp.int32)
    csum = jnp.cumsum(onehot, axis=0)
    rank = jnp.sum(onehot * csum, axis=1) - 1
    counts = csum[-1]
    padded = ((counts + tm - 1) // tm) * tm
    ends = jnp.cumsum(padded)
    dest = (jnp.sum(onehot * (ends - padded)[None, :], axis=1) + rank).astype(jnp.int32)
    n_active = (ends[-1] // tm).astype(jnp.int32)
    tile = jnp.arange(n_tiles, dtype=jnp.int32)
    active = tile < n_active
    tile_e = jnp.sum((ends[None, :] <= (tile * tm)[:, None]).astype(jnp.int32), axis=1)
    last_e = jnp.sum(jnp.where(tile == n_active - 1, tile_e, 0))
    tile_e = jnp.where(active, tile_e, last_e).astype(jnp.int32)
    has_pad = jnp.any((ends[None, :] == ((tile + 1) * tm)[:, None]) & (padded != counts)[None, :], axis=1)
    zflag = (has_pad | ~active).astype(jnp.int32)
    later = active[None, :] & (tile_e[None, :] > tile_e[:, None])
    next_e = jnp.min(jnp.where(later, tile_e[None, :], N_EXPERTS), axis=1)
    next_e = jnp.where(next_e < N_EXPERTS, next_e, -1).astype(jnp.int32)
    return dest, tile_e, next_e, n_active.reshape(1), zflag


def kernel(x_prompt, x_sample, cache_attn_k, cache_attn_v, cache_conv, cache_mem_k, cache_mem_v, page_table, mem_prompt, norm_mix_g, w_in, b_gate, q_norm_g, k_norm_g, lambda_q1, lambda_k1, lambda_q2, lambda_k2, subln_g, conv_w, w_out, norm_mem_g, norm_x_g, w_xq, w_xkv, xq_norm_g, xk_norm_g, w_xo, norm_ffn_g, w_group, b_group, w_expert_router, b_expert_router, w_gate_up, w_down):
    depth = w_in.shape[0]
    bp, sp, _ = x_prompt.shape
    bs, ts, _ = x_sample.shape
    tp, tsn = bp * sp, bs * ts
    slopes = jnp.exp2(-8.0 * jnp.arange(1, N_HEADS + 1, dtype=F32) / N_HEADS) * LOG2E
    xp = x_prompt.reshape(tp, D_MODEL)
    xs = x_sample.reshape(tsn, D_MODEL)
    moe_tm = 256
    moe_tiles = (tp + tsn) * 2 // moe_tm + N_EXPERTS
    outs = [[] for _ in range(8)]
    for l in range(depth):
        lam_init = 0.8 - 0.6 * math.exp(-0.3 * l)
        lam = (jnp.exp(jnp.sum(lambda_q1[l] * lambda_k1[l])) - jnp.exp(jnp.sum(lambda_q2[l] * lambda_k2[l]))
               + lam_init).reshape(1).astype(F32)
        out_scale = 1.0 - lam_init
        w_in_bf = w_in[l].astype(BF16)
        w_out_bf = w_out[l].astype(BF16)
        w_xq_bf = w_xq[l].astype(BF16)
        w_xkv_bf = w_xkv[l].astype(BF16)
        w_xo_bf = w_xo[l].astype(BF16)

        hb = _rmsnorm_cast(xp, norm_mix_g[l], 512)
        q, kf, kb, vf, vb, z, b, ga, gb = _inproj(hb, w_in_bf, q_norm_g[l], k_norm_g[l], b_gate[l], BF16, BF16,
                                                  1024, 256)
        o = _prompt_attn(q.reshape(bp, sp, D_MODEL), kb.reshape(bp, sp, D_MODEL), vb.reshape(bp, sp, D_MODEL),
                         lam, slopes, subln_g[l], out_scale, 512, 4)
        xp = _mix_outproj_prompt(o.reshape(tp, D_MODEL), ga, gb, b, z, conv_w[l], w_out_bf, xp, sp, 256, D_MODEL)
        outs[0].append(kf.reshape(bp, sp, N_HEADS, HEAD_W))
        outs[1].append(vf.reshape(bp, sp, N_HEADS, V_DIM))
        outs[2].append(z.reshape(bp, sp, D_MODEL)[:, sp - 2:, :])

        hb = _rmsnorm_cast(xs, norm_mix_g[l], 512)
        q, kf, kb, vf, vb, z, b, ga, gb = _inproj(hb, w_in_bf, q_norm_g[l], k_norm_g[l], b_gate[l], F32, F32,
                                                  1024, 256)
        shp = (bs, ts, D_MODEL)
        o = _sample_attn(q.reshape(shp), kf.reshape(shp), vf.reshape(shp), cache_attn_k[l], cache_attn_v[l],
                         page_table, lam, slopes, subln_g[l], out_scale, 4, 3)
        z3 = z.reshape(shp)
        merged = _mix_sample(o, ga.reshape(shp), gb.reshape(shp), b.reshape(shp), z3, cache_conv[l], conv_w[l], 32)
        xs = _mm_res(merged.reshape(tsn, D_MODEL), w_out_bf, xs, 512, 1024)
        outs[5].append(kf.reshape(bs, ts, N_HEADS, HEAD_W))
        outs[6].append(vf.reshape(bs, ts, N_HEADS, V_DIM))
        outs[7].append(z3[:, ts - 2:, :])

        mem_nb = _rmsnorm_cast(mem_prompt.reshape(bp * N_MEM, D_MODEL), norm_mem_g[l], 512)
        mkf, mkb, mvf, mvb = _mem_kv(mem_nb, w_xkv_bf, xk_norm_g[l])
        xp = _xattn_prompt(xp.reshape(bp, sp, D_MODEL), norm_x_g[l], w_xq_bf, xq_norm_g[l],
                           mkb.reshape(bp, N_MEM, X_WIDTH), mvb.reshape(bp, N_MEM, X_WIDTH), w_xo_bf,
                           512).reshape(tp, D_MODEL)
        xs = _xattn_sample(xs, norm_x_g[l], w_xq_bf, xq_norm_g[l], cache_mem_k[l], cache_mem_v[l], w_xo_bf, ts, 8)
        outs[3].append(mkf.reshape(bp, N_MEM, X_HEADS, X_HEAD_DIM))
        outs[4].append(mvf.reshape(bp, N_MEM, X_HEADS, X_HEAD_DIM))

        w_r = jnp.zeros((D_MODEL, LANES), F32).at[:, :N_GROUPS].set(w_group[l])
        w_r = w_r.at[:, N_GROUPS:N_GROUPS + N_EXPERTS].set(w_expert_router[l])
        b_r = jnp.zeros((1, LANES), F32).at[0, :N_GROUPS].set(b_group[l])
        b_r = b_r.at[0, N_GROUPS:N_GROUPS + N_EXPERTS].set(b_expert_router[l])
        hf, eid, wts = _router(xp, xs, norm_ffn_g[l], w_r, b_r, 512)
        dest, tile_e, next_e, n_active, zflag = _route_plan(eid[:, :2], moe_tm, moe_tiles)
        rows = _dispatch_rows(hf, dest, zflag, moe_tiles, moe_tm, 256)
        ys = _expert_ffn(rows, tile_e, next_e, n_active, w_gate_up[l], w_down[l], moe_tm)
        xp = _combine(xp, wts, ys, dest, 0, moe_tm)
        xs = _combine(xs, wts, ys, dest, tp, moe_tm)
    stack = lambda i: jnp.stack(outs[i])
    return (xp.reshape(bp, sp, D_MODEL), xs.reshape(bs, ts, D_MODEL), stack(0), stack(1), stack(2), stack(3),
            stack(4), stack(5), stack(6), stack(7))
```

```python
import functools
import math

import jax
import jax.numpy as jnp
from jax import lax
from jax.experimental import pallas as pl
from jax.experimental.pallas import tpu as pltpu

F32 = jnp.float32
BF16 = jnp.bfloat16

D_MODEL = 2048
N_HEADS = 8
HEAD_DIM = 128
V_DIM = 256
HEAD_W = 2 * HEAD_DIM
PAGE = 128
X_HEADS = 4
X_HEAD_DIM = 128
X_WIDTH = X_HEADS * X_HEAD_DIM
N_MEM = 256
N_GROUPS = 4
EXPERTS_PER_GROUP = 8
N_EXPERTS = N_GROUPS * EXPERTS_PER_GROUP
D_FF = 512
EPS = 1e-6
NEG = -1e30
LANES = 128
MIB = 1024 * 1024
LOG2E = math.log2(math.e)


def _cparams(sem, vmem_mib):
    return pltpu.CompilerParams(dimension_semantics=sem, vmem_limit_bytes=vmem_mib * MIB)


def _rms(x, g):
    ms = jnp.mean(x * x, axis=-1, keepdims=True)
    return x * lax.rsqrt(ms + EPS) * g


def _headnorm(acc, g, scale=None):
    outs = []
    for c in range(acc.shape[1] // HEAD_DIM):
        y = _rms(acc[:, c * HEAD_DIM:(c + 1) * HEAD_DIM], g)
        outs.append(y if scale is None else y * scale)
    return jnp.concatenate(outs, axis=1)


def _norm_kernel(x_ref, g_ref, o_ref):
    o_ref[...] = _rms(x_ref[...], g_ref[...]).astype(o_ref.dtype)


def _rmsnorm_cast(x2d, g, tm):
    t, d = x2d.shape
    return pl.pallas_call(
        _norm_kernel,
        out_shape=jax.ShapeDtypeStruct((t, d), BF16),
        grid=(t // tm,),
        in_specs=[pl.BlockSpec((tm, d), lambda i: (i, 0)),
                  pl.BlockSpec((1, d), lambda i: (0, 0))],
        out_specs=pl.BlockSpec((tm, d), lambda i: (i, 0)),
        compiler_params=_cparams(("parallel",), 32),
        name="rmsnorm_cast",
    )(x2d, g.reshape(1, d))


def _inproj_kernel(h_ref, wq, wk, wv, wc, wb, wu, wga, wgb, qg_ref, kg_ref, bga_ref, bgb_ref,
                   q_o, kf_o, kb_o, vf_o, vb_o, z_o, b_o, ga_o, gb_o, *, q_scale):
    h = h_ref[...]

    def mm(w):
        return jnp.dot(h, w[...], preferred_element_type=F32)

    q_o[...] = _headnorm(mm(wq), qg_ref[...], q_scale).astype(q_o.dtype)
    k = _headnorm(mm(wk), kg_ref[...])
    kf_o[...] = k
    kb_o[...] = k.astype(BF16)
    v = mm(wv)
    vf_o[...] = v
    vb_o[...] = v.astype(BF16)
    z_o[...] = mm(wc) * mm(wu)
    b_o[...] = mm(wb).astype(b_o.dtype)
    ga_o[...] = jax.nn.sigmoid(mm(wga) + bga_ref[...]).astype(ga_o.dtype)
    gb_o[...] = jax.nn.sigmoid(mm(wgb) + bgb_ref[...]).astype(gb_o.dtype)


def _inproj(hb, w_in_bf, q_g, k_g, b_gate, q_dtype, aux_dtype, tm, tn):
    t = hb.shape[0]
    nb = D_MODEL // tn
    region = {"q": 0, "k": 1, "v": 2, "c": 3, "b": 4, "u": 5, "ga": 6, "gb": 7}

    def wspec(r):
        return pl.BlockSpec((D_MODEL, tn), lambda i, j, r=r: (0, r * nb + j))

    tile = pl.BlockSpec((tm, tn), lambda i, j: (i, j))
    vec = pl.BlockSpec((1, HEAD_DIM), lambda i, j: (0, 0))
    bias = pl.BlockSpec((1, tn), lambda i, j: (0, j))
    f32o = jax.ShapeDtypeStruct((t, D_MODEL), F32)
    bfo = jax.ShapeDtypeStruct((t, D_MODEL), BF16)
    return pl.pallas_call(
        functools.partial(_inproj_kernel, q_scale=HEAD_DIM ** -0.5 * LOG2E),
        out_shape=(jax.ShapeDtypeStruct((t, D_MODEL), q_dtype), f32o, bfo, f32o, bfo, f32o)
        + (jax.ShapeDtypeStruct((t, D_MODEL), aux_dtype),) * 3,
        grid=(t // tm, nb),
        in_specs=[pl.BlockSpec((tm, D_MODEL), lambda i, j: (i, 0))]
        + [wspec(region[n]) for n in ("q", "k", "v", "c", "b", "u", "ga", "gb")]
        + [vec, vec, bias, bias],
        out_specs=(tile,) * 9,
        compiler_params=_cparams(("parallel", "arbitrary"), 48),
        name="inproj",
    )(hb, *([w_in_bf] * 8), q_g.reshape(1, HEAD_DIM), k_g.reshape(1, HEAD_DIM),
      b_gate[0:1], b_gate[1:2])


def _lane_tiles(x):
    return [x[:, c * LANES:(c + 1) * LANES] for c in range(x.shape[1] // LANES)]


def _lane_repeat(x, n):
    return x if n == 1 else jnp.concatenate([x] * n, axis=1)


def _online_update(s_list, v_list, m_prev, l_prev, acc_prev, c_list=None):
    if c_list is None:
        c_list = [None] * len(s_list)
    tops = []
    for s, c in zip(s_list, c_list):
        top = functools.reduce(jnp.maximum, _lane_tiles(s))
        tops.append(top if c is None else top + c)
    m_cur = jnp.max(functools.reduce(jnp.maximum, tops), axis=1, keepdims=True)
    m_next = jnp.maximum(m_prev, m_cur)
    alpha = jnp.exp2(m_prev - m_next)
    l_next = alpha * l_prev
    pv = None
    for s, v, c in zip(s_list, v_list, c_list):
        shift = m_next if c is None else m_next - c
        p = jnp.exp2(s - _lane_repeat(shift, s.shape[1] // LANES))
        l_next = l_next + functools.reduce(jnp.add, _lane_tiles(p))
        d = v(p.astype(BF16)) if callable(v) else jnp.dot(p.astype(BF16), v, preferred_element_type=F32)
        pv = d if pv is None else pv + d
    acc_next = acc_prev * _lane_repeat(alpha, V_DIM // LANES) + pv
    return m_next, l_next, acc_next


def _diff_out(a1, l1, a2, l2, lam, g, out_scale):
    r1 = 1.0 / jnp.sum(l1, axis=1, keepdims=True)
    r2 = 1.0 / jnp.sum(l2, axis=1, keepdims=True)
    o = a1 * r1 - lam * (a2 * r2)
    return _rms(o, g) * out_scale


def _init_softmax_state(m_scr, l_scr, acc_scr):
    m_scr[...] = jnp.full(m_scr.shape, NEG, F32)
    l_scr[...] = jnp.zeros(l_scr.shape, F32)
    acc_scr[...] = jnp.zeros(acc_scr.shape, F32)


def _pattn_kernel(qi_ref, ki_ref, lam_ref, slopes_ref, q_ref, k_ref, v_ref, g_ref, o_ref, m_scr, l_scr, acc_scr,
                  *, t, hp, out_scale):
    h = pl.program_id(1)
    step = pl.program_id(2)
    qi = qi_ref[step]
    ki = ki_ref[step]

    @pl.when(ki == 0)
    def _():
        _init_softmax_state(m_scr, l_scr, acc_scr)

    def body(masked):
        kpos = ((ki - qi) * t + lax.broadcasted_iota(jnp.int32, (1, t), 1)).astype(F32)
        if masked:
            keep = (lax.broadcasted_iota(jnp.int32, (t, t), 1)
                    <= lax.broadcasted_iota(jnp.int32, (t, t), 0))
        state = []
        for i in range(hp):
            bias = slopes_ref[h * hp + i] * kpos
            v = v_ref[0, :, i * V_DIM:(i + 1) * V_DIM]
            for m in range(2):
                cols = slice(i * HEAD_W + m * HEAD_DIM, i * HEAD_W + (m + 1) * HEAD_DIM)
                s = lax.dot_general(q_ref[0, :, cols], k_ref[0, :, cols], (((1,), (1,)), ((), ())),
                                    preferred_element_type=F32)
                s = s + bias
                if masked:
                    s = jnp.where(keep, s, NEG)
                idx = 2 * i + m
                state.append(_online_update([s], [v], m_scr[idx], l_scr[idx], acc_scr[idx]))
        return state

    @pl.when(ki < qi)
    def _():
        for idx, (m_next, l_next, acc_next) in enumerate(body(False)):
            m_scr[idx] = m_next
            l_scr[idx] = l_next
            acc_scr[idx] = acc_next

    @pl.when(ki == qi)
    def _():
        state = body(True)
        lam = lam_ref[0]
        g = g_ref[...]
        o_ref[0] = jnp.concatenate(
            [_diff_out(state[2 * i][2], state[2 * i][1], state[2 * i + 1][2], state[2 * i + 1][1], lam, g, out_scale)
             for i in range(hp)], axis=1).astype(o_ref.dtype)


def _prompt_attn(qb, kb, vb, lam, slopes2, subln_g, out_scale, t, hp):
    b, s, _ = qb.shape
    n = s // t
    pairs = [(qi, ki) for qi in range(n) for ki in range(qi + 1)]
    qi_tab = jnp.asarray([p[0] for p in pairs], jnp.int32)
    ki_tab = jnp.asarray([p[1] for p in pairs], jnp.int32)
    smem = pl.BlockSpec(memory_space=pltpu.SMEM)
    grid_spec = pltpu.PrefetchScalarGridSpec(
        num_scalar_prefetch=2,
        grid=(b, N_HEADS // hp, len(pairs)),
        in_specs=[smem, smem,
                  pl.BlockSpec((1, t, hp * HEAD_W), lambda bi, h, st, qt, kt: (bi, qt[st], h)),
                  pl.BlockSpec((1, t, hp * HEAD_W), lambda bi, h, st, qt, kt: (bi, kt[st], h)),
                  pl.BlockSpec((1, t, hp * V_DIM), lambda bi, h, st, qt, kt: (bi, kt[st], h)),
                  pl.BlockSpec((1, V_DIM), lambda bi, h, st, qt, kt: (0, 0))],
        out_specs=pl.BlockSpec((1, t, hp * V_DIM), lambda bi, h, st, qt, kt: (bi, qt[st], h)),
        scratch_shapes=[pltpu.VMEM((2 * hp, t, LANES), F32), pltpu.VMEM((2 * hp, t, LANES), F32),
                        pltpu.VMEM((2 * hp, t, V_DIM), F32)],
    )
    return pl.pallas_call(
        functools.partial(_pattn_kernel, t=t, hp=hp, out_scale=out_scale),
        out_shape=jax.ShapeDtypeStruct((b, s, D_MODEL), BF16),
        grid_spec=grid_spec,
        compiler_params=_cparams(("parallel", "parallel", "arbitrary"), 40),
        name="prompt_attn",
    )(qi_tab, ki_tab, lam, slopes2, qb, kb, vb, subln_g.reshape(1, V_DIM))


def _sattn_bias(slopes2, tq):
    rows = jnp.arange(2 * N_HEADS * tq)
    row_h = (rows // tq) % N_HEADS
    slope = slopes2[row_h][:, None]
    lanes = jnp.arange(PAGE * N_HEADS)
    page = jnp.where(row_h[:, None] == (lanes % N_HEADS)[None, :], slope * (lanes // N_HEADS)[None, :].astype(F32), NEG)
    tok = jnp.arange(LANES)
    new = jnp.where(tok[None, :] <= (rows % tq)[:, None], slope * tok[None, :].astype(F32), NEG)
    return page.astype(F32), new.astype(F32)


def _sattn_kernel(pt_ref, lam_ref, q_ref, kn_ref, vn_ref, pk_hbm, pv_hbm, bias_ref, bias_new_ref, slope_ref, g_ref,
                  o_ref, k_buf, v_buf, qbd_scr, qfull_scr, m_scr, l_scr, acc_scr, sem,
                  *, npg, n_steps, n_slots, past, out_scale):
    p = pl.program_id(1)
    tq = q_ref.shape[1]
    half = N_HEADS * tq

    g = pl.program_id(0) * n_steps + p
    total = pl.num_programs(0) * n_steps

    def page_copies(step):
        slot = step % n_slots
        cps = []
        for j in range(npg):
            page = pt_ref[step // n_steps, (step % n_steps) * npg + j]
            cps.append(pltpu.make_async_copy(pk_hbm.at[page], k_buf.at[slot, j], sem.at[0, slot]))
            cps.append(pltpu.make_async_copy(pv_hbm.at[page], v_buf.at[slot, j], sem.at[1, slot]))
        return cps

    @pl.when(g == 0)
    def _():
        for first in range(n_slots - 1):
            for cp in page_copies(jnp.int32(first)):
                cp.start()

    @pl.when(g + (n_slots - 1) < total)
    def _():
        for cp in page_copies(g + (n_slots - 1)):
            cp.start()

    for cp in page_copies(g):
        cp.wait()
    slot = g % n_slots

    @pl.when(p == 0)
    def _():
        _init_softmax_state(m_scr, l_scr, acc_scr)
        q = q_ref[0]
        zero = jnp.zeros((half, HEAD_DIM), F32)
        q1, q2 = [jnp.concatenate([q[:, h * HEAD_W + m * HEAD_DIM:h * HEAD_W + (m + 1) * HEAD_DIM]
                                   for h in range(N_HEADS)], axis=0) for m in range(2)]
        qbd_scr[...] = jnp.concatenate([jnp.concatenate([q1, zero], axis=1),
                                        jnp.concatenate([zero, q2], axis=1)], axis=0).astype(BF16)
        q_rep = jnp.concatenate([q] * N_HEADS, axis=0)
        lane = lax.broadcasted_iota(jnp.int32, q_rep.shape, 1)
        own_head = lane // HEAD_W == lax.broadcasted_iota(jnp.int32, q_rep.shape, 0) // tq
        qfull_scr[...] = jnp.concatenate(
            [jnp.where(own_head & ((lane // HEAD_DIM) % 2 == m), q_rep, 0.0) for m in range(2)],
            axis=0).astype(BF16)

    nt = (((1,), (1,)), ((), ()))

    def body(last):
        qbd = qbd_scr[...]
        bias = bias_ref[...]
        slope = slope_ref[...]
        s_list, v_list, c_list = [], [], []
        for j in range(npg):
            s = lax.dot_general(qbd, k_buf[slot, j].astype(BF16), nt, preferred_element_type=F32)
            s_list.append(s + bias)
            v_list.append(v_buf[slot, j].astype(BF16))
            c_list.append(slope * ((p * npg + j) * PAGE - past).astype(F32))
        if last:
            pad = jnp.zeros((LANES - tq, D_MODEL), F32)
            kn = jnp.concatenate([kn_ref[0], pad], axis=0).astype(BF16)
            vn = jnp.concatenate([vn_ref[0], pad], axis=0).astype(BF16)
            s = lax.dot_general(qfull_scr[...], kn, nt, preferred_element_type=F32)
            s_list.append(s + bias_new_ref[...])

            def new_values(pb):
                full = jnp.dot(pb, vn, preferred_element_type=F32)
                return jnp.concatenate([full[(m * N_HEADS + h) * tq:(m * N_HEADS + h + 1) * tq,
                                             h * V_DIM:(h + 1) * V_DIM]
                                        for m in range(2) for h in range(N_HEADS)], axis=0)

            v_list.append(new_values)
            c_list.append(None)
        return _online_update(s_list, v_list, m_scr[...], l_scr[...], acc_scr[...], c_list)

    @pl.when(p < n_steps - 1)
    def _():
        m_next, l_next, acc_next = body(False)
        m_scr[...] = m_next
        l_scr[...] = l_next
        acc_scr[...] = acc_next

    @pl.when(p == n_steps - 1)
    def _():
        _, l, acc = body(True)
        y = _diff_out(acc[:half], l[:half], acc[half:], l[half:], lam_ref[0], g_ref[...], out_scale)
        o_ref[0] = jnp.concatenate([y[h * tq:(h + 1) * tq] for h in range(N_HEADS)], axis=1)


def _sample_attn(qs, kn, vn, pool_k, pool_v, page_table, lam, slopes2, subln_g, out_scale, npg, n_slots):
    db, tq, _ = qs.shape
    n_pages = page_table.shape[1]
    n_steps = n_pages // npg
    n_phys = pool_k.shape[0]
    rows = 2 * N_HEADS * tq
    page_rows = PAGE * N_HEADS
    pk = pool_k.reshape(n_phys, page_rows, HEAD_W)
    pv = pool_v.reshape(n_phys, page_rows, V_DIM)
    bias, bias_new = _sattn_bias(slopes2, tq)
    slope_rep = jnp.broadcast_to(slopes2[(jnp.arange(rows) // tq) % N_HEADS][:, None], (rows, LANES))
    smem = pl.BlockSpec(memory_space=pltpu.SMEM)
    tok = pl.BlockSpec((1, tq, D_MODEL), lambda b, p, pt: (b, 0, 0))
    const = lambda shp: pl.BlockSpec(shp, lambda b, p, pt: (0, 0))

    hbm = pl.BlockSpec(memory_space=pl.ANY)
    grid_spec = pltpu.PrefetchScalarGridSpec(
        num_scalar_prefetch=1,
        grid=(db, n_steps),
        in_specs=[smem, tok, tok, tok, hbm, hbm,
                  const((rows, page_rows)), const((rows, LANES)), const((rows, LANES)), const((1, V_DIM))],
        out_specs=tok,
        scratch_shapes=[pltpu.VMEM((n_slots, npg, page_rows, HEAD_W), F32),
                        pltpu.VMEM((n_slots, npg, page_rows, V_DIM), F32),
                        pltpu.VMEM((rows, HEAD_W), BF16), pltpu.VMEM((rows, D_MODEL), BF16),
                        pltpu.VMEM((rows, LANES), F32), pltpu.VMEM((rows, LANES), F32),
                        pltpu.VMEM((rows, V_DIM), F32), pltpu.SemaphoreType.DMA((2, n_slots))],
    )
    return pl.pallas_call(
        functools.partial(_sattn_kernel, npg=npg, n_steps=n_steps, n_slots=n_slots, past=n_pages * PAGE,
                          out_scale=out_scale),
        out_shape=jax.ShapeDtypeStruct((db, tq, D_MODEL), F32),
        grid_spec=grid_spec,
        compiler_params=_cparams(("arbitrary", "arbitrary"), 48),
        name="sample_attn",
    )(page_table, lam, qs, kn, vn, pk, pv, bias, bias_new, slope_rep, subln_g.reshape(1, V_DIM))


def _conv_merge(o, ga, gb, b, z, z1, z2, cw):
    y = z2 * cw[0] + z1 * cw[1] + z * cw[2]
    return (ga.astype(F32) * o.astype(F32) + gb.astype(F32) * (b.astype(F32) * y)).astype(BF16)


def _mix_outproj_kernel(o_ref, ga_ref, gb_ref, b_ref, z_ref, zp_ref, cw_ref, w_ref, r_ref, out_ref, merged_scr,
                        *, tiles_per_seq):
    i = pl.program_id(0)

    @pl.when(pl.program_id(1) == 0)
    def _():
        z = z_ref[...]
        tm = z.shape[0]
        zp = jnp.where(i % tiles_per_seq == 0, 0.0, zp_ref[...])
        pm1 = zp[7:8, :]
        pm2 = zp[6:7, :]
        row = lax.broadcasted_iota(jnp.int32, (tm, 1), 0)
        z1 = jnp.where(row == 0, pm1, pltpu.roll(z, 1, 0))
        z2 = jnp.where(row == 0, pm2, jnp.where(row == 1, pm1, pltpu.roll(z, 2, 0)))
        cw = cw_ref[...]
        merged_scr[...] = _conv_merge(o_ref[...], ga_ref[...], gb_ref[...], b_ref[...], z, z1, z2,
                                      (cw[0:1], cw[1:2], cw[2:3]))

    out_ref[...] = r_ref[...] + jnp.dot(merged_scr[...], w_ref[...], preferred_element_type=F32)


def _mix_outproj_prompt(o, ga, gb, b, z, conv_w, w, r, seq, tm, tn):
    t = z.shape[0]
    n = w.shape[1]
    row = pl.BlockSpec((tm, D_MODEL), lambda i, j: (i, 0))
    prev = pl.BlockSpec((8, D_MODEL), lambda i, j: (jnp.maximum(i * (tm // 8) - 1, 0), 0))
    tile = pl.BlockSpec((tm, tn), lambda i, j: (i, j))
    return pl.pallas_call(
        functools.partial(_mix_outproj_kernel, tiles_per_seq=seq // tm),
        out_shape=jax.ShapeDtypeStruct((t, n), F32),
        grid=(t // tm, n // tn),
        in_specs=[row, row, row, row, row, prev, pl.BlockSpec((3, D_MODEL), lambda i, j: (0, 0)),
                  pl.BlockSpec((D_MODEL, tn), lambda i, j: (0, j)), tile],
        out_specs=tile,
        scratch_shapes=[pltpu.VMEM((tm, D_MODEL), BF16)],
        compiler_params=_cparams(("parallel", "arbitrary"), 48),
        name="mix_outproj",
    )(o, ga, gb, b, z, z, conv_w, w, r)


def _mix_sample_kernel(o_ref, ga_ref, gb_ref, b_ref, z_ref, cc_ref, cw_ref, out_ref):
    z = z_ref[...]
    cc = cc_ref[...]
    c0 = cc[:, 0:1, :]
    c1 = cc[:, 1:2, :]
    row = lax.broadcasted_iota(jnp.int32, (1, z.shape[1], 1), 1)
    z1 = jnp.where(row == 0, c1, pltpu.roll(z, 1, 1))
    z2 = jnp.where(row == 0, c0, jnp.where(row == 1, c1, pltpu.roll(z, 2, 1)))
    cw = cw_ref[...]
    out_ref[...] = _conv_merge(o_ref[...], ga_ref[...], gb_ref[...], b_ref[...], z, z1, z2,
                               (cw[0:1][None], cw[1:2][None], cw[2:3][None]))


def _mix_sample(o, ga, gb, b, z, cache_conv, conv_w, g):
    db, ts, _ = z.shape
    tile = pl.BlockSpec((g, ts, D_MODEL), lambda i: (i, 0, 0))
    return pl.pallas_call(
        _mix_sample_kernel,
        out_shape=jax.ShapeDtypeStruct((db, ts, D_MODEL), BF16),
        grid=(db // g,),
        in_specs=[tile, tile, tile, tile, tile,
                  pl.BlockSpec((g, cache_conv.shape[1], D_MODEL), lambda i: (i, 0, 0)),
                  pl.BlockSpec((3, D_MODEL), lambda i: (0, 0))],
        out_specs=tile,
        compiler_params=_cparams(("parallel",), 40),
        name="mix_sample",
    )(o, ga, gb, b, z, cache_conv, conv_w)


def _mm_res_kernel(a_ref, w_ref, r_ref, o_ref):
    o_ref[...] = r_ref[...] + jnp.dot(a_ref[...], w_ref[...], preferred_element_type=F32)


def _mm_res(a, w, r, tm, tn):
    t, k = a.shape
    n = w.shape[1]
    return pl.pallas_call(
        _mm_res_kernel,
        out_shape=jax.ShapeDtypeStruct((t, n), F32),
        grid=(t // tm, n // tn),
        in_specs=[pl.BlockSpec((tm, k), lambda i, j: (i, 0)),
                  pl.BlockSpec((k, tn), lambda i, j: (0, j)),
                  pl.BlockSpec((tm, tn), lambda i, j: (i, j))],
        out_specs=pl.BlockSpec((tm, tn), lambda i, j: (i, j)),
        compiler_params=_cparams(("parallel", "parallel"), 48),
        name="outproj",
    )(a, w, r)


def _memkv_kernel(h_ref, wk_ref, wv_ref, g_ref, kf_o, kb_o, vf_o, vb_o):
    h = h_ref[...]
    k = _headnorm(jnp.dot(h, wk_ref[...], preferred_element_type=F32), g_ref[...])
    kf_o[...] = k
    kb_o[...] = k.astype(BF16)
    v = jnp.dot(h, wv_ref[...], preferred_element_type=F32)
    vf_o[...] = v
    vb_o[...] = v.astype(BF16)


def _mem_kv(mem_nb, w_xkv_bf, xk_g):
    t = mem_nb.shape[0]
    f32o = jax.ShapeDtypeStruct((t, X_WIDTH), F32)
    bfo = jax.ShapeDtypeStruct((t, X_WIDTH), BF16)
    full = lambda s: pl.BlockSpec(s, lambda i: (0,) * len(s))
    return pl.pallas_call(
        _memkv_kernel,
        out_shape=(f32o, bfo, f32o, bfo),
        grid=(1,),
        in_specs=[full((t, D_MODEL)),
                  pl.BlockSpec((D_MODEL, X_WIDTH), lambda i: (0, 0)),
                  pl.BlockSpec((D_MODEL, X_WIDTH), lambda i: (0, 1)),
                  full((1, X_HEAD_DIM))],
        out_specs=(full((t, X_WIDTH)),) * 4,
        compiler_params=_cparams(("arbitrary",), 32),
        name="mem_kv",
    )(mem_nb, w_xkv_bf, w_xkv_bf, xk_g.reshape(1, X_HEAD_DIM))


def _xattn_head(qh, kh, vh):
    s = lax.dot_general(qh, kh, (((1,), (1,)), ((), ())), preferred_element_type=F32)
    e = jnp.exp(s - jnp.max(s, axis=1, keepdims=True))
    p = e * (1.0 / jnp.sum(e, axis=1, keepdims=True))
    return jnp.dot(p.astype(BF16), vh, preferred_element_type=F32)


def _xattn_q(x, g_ref, wq_ref, qg_ref):
    hn = _rms(x, g_ref[...]).astype(BF16)
    q = jnp.dot(hn, wq_ref[...], preferred_element_type=F32)
    return _headnorm(q, qg_ref[...], X_HEAD_DIM ** -0.5)


def _xattn_prompt_kernel(x_ref, g_ref, wq_ref, qg_ref, mk_ref, mv_ref, wo_ref, o_ref):
    x = x_ref[0]
    q = _xattn_q(x, g_ref, wq_ref, qg_ref).astype(BF16)
    mk = mk_ref[0]
    mv = mv_ref[0]
    heads = []
    for g in range(X_HEADS):
        sl = slice(g * X_HEAD_DIM, (g + 1) * X_HEAD_DIM)
        heads.append(_xattn_head(q[:, sl], mk[:, sl], mv[:, sl]))
    o = jnp.concatenate(heads, axis=1).astype(BF16)
    o_ref[0] = x + jnp.dot(o, wo_ref[...], preferred_element_type=F32)


def _xattn_prompt(x, g, w_xq_bf, xq_g, mk_b, mv_b, w_xo_bf, tm):
    b, s, _ = x.shape
    tile = pl.BlockSpec((1, tm, D_MODEL), lambda bi, i: (bi, i, 0))
    mem = pl.BlockSpec((1, N_MEM, X_WIDTH), lambda bi, i: (bi, 0, 0))
    const = lambda shp: pl.BlockSpec(shp, lambda bi, i: (0, 0))
    return pl.pallas_call(
        _xattn_prompt_kernel,
        out_shape=jax.ShapeDtypeStruct((b, s, D_MODEL), F32),
        grid=(b, s // tm),
        in_specs=[tile, const((1, D_MODEL)), const((D_MODEL, X_WIDTH)), const((1, X_HEAD_DIM)),
                  mem, mem, const((X_WIDTH, D_MODEL))],
        out_specs=tile,
        compiler_params=_cparams(("parallel", "parallel"), 48),
        name="xattn_prompt",
    )(x, g.reshape(1, D_MODEL), w_xq_bf, xq_g.reshape(1, X_HEAD_DIM), mk_b, mv_b, w_xo_bf)


def _xattn_sample_kernel(x_ref, g_ref, wq_ref, qg_ref, mk_ref, mv_ref, mask_ref, wo_ref, o_ref, *, ts):
    x = x_ref[...]
    q = _xattn_q(x, g_ref, wq_ref, qg_ref)
    mask = mask_ref[...]
    nt = (((1,), (1,)), ((), ()))
    per_batch = []
    for bi in range(mk_ref.shape[0]):
        rows = slice(bi * ts, (bi + 1) * ts)
        qs = jnp.concatenate([q[rows, g * X_HEAD_DIM:(g + 1) * X_HEAD_DIM] for g in range(X_HEADS)], axis=0)
        s = lax.dot_general(qs.astype(BF16), mk_ref[bi].astype(BF16), nt, preferred_element_type=F32) + mask
        e = jnp.exp(s - jnp.max(s, axis=1, keepdims=True))
        o = jnp.dot(e.astype(BF16), mv_ref[bi].astype(BF16), preferred_element_type=F32)
        o = o * (1.0 / jnp.sum(e, axis=1, keepdims=True))
        per_batch.append(jnp.concatenate([o[g * ts:(g + 1) * ts] for g in range(X_HEADS)], axis=1))
    a = jnp.concatenate(per_batch, axis=0).astype(BF16)
    o_ref[...] = x + jnp.dot(a, wo_ref[...], preferred_element_type=F32)


def _xattn_sample(x2d, g, w_xq_bf, xq_g, cache_mk, cache_mv, w_xo_bf, ts, gb):
    t = x2d.shape[0]
    db = cache_mk.shape[0]
    mk = cache_mk.reshape(db, N_MEM * X_HEADS, X_HEAD_DIM)
    mv = cache_mv.reshape(db, N_MEM * X_HEADS, X_HEAD_DIM)
    tile = pl.BlockSpec((gb * ts, D_MODEL), lambda i: (i, 0))
    mem = pl.BlockSpec((gb, N_MEM * X_HEADS, X_HEAD_DIM), lambda i: (i, 0, 0))
    const = lambda shp: pl.BlockSpec(shp, lambda i: (0, 0))
    row_head = jnp.arange(X_HEADS * ts) // ts
    lane_head = jnp.arange(N_MEM * X_HEADS) % X_HEADS
    mask = jnp.where(row_head[:, None] == lane_head[None, :], 0.0, NEG).astype(F32)
    return pl.pallas_call(
        functools.partial(_xattn_sample_kernel, ts=ts),
        out_shape=jax.ShapeDtypeStruct((t, D_MODEL), F32),
        grid=(db // gb,),
        in_specs=[tile, const((1, D_MODEL)), const((D_MODEL, X_WIDTH)), const((1, X_HEAD_DIM)),
                  mem, mem, const((X_HEADS * ts, N_MEM * X_HEADS)), const((X_WIDTH, D_MODEL))],
        out_specs=tile,
        compiler_params=_cparams(("parallel",), 48),
        name="xattn_sample",
    )(x2d, g.reshape(1, D_MODEL), w_xq_bf, xq_g.reshape(1, X_HEAD_DIM), mk, mv, mask, w_xo_bf)


def _first_argmax(v, lane):
    vmax = jnp.max(v, axis=1, keepdims=True)
    idx = jnp.min(jnp.where(v == vmax, lane, LANES), axis=1, keepdims=True)
    return vmax, idx


def _router_kernel(xp_ref, xs_ref, g_ref, wr_ref, br_ref, h_o, eid_o, w_o, *, n_prompt_tiles):
    i = pl.program_id(0)

    def body(x_ref):
        h = _rms(x_ref[...], g_ref[...])
        h_o[...] = h
        w = wr_ref[...]
        w_hi = w.astype(BF16)
        w_lo = (w - w_hi.astype(F32)).astype(BF16)
        h_hi = h.astype(BF16)
        h_lo = (h - h_hi.astype(F32)).astype(BF16)
        mm = functools.partial(jnp.dot, preferred_element_type=F32)
        logits = mm(h_hi, w_hi) + (mm(h_lo, w_hi) + mm(h_hi, w_lo)) + br_ref[...]
        lane = lax.broadcasted_iota(jnp.int32, logits.shape, 1)
        ninf = -jnp.inf
        gl = jnp.where(lane < N_GROUPS, logits, ninf)
        gmax, gidx = _first_argmax(gl, lane)
        g_w = 1.0 / jnp.sum(jnp.exp(gl - gmax), axis=1, keepdims=True)
        lo = N_GROUPS + gidx * EXPERTS_PER_GROUP
        el = jnp.where((lane >= lo) & (lane < lo + EXPERTS_PER_GROUP), logits, ninf)
        v1, i1 = _first_argmax(el, lane)
        v2, i2 = _first_argmax(jnp.where(lane == i1, ninf, el), lane)
        e2 = jnp.exp(v2 - v1)
        w1 = g_w / (1.0 + e2)
        w2 = g_w * e2 / (1.0 + e2)
        eid_o[...] = jnp.where(lane == 0, i1 - N_GROUPS, jnp.where(lane == 1, i2 - N_GROUPS, 0))
        w_o[...] = jnp.where(lane == 0, w1, jnp.where(lane == 1, w2, 0.0))

    @pl.when(i < n_prompt_tiles)
    def _():
        body(xp_ref)

    @pl.when(i >= n_prompt_tiles)
    def _():
        body(xs_ref)


def _router(xp2d, xs2d, g, w_r, b_r, tm):
    tp, ts = xp2d.shape[0], xs2d.shape[0]
    npt, nst = tp // tm, ts // tm
    t = tp + ts
    tile = pl.BlockSpec((tm, D_MODEL), lambda i: (i, 0))
    const = lambda shp: pl.BlockSpec(shp, lambda i: (0, 0))
    return pl.pallas_call(
        functools.partial(_router_kernel, n_prompt_tiles=npt),
        out_shape=(jax.ShapeDtypeStruct((t, D_MODEL), F32),
                   jax.ShapeDtypeStruct((t, LANES), jnp.int32),
                   jax.ShapeDtypeStruct((t, LANES), F32)),
        grid=(npt + nst,),
        in_specs=[pl.BlockSpec((tm, D_MODEL), lambda i: (jnp.minimum(i, npt - 1), 0)),
                  pl.BlockSpec((tm, D_MODEL), lambda i: (jnp.maximum(i - npt, 0), 0)),
                  const((1, D_MODEL)), const((D_MODEL, LANES)), const((1, LANES))],
        out_specs=(tile, pl.BlockSpec((tm, LANES), lambda i: (i, 0)), pl.BlockSpec((tm, LANES), lambda i: (i, 0))),
        compiler_params=_cparams(("parallel",), 40),
        name="moe_router",
    )(xp2d, xs2d, g.reshape(1, D_MODEL), w_r, b_r)


def _dispatch_kernel(dest_ref, zflag_ref, h_ref, rows_hbm, zero_scr, sem, *, tmd, tm, n_tiles):
    i = pl.program_id(0)

    def zero_tile(t, start):
        @pl.when(zflag_ref[t] != 0)
        def _():
            cp = pltpu.make_async_copy(zero_scr, rows_hbm.at[pl.ds(pl.multiple_of(t * tm, tm), tm)], sem.at[2])
            if start:
                cp.start()
            else:
                cp.wait()

    @pl.when(i == 0)
    def _():
        zero_scr[...] = jnp.zeros(zero_scr.shape, zero_scr.dtype)
        lax.fori_loop(0, n_tiles, lambda t, c: (zero_tile(t, True), c)[1], 0)
        lax.fori_loop(0, n_tiles, lambda t, c: (zero_tile(t, False), c)[1], 0)

    base = 2 * i * tmd

    def row_copy(r, slot):
        return pltpu.make_async_copy(h_ref.at[pl.ds(r, 1)], rows_hbm.at[pl.ds(dest_ref[base + 2 * r + slot], 1)],
                                     sem.at[slot])

    def start(r, carry):
        row_copy(r, 0).start(priority=0)
        row_copy(r, 1).start(priority=1)
        return carry

    lax.fori_loop(0, tmd, start, 0)
    for slot in range(2):
        pltpu.make_async_copy(h_ref, rows_hbm.at[pl.ds(0, tmd)], sem.at[slot]).wait()


def _dispatch_rows(hf, dest_flat, zflag, n_tiles, tm, tmd):
    t, d = hf.shape
    grid_spec = pltpu.PrefetchScalarGridSpec(
        num_scalar_prefetch=2,
        grid=(t // tmd,),
        in_specs=[pl.BlockSpec((tmd, d), lambda i, dest, zf: (i, 0))],
        out_specs=pl.BlockSpec(memory_space=pl.ANY),
        scratch_shapes=[pltpu.VMEM((tm, d), hf.dtype), pltpu.SemaphoreType.DMA((3,))],
    )
    return pl.pallas_call(
        functools.partial(_dispatch_kernel, tmd=tmd, tm=tm, n_tiles=n_tiles),
        out_shape=jax.ShapeDtypeStruct((n_tiles * tm, d), hf.dtype),
        grid_spec=grid_spec,
        compiler_params=_cparams(("arbitrary",), 32),
        name="moe_dispatch",
    )(dest_flat, zflag, hf)


def _ffn_kernel(te_ref, slot_ref, fetch_ref, e1_ref, na_ref, x_ref, wgu_hbm, wd_hbm, o_ref,
                wgu_land, wd_land, wgu_scr, wd_scr, sem):
    t = pl.program_id(0)

    def weight_copies(e, slot):
        return (pltpu.make_async_copy(wgu_hbm.at[e], wgu_land.at[slot], sem.at[0, slot]),
                pltpu.make_async_copy(wd_hbm.at[e], wd_land.at[slot], sem.at[1, slot]))

    @pl.when(t < na_ref[0])
    def _():
        e = te_ref[t]
        slot = slot_ref[t]

        @pl.when(t == 0)
        def _():
            for cp in weight_copies(e, 0):
                cp.start()

            @pl.when(e1_ref[0] >= 0)
            def _():
                for cp in weight_copies(e1_ref[0], 1):
                    cp.start()

        @pl.when((t == 0) | (e != te_ref[jnp.maximum(t - 1, 0)]))
        def _():
            for cp in weight_copies(e, slot):
                cp.wait()
            wgu_scr[...] = wgu_land[slot].astype(BF16)
            wd_scr[...] = wd_land[slot].astype(BF16)

            @pl.when(fetch_ref[t] >= 0)
            def _():
                for cp in weight_copies(fetch_ref[t], slot):
                    cp.start()

        gu = jnp.dot(x_ref[...].astype(BF16), wgu_scr[...], preferred_element_type=F32)
        gate = gu[:, :D_FF]
        act = (gate * jax.nn.sigmoid(gate)) * gu[:, D_FF:]
        o_ref[...] = jnp.dot(act.astype(BF16), wd_scr[...], preferred_element_type=F32)

    @pl.when(t >= na_ref[0])
    def _():
        o_ref[...] = jnp.zeros(o_ref.shape, o_ref.dtype)


def _expert_ffn(xs, tile_e, slot_tab, fetch_tab, e1, n_active, w_gu, w_down, tm):
    n_tiles = xs.shape[0] // tm
    grid_spec = pltpu.PrefetchScalarGridSpec(
        num_scalar_prefetch=5,
        grid=(n_tiles,),
        in_specs=[pl.BlockSpec((tm, D_MODEL), lambda t, te, sl, fe, e1, na: (jnp.minimum(t, na[0] - 1), 0)),
                  pl.BlockSpec(memory_space=pl.ANY), pl.BlockSpec(memory_space=pl.ANY)],
        out_specs=pl.BlockSpec((tm, D_MODEL), lambda t, te, sl, fe, e1, na: (t, 0)),
        scratch_shapes=[pltpu.VMEM((2, D_MODEL, 2 * D_FF), F32), pltpu.VMEM((2, D_FF, D_MODEL), F32),
                        pltpu.VMEM((D_MODEL, 2 * D_FF), BF16), pltpu.VMEM((D_FF, D_MODEL), BF16),
                        pltpu.SemaphoreType.DMA((2, 2))],
    )
    return pl.pallas_call(
        _ffn_kernel,
        out_shape=jax.ShapeDtypeStruct((n_tiles * tm, D_MODEL), F32),
        grid_spec=grid_spec,
        compiler_params=_cparams(("arbitrary",), 48),
        name="moe_ffn",
    )(tile_e, slot_tab, fetch_tab, e1, n_active, xs, w_gu, w_down)


def _combine_kernel(dest_ref, x_ref, w_ref, ys_hbm, o_ref, buf_a, buf_b, sem, *, tm, base_tok):
    base = 2 * (base_tok + pl.program_id(0) * tm)
    bufs = (buf_a, buf_b)

    def start(r, carry):
        for slot in range(2):
            pltpu.make_async_copy(ys_hbm.at[pl.ds(dest_ref[base + 2 * r + slot], 1)], bufs[slot].at[pl.ds(r, 1)],
                                  sem.at[slot]).start(priority=slot)
        return carry

    lax.fori_loop(0, tm, start, 0)
    for slot in range(2):
        pltpu.make_async_copy(ys_hbm.at[pl.ds(0, tm)], bufs[slot], sem.at[slot]).wait()
    w = w_ref[...]
    o_ref[...] = x_ref[...] + (w[:, 0:1] * buf_a[...] + w[:, 1:2] * buf_b[...])


def _combine(x2d, wts, ys, dest_flat, base_tok, tm):
    t, d = x2d.shape
    base_blk = base_tok // tm
    grid_spec = pltpu.PrefetchScalarGridSpec(
        num_scalar_prefetch=1,
        grid=(t // tm,),
        in_specs=[pl.BlockSpec((tm, d), lambda i, dest: (i, 0)),
                  pl.BlockSpec((tm, LANES), lambda i, dest: (i + base_blk, 0)),
                  pl.BlockSpec(memory_space=pl.ANY)],
        out_specs=pl.BlockSpec((tm, d), lambda i, dest: (i, 0)),
        scratch_shapes=[pltpu.VMEM((tm, d), F32), pltpu.VMEM((tm, d), F32), pltpu.SemaphoreType.DMA((2,))],
    )
    return pl.pallas_call(
        functools.partial(_combine_kernel, tm=tm, base_tok=base_tok),
        out_shape=jax.ShapeDtypeStruct((t, d), F32),
        grid_spec=grid_spec,
        compiler_params=_cparams(("arbitrary",), 32),
        name="moe_combine",
    )(dest_flat, x2d, wts, ys)


def _route_plan(eid, tm, n_tiles):
    flat_e = eid.reshape(-1)
    onehot = (flat_e[:, None] == jnp.arange(N_EXPERTS, dtype=jnp.int32)[None, :]).astype(jnp.int32)
    csum = jnp.cumsum(onehot, axis=0)
    rank = jnp.sum(onehot * csum, axis=1) - 1
    counts = csum[-1]
    padded = ((counts + tm - 1) // tm) * tm
    ends = jnp.cumsum(padded)
    dest = (jnp.sum(onehot * (ends - padded)[None, :], axis=1) + rank).astype(jnp.int32)
    n_active = (ends[-1] // tm).astype(jnp.int32)
    tile = jnp.arange(n_tiles, dtype=jnp.int32)
    active = tile < n_active
    tile_e = jnp.sum((ends[None, :] <= (tile * tm)[:, None]).astype(jnp.int32), axis=1)
    last_e = jnp.sum(jnp.where(tile == n_active - 1, tile_e, 0))
    tile_e = jnp.where(active, tile_e, last_e).astype(jnp.int32)
    has_pad = jnp.any((ends[None, :] == ((tile + 1) * tm)[:, None]) & (padded != counts)[None, :], axis=1)
    zflag = (has_pad | ~active).astype(jnp.int32)
    e_ids = jnp.arange(N_EXPERTS, dtype=jnp.int32)
    present = padded > 0

    def next_present(after):
        return jnp.min(jnp.where(present[None, :] & (e_ids[None, :] > after[:, None]), e_ids[None, :], N_EXPERTS),
                       axis=1)

    run_idx = jnp.sum((present[None, :] & (e_ids[None, :] < tile_e[:, None])).astype(jnp.int32), axis=1)
    slot_tab = (run_idx % 2).astype(jnp.int32)
    n1 = next_present(tile_e)
    n2 = next_present(n1)
    fetch_tab = jnp.where(n2 < N_EXPERTS, n2, -1).astype(jnp.int32)
    e1 = jnp.where(n1[0] < N_EXPERTS, n1[0], -1).astype(jnp.int32).reshape(1)
    return dest, tile_e, slot_tab, fetch_tab, e1, n_active.reshape(1), zflag


def kernel(x_prompt, x_sample, cache_attn_k, cache_attn_v, cache_conv, cache_mem_k, cache_mem_v, page_table, mem_prompt, norm_mix_g, w_in, b_gate, q_norm_g, k_norm_g, lambda_q1, lambda_k1, lambda_q2, lambda_k2, subln_g, conv_w, w_out, norm_mem_g, norm_x_g, w_xq, w_xkv, xq_norm_g, xk_norm_g, w_xo, norm_ffn_g, w_group, b_group, w_expert_router, b_expert_router, w_gate_up, w_down):
    depth = w_in.shape[0]
    bp, sp, _ = x_prompt.shape
    bs, ts, _ = x_sample.shape
    tp, tsn = bp * sp, bs * ts
    slopes = jnp.exp2(-8.0 * jnp.arange(1, N_HEADS + 1, dtype=F32) / N_HEADS) * LOG2E
    xp = x_prompt.reshape(tp, D_MODEL)
    xs = x_sample.reshape(tsn, D_MODEL)
    moe_tm = 256
    moe_tiles = (tp + tsn) * 2 // moe_tm + N_EXPERTS
    outs = [[] for _ in range(8)]
    for l in range(depth):
        lam_init = 0.8 - 0.6 * math.exp(-0.3 * l)
        lam = (jnp.exp(jnp.sum(lambda_q1[l] * lambda_k1[l])) - jnp.exp(jnp.sum(lambda_q2[l] * lambda_k2[l]))
               + lam_init).reshape(1).astype(F32)
        out_scale = 1.0 - lam_init
        w_in_bf = w_in[l].astype(BF16)
        w_out_bf = w_out[l].astype(BF16)
        w_xq_bf = w_xq[l].astype(BF16)
        w_xkv_bf = w_xkv[l].astype(BF16)
        w_xo_bf = w_xo[l].astype(BF16)

        hb = _rmsnorm_cast(xp, norm_mix_g[l], 512)
        q, kf, kb, vf, vb, z, b, ga, gb = _inproj(hb, w_in_bf, q_norm_g[l], k_norm_g[l], b_gate[l], BF16, BF16,
                                                  1024, 256)
        o = _prompt_attn(q.reshape(bp, sp, D_MODEL), kb.reshape(bp, sp, D_MODEL), vb.reshape(bp, sp, D_MODEL),
                         lam, slopes, subln_g[l], out_scale, 512, 4)
        xp = _mix_outproj_prompt(o.reshape(tp, D_MODEL), ga, gb, b, z, conv_w[l], w_out_bf, xp, sp, 256, D_MODEL)
        outs[0].append(kf.reshape(bp, sp, N_HEADS, HEAD_W))
        outs[1].append(vf.reshape(bp, sp, N_HEADS, V_DIM))
        outs[2].append(z.reshape(bp, sp, D_MODEL)[:, sp - 2:, :])

        hb = _rmsnorm_cast(xs, norm_mix_g[l], 512)
        q, kf, kb, vf, vb, z, b, ga, gb = _inproj(hb, w_in_bf, q_norm_g[l], k_norm_g[l], b_gate[l], F32, F32,
                                                  1024, 256)
        shp = (bs, ts, D_MODEL)
        o = _sample_attn(q.reshape(shp), kf.reshape(shp), vf.reshape(shp), cache_attn_k[l], cache_attn_v[l],
                         page_table, lam, slopes, subln_g[l], out_scale, 4, 3)
        z3 = z.reshape(shp)
        merged = _mix_sample(o, ga.reshape(shp), gb.reshape(shp), b.reshape(shp), z3, cache_conv[l], conv_w[l], 32)
        xs = _mm_res(merged.reshape(tsn, D_MODEL), w_out_bf, xs, 512, 1024)
        outs[5].append(kf.reshape(bs, ts, N_HEADS, HEAD_W))
        outs[6].append(vf.reshape(bs, ts, N_HEADS, V_DIM))
        outs[7].append(z3[:, ts - 2:, :])

        mem_nb = _rmsnorm_cast(mem_prompt.reshape(bp * N_MEM, D_MODEL), norm_mem_g[l], 512)
        mkf, mkb, mvf, mvb = _mem_kv(mem_nb, w_xkv_bf, xk_norm_g[l])
        xp = _xattn_prompt(xp.reshape(bp, sp, D_MODEL), norm_x_g[l], w_xq_bf, xq_norm_g[l],
                           mkb.reshape(bp, N_MEM, X_WIDTH), mvb.reshape(bp, N_MEM, X_WIDTH), w_xo_bf,
                           512).reshape(tp, D_MODEL)
        xs = _xattn_sample(xs, norm_x_g[l], w_xq_bf, xq_norm_g[l], cache_mem_k[l], cache_mem_v[l], w_xo_bf, ts, 8)
        outs[3].append(mkf.reshape(bp, N_MEM, X_HEADS, X_HEAD_DIM))
        outs[4].append(mvf.reshape(bp, N_MEM, X_HEADS, X_HEAD_DIM))

        w_r = jnp.zeros((D_MODEL, LANES), F32).at[:, :N_GROUPS].set(w_group[l])
        w_r = w_r.at[:, N_GROUPS:N_GROUPS + N_EXPERTS].set(w_expert_router[l])
        b_r = jnp.zeros((1, LANES), F32).at[0, :N_GROUPS].set(b_group[l])
        b_r = b_r.at[0, N_GROUPS:N_GROUPS + N_EXPERTS].set(b_expert_router[l])
        hf, eid, wts = _router(xp, xs, norm_ffn_g[l], w_r, b_r, 512)
        dest, tile_e, slot_tab, fetch_tab, e1, n_active, zflag = _route_plan(eid[:, :2], moe_tm, moe_tiles)
        rows = _dispatch_rows(hf, dest, zflag, moe_tiles, moe_tm, 256)
        ys = _expert_ffn(rows, tile_e, slot_tab, fetch_tab, e1, n_active, w_gate_up[l], w_down[l], moe_tm)
        xp = _combine(xp, wts, ys, dest, 0, moe_tm)
        xs = _combine(xs, wts, ys, dest, tp, moe_tm)
    stack = lambda i: jnp.stack(outs[i])
    return (xp.reshape(bp, sp, D_MODEL), xs.reshape(bs, ts, D_MODEL), stack(0), stack(1), stack(2), stack(3),
            stack(4), stack(5), stack(6), stack(7))
```

```python
import functools
import math

import jax
import jax.numpy as jnp
from jax import lax
from jax.experimental import pallas as pl
from jax.experimental.pallas import tpu as pltpu

F32 = jnp.float32
BF16 = jnp.bfloat16

D_MODEL = 2048
N_HEADS = 8
HEAD_DIM = 128
V_DIM = 256
HEAD_W = 2 * HEAD_DIM
PAGE = 128
X_HEADS = 4
X_HEAD_DIM = 128
X_WIDTH = X_HEADS * X_HEAD_DIM
N_MEM = 256
N_GROUPS = 4
EXPERTS_PER_GROUP = 8
N_EXPERTS = N_GROUPS * EXPERTS_PER_GROUP
D_FF = 512
EPS = 1e-6
NEG = -1e30
LANES = 128
MIB = 1024 * 1024
LOG2E = math.log2(math.e)


def _cparams(sem, vmem_mib):
    return pltpu.CompilerParams(dimension_semantics=sem, vmem_limit_bytes=vmem_mib * MIB)


def _rms(x, g):
    ms = jnp.mean(x * x, axis=-1, keepdims=True)
    return x * lax.rsqrt(ms + EPS) * g


def _headnorm(acc, g, scale=None):
    outs = []
    for c in range(acc.shape[1] // HEAD_DIM):
        y = _rms(acc[:, c * HEAD_DIM:(c + 1) * HEAD_DIM], g)
        outs.append(y if scale is None else y * scale)
    return jnp.concatenate(outs, axis=1)


def _norm_kernel(x_ref, g_ref, o_ref):
    o_ref[...] = _rms(x_ref[...], g_ref[...]).astype(o_ref.dtype)


def _rmsnorm_cast(x2d, g, tm):
    t, d = x2d.shape
    return pl.pallas_call(
        _norm_kernel,
        out_shape=jax.ShapeDtypeStruct((t, d), BF16),
        grid=(t // tm,),
        in_specs=[pl.BlockSpec((tm, d), lambda i: (i, 0)),
                  pl.BlockSpec((1, d), lambda i: (0, 0))],
        out_specs=pl.BlockSpec((tm, d), lambda i: (i, 0)),
        compiler_params=_cparams(("parallel",), 32),
        name="rmsnorm_cast",
    )(x2d, g.reshape(1, d))


def _inproj_kernel(h_ref, wq, wk, wv, wc, wb, wu, wga, wgb, qg_ref, kg_ref, bga_ref, bgb_ref,
                   q_o, kf_o, kb_o, vf_o, vb_o, z_o, b_o, ga_o, gb_o, *, q_scale):
    h = h_ref[...]

    def mm(w):
        return jnp.dot(h, w[...], preferred_element_type=F32)

    q_o[...] = _headnorm(mm(wq), qg_ref[...], q_scale).astype(q_o.dtype)
    k = _headnorm(mm(wk), kg_ref[...])
    kf_o[...] = k
    kb_o[...] = k.astype(BF16)
    v = mm(wv)
    vf_o[...] = v
    vb_o[...] = v.astype(BF16)
    z_o[...] = mm(wc) * mm(wu)
    b_o[...] = mm(wb).astype(b_o.dtype)
    ga_o[...] = jax.nn.sigmoid(mm(wga) + bga_ref[...]).astype(ga_o.dtype)
    gb_o[...] = jax.nn.sigmoid(mm(wgb) + bgb_ref[...]).astype(gb_o.dtype)


def _inproj(hb, w_in_bf, q_g, k_g, b_gate, q_dtype, aux_dtype, tm, tn):
    t = hb.shape[0]
    nb = D_MODEL // tn
    region = {"q": 0, "k": 1, "v": 2, "c": 3, "b": 4, "u": 5, "ga": 6, "gb": 7}

    def wspec(r):
        return pl.BlockSpec((D_MODEL, tn), lambda i, j, r=r: (0, r * nb + j))

    tile = pl.BlockSpec((tm, tn), lambda i, j: (i, j))
    vec = pl.BlockSpec((1, HEAD_DIM), lambda i, j: (0, 0))
    bias = pl.BlockSpec((1, tn), lambda i, j: (0, j))
    f32o = jax.ShapeDtypeStruct((t, D_MODEL), F32)
    bfo = jax.ShapeDtypeStruct((t, D_MODEL), BF16)
    return pl.pallas_call(
        functools.partial(_inproj_kernel, q_scale=HEAD_DIM ** -0.5 * LOG2E),
        out_shape=(jax.ShapeDtypeStruct((t, D_MODEL), q_dtype), f32o, bfo, f32o, bfo, f32o)
        + (jax.ShapeDtypeStruct((t, D_MODEL), aux_dtype),) * 3,
        grid=(t // tm, nb),
        in_specs=[pl.BlockSpec((tm, D_MODEL), lambda i, j: (i, 0))]
        + [wspec(region[n]) for n in ("q", "k", "v", "c", "b", "u", "ga", "gb")]
        + [vec, vec, bias, bias],
        out_specs=(tile,) * 9,
        compiler_params=_cparams(("parallel", "arbitrary"), 48),
        name="inproj",
    )(hb, *([w_in_bf] * 8), q_g.reshape(1, HEAD_DIM), k_g.reshape(1, HEAD_DIM),
      b_gate[0:1], b_gate[1:2])


def _lane_tiles(x):
    return [x[:, c * LANES:(c + 1) * LANES] for c in range(x.shape[1] // LANES)]


def _lane_repeat(x, n):
    return x if n == 1 else jnp.concatenate([x] * n, axis=1)


def _online_update(s_list, v_list, m_prev, l_prev, acc_prev, c_list=None):
    if c_list is None:
        c_list = [None] * len(s_list)
    tops = []
    for s, c in zip(s_list, c_list):
        top = functools.reduce(jnp.maximum, _lane_tiles(s))
        tops.append(top if c is None else top + c)
    m_cur = jnp.max(functools.reduce(jnp.maximum, tops), axis=1, keepdims=True)
    m_next = jnp.maximum(m_prev, m_cur)
    alpha = jnp.exp2(m_prev - m_next)
    l_next = alpha * l_prev
    pv = None
    for s, v, c in zip(s_list, v_list, c_list):
        shift = m_next if c is None else m_next - c
        p = jnp.exp2(s - _lane_repeat(shift, s.shape[1] // LANES))
        l_next = l_next + functools.reduce(jnp.add, _lane_tiles(p))
        d = v(p.astype(BF16)) if callable(v) else jnp.dot(p.astype(BF16), v, preferred_element_type=F32)
        pv = d if pv is None else pv + d
    acc_next = acc_prev * _lane_repeat(alpha, V_DIM // LANES) + pv
    return m_next, l_next, acc_next


def _diff_out(a1, l1, a2, l2, lam, g, out_scale):
    r1 = 1.0 / jnp.sum(l1, axis=1, keepdims=True)
    r2 = 1.0 / jnp.sum(l2, axis=1, keepdims=True)
    o = a1 * r1 - lam * (a2 * r2)
    return _rms(o, g) * out_scale


def _init_softmax_state(m_scr, l_scr, acc_scr):
    m_scr[...] = jnp.full(m_scr.shape, NEG, F32)
    l_scr[...] = jnp.zeros(l_scr.shape, F32)
    acc_scr[...] = jnp.zeros(acc_scr.shape, F32)


def _pattn_kernel(qi_ref, ki_ref, lam_ref, slopes_ref, q_ref, k_ref, v_ref, g_ref, o_ref, m_scr, l_scr, acc_scr,
                  *, t, hp, out_scale):
    h = pl.program_id(1)
    step = pl.program_id(2)
    qi = qi_ref[step]
    ki = ki_ref[step]

    @pl.when(ki == 0)
    def _():
        _init_softmax_state(m_scr, l_scr, acc_scr)

    def body(masked):
        kpos = ((ki - qi) * t + lax.broadcasted_iota(jnp.int32, (1, t), 1)).astype(F32)
        if masked:
            keep = (lax.broadcasted_iota(jnp.int32, (t, t), 1)
                    <= lax.broadcasted_iota(jnp.int32, (t, t), 0))
        state = []
        for i in range(hp):
            bias = slopes_ref[h * hp + i] * kpos
            v = v_ref[0, :, i * V_DIM:(i + 1) * V_DIM]
            for m in range(2):
                cols = slice(i * HEAD_W + m * HEAD_DIM, i * HEAD_W + (m + 1) * HEAD_DIM)
                s = lax.dot_general(q_ref[0, :, cols], k_ref[0, :, cols], (((1,), (1,)), ((), ())),
                                    preferred_element_type=F32)
                s = s + bias
                if masked:
                    s = jnp.where(keep, s, NEG)
                idx = 2 * i + m
                state.append(_online_update([s], [v], m_scr[idx], l_scr[idx], acc_scr[idx]))
        return state

    @pl.when(ki < qi)
    def _():
        for idx, (m_next, l_next, acc_next) in enumerate(body(False)):
            m_scr[idx] = m_next
            l_scr[idx] = l_next
            acc_scr[idx] = acc_next

    @pl.when(ki == qi)
    def _():
        state = body(True)
        lam = lam_ref[0]
        g = g_ref[...]
        o_ref[0] = jnp.concatenate(
            [_diff_out(state[2 * i][2], state[2 * i][1], state[2 * i + 1][2], state[2 * i + 1][1], lam, g, out_scale)
             for i in range(hp)], axis=1).astype(o_ref.dtype)


def _prompt_attn(qb, kb, vb, lam, slopes2, subln_g, out_scale, t, hp):
    b, s, _ = qb.shape
    n = s // t
    pairs = [(qi, ki) for qi in range(n) for ki in range(qi + 1)]
    qi_tab = jnp.asarray([p[0] for p in pairs], jnp.int32)
    ki_tab = jnp.asarray([p[1] for p in pairs], jnp.int32)
    smem = pl.BlockSpec(memory_space=pltpu.SMEM)
    grid_spec = pltpu.PrefetchScalarGridSpec(
        num_scalar_prefetch=2,
        grid=(b, N_HEADS // hp, len(pairs)),
        in_specs=[smem, smem,
                  pl.BlockSpec((1, t, hp * HEAD_W), lambda bi, h, st, qt, kt: (bi, qt[st], h)),
                  pl.BlockSpec((1, t, hp * HEAD_W), lambda bi, h, st, qt, kt: (bi, kt[st], h)),
                  pl.BlockSpec((1, t, hp * V_DIM), lambda bi, h, st, qt, kt: (bi, kt[st], h)),
                  pl.BlockSpec((1, V_DIM), lambda bi, h, st, qt, kt: (0, 0))],
        out_specs=pl.BlockSpec((1, t, hp * V_DIM), lambda bi, h, st, qt, kt: (bi, qt[st], h)),
        scratch_shapes=[pltpu.VMEM((2 * hp, t, LANES), F32), pltpu.VMEM((2 * hp, t, LANES), F32),
                        pltpu.VMEM((2 * hp, t, V_DIM), F32)],
    )
    return pl.pallas_call(
        functools.partial(_pattn_kernel, t=t, hp=hp, out_scale=out_scale),
        out_shape=jax.ShapeDtypeStruct((b, s, D_MODEL), BF16),
        grid_spec=grid_spec,
        compiler_params=_cparams(("parallel", "parallel", "arbitrary"), 40),
        name="prompt_attn",
    )(qi_tab, ki_tab, lam, slopes2, qb, kb, vb, subln_g.reshape(1, V_DIM))


def _sattn_bias(slopes2, tq):
    rows = jnp.arange(2 * N_HEADS * tq)
    row_h = (rows // tq) % N_HEADS
    slope = slopes2[row_h][:, None]
    lanes = jnp.arange(PAGE * N_HEADS)
    page = jnp.where(row_h[:, None] == (lanes % N_HEADS)[None, :], slope * (lanes // N_HEADS)[None, :].astype(F32), NEG)
    tok = jnp.arange(LANES)
    new = jnp.where(tok[None, :] <= (rows % tq)[:, None], slope * tok[None, :].astype(F32), NEG)
    return page.astype(F32), new.astype(F32)


def _sattn_kernel(pt_ref, lam_ref, q_ref, kn_ref, vn_ref, pk_hbm, pv_hbm, bias_ref, bias_new_ref, slope_ref, g_ref,
                  o_ref, k_buf, v_buf, qbd_scr, qfull_scr, m_scr, l_scr, acc_scr, sem,
                  *, npg, n_steps, n_slots, past, out_scale):
    p = pl.program_id(1)
    tq = q_ref.shape[1]
    half = N_HEADS * tq

    g = pl.program_id(0) * n_steps + p
    total = pl.num_programs(0) * n_steps

    def page_copies(step):
        slot = step % n_slots
        cps = []
        for j in range(npg):
            page = pt_ref[step // n_steps, (step % n_steps) * npg + j]
            cps.append(pltpu.make_async_copy(pk_hbm.at[page], k_buf.at[slot, j], sem.at[0, slot]))
            cps.append(pltpu.make_async_copy(pv_hbm.at[page], v_buf.at[slot, j], sem.at[1, slot]))
        return cps

    @pl.when(g == 0)
    def _():
        for first in range(n_slots - 1):
            for cp in page_copies(jnp.int32(first)):
                cp.start()

    @pl.when(g + (n_slots - 1) < total)
    def _():
        for cp in page_copies(g + (n_slots - 1)):
            cp.start()

    for cp in page_copies(g):
        cp.wait()
    slot = g % n_slots

    @pl.when(p == 0)
    def _():
        _init_softmax_state(m_scr, l_scr, acc_scr)
        q = q_ref[0]
        zero = jnp.zeros((half, HEAD_DIM), F32)
        q1, q2 = [jnp.concatenate([q[:, h * HEAD_W + m * HEAD_DIM:h * HEAD_W + (m + 1) * HEAD_DIM]
                                   for h in range(N_HEADS)], axis=0) for m in range(2)]
        qbd_scr[...] = jnp.concatenate([jnp.concatenate([q1, zero], axis=1),
                                        jnp.concatenate([zero, q2], axis=1)], axis=0).astype(BF16)
        q_rep = jnp.concatenate([q] * N_HEADS, axis=0)
        lane = lax.broadcasted_iota(jnp.int32, q_rep.shape, 1)
        own_head = lane // HEAD_W == lax.broadcasted_iota(jnp.int32, q_rep.shape, 0) // tq
        qfull_scr[...] = jnp.concatenate(
            [jnp.where(own_head & ((lane // HEAD_DIM) % 2 == m), q_rep, 0.0) for m in range(2)],
            axis=0).astype(BF16)

    nt = (((1,), (1,)), ((), ()))

    def body(last):
        qbd = qbd_scr[...]
        bias = bias_ref[...]
        slope = slope_ref[...]
        s_list, v_list, c_list = [], [], []
        for j in range(npg):
            s = lax.dot_general(qbd, k_buf[slot, j].astype(BF16), nt, preferred_element_type=F32)
            s_list.append(s + bias)
            v_list.append(v_buf[slot, j].astype(BF16))
            c_list.append(slope * ((p * npg + j) * PAGE - past).astype(F32))
        if last:
            pad = jnp.zeros((LANES - tq, D_MODEL), F32)
            kn = jnp.concatenate([kn_ref[0], pad], axis=0).astype(BF16)
            vn = jnp.concatenate([vn_ref[0], pad], axis=0).astype(BF16)
            s = lax.dot_general(qfull_scr[...], kn, nt, preferred_element_type=F32)
            s_list.append(s + bias_new_ref[...])

            def new_values(pb):
                full = jnp.dot(pb, vn, preferred_element_type=F32)
                return jnp.concatenate([full[(m * N_HEADS + h) * tq:(m * N_HEADS + h + 1) * tq,
                                             h * V_DIM:(h + 1) * V_DIM]
                                        for m in range(2) for h in range(N_HEADS)], axis=0)

            v_list.append(new_values)
            c_list.append(None)
        return _online_update(s_list, v_list, m_scr[...], l_scr[...], acc_scr[...], c_list)

    @pl.when(p < n_steps - 1)
    def _():
        m_next, l_next, acc_next = body(False)
        m_scr[...] = m_next
        l_scr[...] = l_next
        acc_scr[...] = acc_next

    @pl.when(p == n_steps - 1)
    def _():
        _, l, acc = body(True)
        y = _diff_out(acc[:half], l[:half], acc[half:], l[half:], lam_ref[0], g_ref[...], out_scale)
        o_ref[0] = jnp.concatenate([y[h * tq:(h + 1) * tq] for h in range(N_HEADS)], axis=1)


def _sample_attn(qs, kn, vn, pool_k, pool_v, page_table, lam, slopes2, subln_g, out_scale, npg, n_slots):
    db, tq, _ = qs.shape
    n_pages = page_table.shape[1]
    n_steps = n_pages // npg
    n_phys = pool_k.shape[0]
    rows = 2 * N_HEADS * tq
    page_rows = PAGE * N_HEADS
    pk = pool_k.reshape(n_phys, page_rows, HEAD_W)
    pv = pool_v.reshape(n_phys, page_rows, V_DIM)
    bias, bias_new = _sattn_bias(slopes2, tq)
    slope_rep = jnp.broadcast_to(slopes2[(jnp.arange(rows) // tq) % N_HEADS][:, None], (rows, LANES))
    smem = pl.BlockSpec(memory_space=pltpu.SMEM)
    tok = pl.BlockSpec((1, tq, D_MODEL), lambda b, p, pt: (b, 0, 0))
    const = lambda shp: pl.BlockSpec(shp, lambda b, p, pt: (0, 0))

    hbm = pl.BlockSpec(memory_space=pl.ANY)
    grid_spec = pltpu.PrefetchScalarGridSpec(
        num_scalar_prefetch=1,
        grid=(db, n_steps),
        in_specs=[smem, tok, tok, tok, hbm, hbm,
                  const((rows, page_rows)), const((rows, LANES)), const((rows, LANES)), const((1, V_DIM))],
        out_specs=tok,
        scratch_shapes=[pltpu.VMEM((n_slots, npg, page_rows, HEAD_W), F32),
                        pltpu.VMEM((n_slots, npg, page_rows, V_DIM), F32),
                        pltpu.VMEM((rows, HEAD_W), BF16), pltpu.VMEM((rows, D_MODEL), BF16),
                        pltpu.VMEM((rows, LANES), F32), pltpu.VMEM((rows, LANES), F32),
                        pltpu.VMEM((rows, V_DIM), F32), pltpu.SemaphoreType.DMA((2, n_slots))],
    )
    return pl.pallas_call(
        functools.partial(_sattn_kernel, npg=npg, n_steps=n_steps, n_slots=n_slots, past=n_pages * PAGE,
                          out_scale=out_scale),
        out_shape=jax.ShapeDtypeStruct((db, tq, D_MODEL), F32),
        grid_spec=grid_spec,
        compiler_params=_cparams(("arbitrary", "arbitrary"), 48),
        name="sample_attn",
    )(page_table, lam, qs, kn, vn, pk, pv, bias, bias_new, slope_rep, subln_g.reshape(1, V_DIM))


def _conv_merge(o, ga, gb, b, z, z1, z2, cw):
    y = z2 * cw[0] + z1 * cw[1] + z * cw[2]
    return (ga.astype(F32) * o.astype(F32) + gb.astype(F32) * (b.astype(F32) * y)).astype(BF16)


def _mix_outproj_kernel(o_ref, ga_ref, gb_ref, b_ref, z_ref, zp_ref, cw_ref, w_ref, r_ref, out_ref, merged_scr,
                        *, tiles_per_seq):
    i = pl.program_id(0)

    @pl.when(pl.program_id(1) == 0)
    def _():
        z = z_ref[...]
        tm = z.shape[0]
        zp = jnp.where(i % tiles_per_seq == 0, 0.0, zp_ref[...])
        pm1 = zp[7:8, :]
        pm2 = zp[6:7, :]
        row = lax.broadcasted_iota(jnp.int32, (tm, 1), 0)
        z1 = jnp.where(row == 0, pm1, pltpu.roll(z, 1, 0))
        z2 = jnp.where(row == 0, pm2, jnp.where(row == 1, pm1, pltpu.roll(z, 2, 0)))
        cw = cw_ref[...]
        merged_scr[...] = _conv_merge(o_ref[...], ga_ref[...], gb_ref[...], b_ref[...], z, z1, z2,
                                      (cw[0:1], cw[1:2], cw[2:3]))

    out_ref[...] = r_ref[...] + jnp.dot(merged_scr[...], w_ref[...], preferred_element_type=F32)


def _mix_outproj_prompt(o, ga, gb, b, z, conv_w, w, r, seq, tm, tn):
    t = z.shape[0]
    n = w.shape[1]
    row = pl.BlockSpec((tm, D_MODEL), lambda i, j: (i, 0))
    prev = pl.BlockSpec((8, D_MODEL), lambda i, j: (jnp.maximum(i * (tm // 8) - 1, 0), 0))
    tile = pl.BlockSpec((tm, tn), lambda i, j: (i, j))
    return pl.pallas_call(
        functools.partial(_mix_outproj_kernel, tiles_per_seq=seq // tm),
        out_shape=jax.ShapeDtypeStruct((t, n), F32),
        grid=(t // tm, n // tn),
        in_specs=[row, row, row, row, row, prev, pl.BlockSpec((3, D_MODEL), lambda i, j: (0, 0)),
                  pl.BlockSpec((D_MODEL, tn), lambda i, j: (0, j)), tile],
        out_specs=tile,
        scratch_shapes=[pltpu.VMEM((tm, D_MODEL), BF16)],
        compiler_params=_cparams(("parallel", "arbitrary"), 48),
        name="mix_outproj",
    )(o, ga, gb, b, z, z, conv_w, w, r)


def _mix_sample_kernel(o_ref, ga_ref, gb_ref, b_ref, z_ref, cc_ref, cw_ref, out_ref):
    z = z_ref[...]
    cc = cc_ref[...]
    c0 = cc[:, 0:1, :]
    c1 = cc[:, 1:2, :]
    row = lax.broadcasted_iota(jnp.int32, (1, z.shape[1], 1), 1)
    z1 = jnp.where(row == 0, c1, pltpu.roll(z, 1, 1))
    z2 = jnp.where(row == 0, c0, jnp.where(row == 1, c1, pltpu.roll(z, 2, 1)))
    cw = cw_ref[...]
    out_ref[...] = _conv_merge(o_ref[...], ga_ref[...], gb_ref[...], b_ref[...], z, z1, z2,
                               (cw[0:1][None], cw[1:2][None], cw[2:3][None]))


def _mix_sample(o, ga, gb, b, z, cache_conv, conv_w, g):
    db, ts, _ = z.shape
    tile = pl.BlockSpec((g, ts, D_MODEL), lambda i: (i, 0, 0))
    return pl.pallas_call(
        _mix_sample_kernel,
        out_shape=jax.ShapeDtypeStruct((db, ts, D_MODEL), BF16),
        grid=(db // g,),
        in_specs=[tile, tile, tile, tile, tile,
                  pl.BlockSpec((g, cache_conv.shape[1], D_MODEL), lambda i: (i, 0, 0)),
                  pl.BlockSpec((3, D_MODEL), lambda i: (0, 0))],
        out_specs=tile,
        compiler_params=_cparams(("parallel",), 40),
        name="mix_sample",
    )(o, ga, gb, b, z, cache_conv, conv_w)


def _mm_res_kernel(a_ref, w_ref, r_ref, o_ref):
    o_ref[...] = r_ref[...] + jnp.dot(a_ref[...], w_ref[...], preferred_element_type=F32)


def _mm_res(a, w, r, tm, tn):
    t, k = a.shape
    n = w.shape[1]
    return pl.pallas_call(
        _mm_res_kernel,
        out_shape=jax.ShapeDtypeStruct((t, n), F32),
        grid=(t // tm, n // tn),
        in_specs=[pl.BlockSpec((tm, k), lambda i, j: (i, 0)),
                  pl.BlockSpec((k, tn), lambda i, j: (0, j)),
                  pl.BlockSpec((tm, tn), lambda i, j: (i, j))],
        out_specs=pl.BlockSpec((tm, tn), lambda i, j: (i, j)),
        compiler_params=_cparams(("parallel", "parallel"), 48),
        name="outproj",
    )(a, w, r)


def _memkv_kernel(h_ref, wk_ref, wv_ref, g_ref, kf_o, kb_o, vf_o, vb_o):
    h = h_ref[...]
    k = _headnorm(jnp.dot(h, wk_ref[...], preferred_element_type=F32), g_ref[...])
    kf_o[...] = k
    kb_o[...] = k.astype(BF16)
    v = jnp.dot(h, wv_ref[...], preferred_element_type=F32)
    vf_o[...] = v
    vb_o[...] = v.astype(BF16)


def _mem_kv(mem_nb, w_xkv_bf, xk_g):
    t = mem_nb.shape[0]
    f32o = jax.ShapeDtypeStruct((t, X_WIDTH), F32)
    bfo = jax.ShapeDtypeStruct((t, X_WIDTH), BF16)
    full = lambda s: pl.BlockSpec(s, lambda i: (0,) * len(s))
    return pl.pallas_call(
        _memkv_kernel,
        out_shape=(f32o, bfo, f32o, bfo),
        grid=(1,),
        in_specs=[full((t, D_MODEL)),
                  pl.BlockSpec((D_MODEL, X_WIDTH), lambda i: (0, 0)),
                  pl.BlockSpec((D_MODEL, X_WIDTH), lambda i: (0, 1)),
                  full((1, X_HEAD_DIM))],
        out_specs=(full((t, X_WIDTH)),) * 4,
        compiler_params=_cparams(("arbitrary",), 32),
        name="mem_kv",
    )(mem_nb, w_xkv_bf, w_xkv_bf, xk_g.reshape(1, X_HEAD_DIM))


def _xattn_head(qh, kh, vh):
    s = lax.dot_general(qh, kh, (((1,), (1,)), ((), ())), preferred_element_type=F32)
    e = jnp.exp(s - jnp.max(s, axis=1, keepdims=True))
    p = e * (1.0 / jnp.sum(e, axis=1, keepdims=True))
    return jnp.dot(p.astype(BF16), vh, preferred_element_type=F32)


def _xattn_q(x, g_ref, wq_ref, qg_ref):
    hn = _rms(x, g_ref[...]).astype(BF16)
    q = jnp.dot(hn, wq_ref[...], preferred_element_type=F32)
    return _headnorm(q, qg_ref[...], X_HEAD_DIM ** -0.5)


def _xattn_prompt_kernel(x_ref, g_ref, wq_ref, qg_ref, mk_ref, mv_ref, wo_ref, o_ref):
    x = x_ref[0]
    q = _xattn_q(x, g_ref, wq_ref, qg_ref).astype(BF16)
    mk = mk_ref[0]
    mv = mv_ref[0]
    heads = []
    for g in range(X_HEADS):
        sl = slice(g * X_HEAD_DIM, (g + 1) * X_HEAD_DIM)
        heads.append(_xattn_head(q[:, sl], mk[:, sl], mv[:, sl]))
    o = jnp.concatenate(heads, axis=1).astype(BF16)
    o_ref[0] = x + jnp.dot(o, wo_ref[...], preferred_element_type=F32)


def _xattn_prompt(x, g, w_xq_bf, xq_g, mk_b, mv_b, w_xo_bf, tm):
    b, s, _ = x.shape
    tile = pl.BlockSpec((1, tm, D_MODEL), lambda bi, i: (bi, i, 0))
    mem = pl.BlockSpec((1, N_MEM, X_WIDTH), lambda bi, i: (bi, 0, 0))
    const = lambda shp: pl.BlockSpec(shp, lambda bi, i: (0, 0))
    return pl.pallas_call(
        _xattn_prompt_kernel,
        out_shape=jax.ShapeDtypeStruct((b, s, D_MODEL), F32),
        grid=(b, s // tm),
        in_specs=[tile, const((1, D_MODEL)), const((D_MODEL, X_WIDTH)), const((1, X_HEAD_DIM)),
                  mem, mem, const((X_WIDTH, D_MODEL))],
        out_specs=tile,
        compiler_params=_cparams(("parallel", "parallel"), 48),
        name="xattn_prompt",
    )(x, g.reshape(1, D_MODEL), w_xq_bf, xq_g.reshape(1, X_HEAD_DIM), mk_b, mv_b, w_xo_bf)


def _xattn_sample_kernel(x_ref, g_ref, wq_ref, qg_ref, mk_ref, mv_ref, mask_ref, wo_ref, o_ref, *, ts):
    x = x_ref[...]
    q = _xattn_q(x, g_ref, wq_ref, qg_ref)
    mask = mask_ref[...]
    nt = (((1,), (1,)), ((), ()))
    per_batch = []
    for bi in range(mk_ref.shape[0]):
        rows = slice(bi * ts, (bi + 1) * ts)
        qs = jnp.concatenate([q[rows, g * X_HEAD_DIM:(g + 1) * X_HEAD_DIM] for g in range(X_HEADS)], axis=0)
        s = lax.dot_general(qs.astype(BF16), mk_ref[bi].astype(BF16), nt, preferred_element_type=F32) + mask
        e = jnp.exp(s - jnp.max(s, axis=1, keepdims=True))
        o = jnp.dot(e.astype(BF16), mv_ref[bi].astype(BF16), preferred_element_type=F32)
        o = o * (1.0 / jnp.sum(e, axis=1, keepdims=True))
        per_batch.append(jnp.concatenate([o[g * ts:(g + 1) * ts] for g in range(X_HEADS)], axis=1))
    a = jnp.concatenate(per_batch, axis=0).astype(BF16)
    o_ref[...] = x + jnp.dot(a, wo_ref[...], preferred_element_type=F32)


def _xattn_sample(x2d, g, w_xq_bf, xq_g, cache_mk, cache_mv, w_xo_bf, ts, gb):
    t = x2d.shape[0]
    db = cache_mk.shape[0]
    mk = cache_mk.reshape(db, N_MEM * X_HEADS, X_HEAD_DIM)
    mv = cache_mv.reshape(db, N_MEM * X_HEADS, X_HEAD_DIM)
    tile = pl.BlockSpec((gb * ts, D_MODEL), lambda i: (i, 0))
    mem = pl.BlockSpec((gb, N_MEM * X_HEADS, X_HEAD_DIM), lambda i: (i, 0, 0))
    const = lambda shp: pl.BlockSpec(shp, lambda i: (0, 0))
    row_head = jnp.arange(X_HEADS * ts) // ts
    lane_head = jnp.arange(N_MEM * X_HEADS) % X_HEADS
    mask = jnp.where(row_head[:, None] == lane_head[None, :], 0.0, NEG).astype(F32)
    return pl.pallas_call(
        functools.partial(_xattn_sample_kernel, ts=ts),
        out_shape=jax.ShapeDtypeStruct((t, D_MODEL), F32),
        grid=(db // gb,),
        in_specs=[tile, const((1, D_MODEL)), const((D_MODEL, X_WIDTH)), const((1, X_HEAD_DIM)),
                  mem, mem, const((X_HEADS * ts, N_MEM * X_HEADS)), const((X_WIDTH, D_MODEL))],
        out_specs=tile,
        compiler_params=_cparams(("parallel",), 48),
        name="xattn_sample",
    )(x2d, g.reshape(1, D_MODEL), w_xq_bf, xq_g.reshape(1, X_HEAD_DIM), mk, mv, mask, w_xo_bf)


def _first_argmax(v, lane):
    vmax = jnp.max(v, axis=1, keepdims=True)
    idx = jnp.min(jnp.where(v == vmax, lane, LANES), axis=1, keepdims=True)
    return vmax, idx


def _router_kernel(xp_ref, xs_ref, g_ref, wr_ref, br_ref, h_o, eid_o, w_o, *, n_prompt_tiles):
    i = pl.program_id(0)

    def body(x_ref):
        h = _rms(x_ref[...], g_ref[...])
        h_o[...] = h
        w = wr_ref[...]
        w_hi = w.astype(BF16)
        w_lo = (w - w_hi.astype(F32)).astype(BF16)
        h_hi = h.astype(BF16)
        h_lo = (h - h_hi.astype(F32)).astype(BF16)
        mm = functools.partial(jnp.dot, preferred_element_type=F32)
        logits = mm(h_hi, w_hi) + (mm(h_lo, w_hi) + mm(h_hi, w_lo)) + br_ref[...]
        lane = lax.broadcasted_iota(jnp.int32, logits.shape, 1)
        ninf = -jnp.inf
        gl = jnp.where(lane < N_GROUPS, logits, ninf)
        gmax, gidx = _first_argmax(gl, lane)
        g_w = 1.0 / jnp.sum(jnp.exp(gl - gmax), axis=1, keepdims=True)
        lo = N_GROUPS + gidx * EXPERTS_PER_GROUP
        el = jnp.where((lane >= lo) & (lane < lo + EXPERTS_PER_GROUP), logits, ninf)
        v1, i1 = _first_argmax(el, lane)
        v2, i2 = _first_argmax(jnp.where(lane == i1, ninf, el), lane)
        e2 = jnp.exp(v2 - v1)
        w1 = g_w / (1.0 + e2)
        w2 = g_w * e2 / (1.0 + e2)
        eid_o[...] = jnp.where(lane == 0, i1 - N_GROUPS, jnp.where(lane == 1, i2 - N_GROUPS, 0))
        w_o[...] = jnp.where(lane == 0, w1, jnp.where(lane == 1, w2, 0.0))

    @pl.when(i < n_prompt_tiles)
    def _():
        body(xp_ref)

    @pl.when(i >= n_prompt_tiles)
    def _():
        body(xs_ref)


def _router(xp2d, xs2d, g, w_r, b_r, tm):
    tp, ts = xp2d.shape[0], xs2d.shape[0]
    npt, nst = tp // tm, ts // tm
    t = tp + ts
    tile = pl.BlockSpec((tm, D_MODEL), lambda i: (i, 0))
    const = lambda shp: pl.BlockSpec(shp, lambda i: (0, 0))
    return pl.pallas_call(
        functools.partial(_router_kernel, n_prompt_tiles=npt),
        out_shape=(jax.ShapeDtypeStruct((t, D_MODEL), F32),
                   jax.ShapeDtypeStruct((t, LANES), jnp.int32),
                   jax.ShapeDtypeStruct((t, LANES), F32)),
        grid=(npt + nst,),
        in_specs=[pl.BlockSpec((tm, D_MODEL), lambda i: (jnp.minimum(i, npt - 1), 0)),
                  pl.BlockSpec((tm, D_MODEL), lambda i: (jnp.maximum(i - npt, 0), 0)),
                  const((1, D_MODEL)), const((D_MODEL, LANES)), const((1, LANES))],
        out_specs=(tile, pl.BlockSpec((tm, LANES), lambda i: (i, 0)), pl.BlockSpec((tm, LANES), lambda i: (i, 0))),
        compiler_params=_cparams(("parallel",), 40),
        name="moe_router",
    )(xp2d, xs2d, g.reshape(1, D_MODEL), w_r, b_r)


def _dispatch_kernel(dest_ref, zflag_ref, h_ref, rows_hbm, zero_scr, sem, *, tmd, tm, n_tiles):
    i = pl.program_id(0)

    def zero_tile(t, start):
        @pl.when(zflag_ref[t] != 0)
        def _():
            cp = pltpu.make_async_copy(zero_scr, rows_hbm.at[pl.ds(pl.multiple_of(t * tm, tm), tm)], sem.at[2])
            if start:
                cp.start()
            else:
                cp.wait()

    @pl.when(i == 0)
    def _():
        zero_scr[...] = jnp.zeros(zero_scr.shape, zero_scr.dtype)
        lax.fori_loop(0, n_tiles, lambda t, c: (zero_tile(t, True), c)[1], 0)
        lax.fori_loop(0, n_tiles, lambda t, c: (zero_tile(t, False), c)[1], 0)

    base = 2 * i * tmd

    def row_copy(r, slot):
        return pltpu.make_async_copy(h_ref.at[pl.ds(r, 1)], rows_hbm.at[pl.ds(dest_ref[base + 2 * r + slot], 1)],
                                     sem.at[slot])

    def start(r, carry):
        row_copy(r, 0).start(priority=0)
        row_copy(r, 1).start(priority=1)
        return carry

    lax.fori_loop(0, tmd, start, 0, unroll=8)
    for slot in range(2):
        pltpu.make_async_copy(h_ref, rows_hbm.at[pl.ds(0, tmd)], sem.at[slot]).wait()


def _dispatch_rows(hf, dest_flat, zflag, n_tiles, tm, tmd):
    t, d = hf.shape
    grid_spec = pltpu.PrefetchScalarGridSpec(
        num_scalar_prefetch=2,
        grid=(t // tmd,),
        in_specs=[pl.BlockSpec((tmd, d), lambda i, dest, zf: (i, 0))],
        out_specs=pl.BlockSpec(memory_space=pl.ANY),
        scratch_shapes=[pltpu.VMEM((tm, d), hf.dtype), pltpu.SemaphoreType.DMA((3,))],
    )
    return pl.pallas_call(
        functools.partial(_dispatch_kernel, tmd=tmd, tm=tm, n_tiles=n_tiles),
        out_shape=jax.ShapeDtypeStruct((n_tiles * tm, d), hf.dtype),
        grid_spec=grid_spec,
        compiler_params=_cparams(("arbitrary",), 32),
        name="moe_dispatch",
    )(dest_flat, zflag, hf)


def _ffn_kernel(te_ref, slot_ref, fetch_ref, e1_ref, na_ref, x_ref, wgu_hbm, wd_hbm, o_ref,
                wgu_land, wd_land, wgu_scr, wd_scr, sem):
    t = pl.program_id(0)

    def weight_copies(e, slot):
        return (pltpu.make_async_copy(wgu_hbm.at[e], wgu_land.at[slot], sem.at[0, slot]),
                pltpu.make_async_copy(wd_hbm.at[e], wd_land.at[slot], sem.at[1, slot]))

    @pl.when(t < na_ref[0])
    def _():
        e = te_ref[t]
        slot = slot_ref[t]

        @pl.when(t == 0)
        def _():
            for cp in weight_copies(e, 0):
                cp.start()

            @pl.when(e1_ref[0] >= 0)
            def _():
                for cp in weight_copies(e1_ref[0], 1):
                    cp.start()

        @pl.when((t == 0) | (e != te_ref[jnp.maximum(t - 1, 0)]))
        def _():
            for cp in weight_copies(e, slot):
                cp.wait()
            wgu_scr[...] = wgu_land[slot].astype(BF16)
            wd_scr[...] = wd_land[slot].astype(BF16)

            @pl.when(fetch_ref[t] >= 0)
            def _():
                for cp in weight_copies(fetch_ref[t], slot):
                    cp.start()

        gu = jnp.dot(x_ref[...].astype(BF16), wgu_scr[...], preferred_element_type=F32)
        gate = gu[:, :D_FF]
        act = (gate * jax.nn.sigmoid(gate)) * gu[:, D_FF:]
        o_ref[...] = jnp.dot(act.astype(BF16), wd_scr[...], preferred_element_type=F32)

    @pl.when(t >= na_ref[0])
    def _():
        o_ref[...] = jnp.zeros(o_ref.shape, o_ref.dtype)


def _expert_ffn(xs, tile_e, slot_tab, fetch_tab, e1, n_active, w_gu, w_down, tm):
    n_tiles = xs.shape[0] // tm
    grid_spec = pltpu.PrefetchScalarGridSpec(
        num_scalar_prefetch=5,
        grid=(n_tiles,),
        in_specs=[pl.BlockSpec((tm, D_MODEL), lambda t, te, sl, fe, e1, na: (jnp.minimum(t, na[0] - 1), 0)),
                  pl.BlockSpec(memory_space=pl.ANY), pl.BlockSpec(memory_space=pl.ANY)],
        out_specs=pl.BlockSpec((tm, D_MODEL), lambda t, te, sl, fe, e1, na: (t, 0)),
        scratch_shapes=[pltpu.VMEM((2, D_MODEL, 2 * D_FF), F32), pltpu.VMEM((2, D_FF, D_MODEL), F32),
                        pltpu.VMEM((D_MODEL, 2 * D_FF), BF16), pltpu.VMEM((D_FF, D_MODEL), BF16),
                        pltpu.SemaphoreType.DMA((2, 2))],
    )
    return pl.pallas_call(
        _ffn_kernel,
        out_shape=jax.ShapeDtypeStruct((n_tiles * tm, D_MODEL), F32),
        grid_spec=grid_spec,
        compiler_params=_cparams(("arbitrary",), 48),
        name="moe_ffn",
    )(tile_e, slot_tab, fetch_tab, e1, n_active, xs, w_gu, w_down)


def _combine_kernel(dest_ref, x_ref, w_ref, ys_hbm, o_ref, buf_a, buf_b, sem, *, tm, base_tok):
    base = 2 * (base_tok + pl.program_id(0) * tm)
    bufs = (buf_a, buf_b)

    def start(r, carry):
        for slot in range(2):
            pltpu.make_async_copy(ys_hbm.at[pl.ds(dest_ref[base + 2 * r + slot], 1)], bufs[slot].at[pl.ds(r, 1)],
                                  sem.at[slot]).start(priority=slot)
        return carry

    lax.fori_loop(0, tm, start, 0, unroll=8)
    for slot in range(2):
        pltpu.make_async_copy(ys_hbm.at[pl.ds(0, tm)], bufs[slot], sem.at[slot]).wait()
    w = w_ref[...]
    o_ref[...] = x_ref[...] + (w[:, 0:1] * buf_a[...] + w[:, 1:2] * buf_b[...])


def _combine(x2d, wts, ys, dest_flat, base_tok, tm):
    t, d = x2d.shape
    base_blk = base_tok // tm
    grid_spec = pltpu.PrefetchScalarGridSpec(
        num_scalar_prefetch=1,
        grid=(t // tm,),
        in_specs=[pl.BlockSpec((tm, d), lambda i, dest: (i, 0)),
                  pl.BlockSpec((tm, LANES), lambda i, dest: (i + base_blk, 0)),
                  pl.BlockSpec(memory_space=pl.ANY)],
        out_specs=pl.BlockSpec((tm, d), lambda i, dest: (i, 0)),
        scratch_shapes=[pltpu.VMEM((tm, d), F32), pltpu.VMEM((tm, d), F32), pltpu.SemaphoreType.DMA((2,))],
    )
    return pl.pallas_call(
        functools.partial(_combine_kernel, tm=tm, base_tok=base_tok),
        out_shape=jax.ShapeDtypeStruct((t, d), F32),
        grid_spec=grid_spec,
        compiler_params=_cparams(("arbitrary",), 32),
        name="moe_combine",
    )(dest_flat, x2d, wts, ys)


def _route_plan(eid, tm, n_tiles):
    flat_e = eid.reshape(-1)
    onehot = (flat_e[:, None] == jnp.arange(N_EXPERTS, dtype=jnp.int32)[None, :]).astype(jnp.int32)
    csum = jnp.cumsum(onehot, axis=0)
    rank = jnp.sum(onehot * csum, axis=1) - 1
    counts = csum[-1]
    padded = ((counts + tm - 1) // tm) * tm
    ends = jnp.cumsum(padded)
    dest = (jnp.sum(onehot * (ends - padded)[None, :], axis=1) + rank).astype(jnp.int32)
    n_active = (ends[-1] // tm).astype(jnp.int32)
    tile = jnp.arange(n_tiles, dtype=jnp.int32)
    active = tile < n_active
    tile_e = jnp.sum((ends[None, :] <= (tile * tm)[:, None]).astype(jnp.int32), axis=1)
    last_e = jnp.sum(jnp.where(tile == n_active - 1, tile_e, 0))
    tile_e = jnp.where(active, tile_e, last_e).astype(jnp.int32)
    has_pad = jnp.any((ends[None, :] == ((tile + 1) * tm)[:, None]) & (padded != counts)[None, :], axis=1)
    zflag = (has_pad | ~active).astype(jnp.int32)
    e_ids = jnp.arange(N_EXPERTS, dtype=jnp.int32)
    present = padded > 0

    def next_present(after):
        return jnp.min(jnp.where(present[None, :] & (e_ids[None, :] > after[:, None]), e_ids[None, :], N_EXPERTS),
                       axis=1)

    run_idx = jnp.sum((present[None, :] & (e_ids[None, :] < tile_e[:, None])).astype(jnp.int32), axis=1)
    slot_tab = (run_idx % 2).astype(jnp.int32)
    n1 = next_present(tile_e)
    n2 = next_present(n1)
    fetch_tab = jnp.where(n2 < N_EXPERTS, n2, -1).astype(jnp.int32)
    e1 = jnp.where(n1[0] < N_EXPERTS, n1[0], -1).astype(jnp.int32).reshape(1)
    return dest, tile_e, slot_tab, fetch_tab, e1, n_active.reshape(1), zflag


def kernel(x_prompt, x_sample, cache_attn_k, cache_attn_v, cache_conv, cache_mem_k, cache_mem_v, page_table, mem_prompt, norm_mix_g, w_in, b_gate, q_norm_g, k_norm_g, lambda_q1, lambda_k1, lambda_q2, lambda_k2, subln_g, conv_w, w_out, norm_mem_g, norm_x_g, w_xq, w_xkv, xq_norm_g, xk_norm_g, w_xo, norm_ffn_g, w_group, b_group, w_expert_router, b_expert_router, w_gate_up, w_down):
    depth = w_in.shape[0]
    bp, sp, _ = x_prompt.shape
    bs, ts, _ = x_sample.shape
    tp, tsn = bp * sp, bs * ts
    slopes = jnp.exp2(-8.0 * jnp.arange(1, N_HEADS + 1, dtype=F32) / N_HEADS) * LOG2E
    xp = x_prompt.reshape(tp, D_MODEL)
    xs = x_sample.reshape(tsn, D_MODEL)
    moe_tm = 256
    moe_tiles = (tp + tsn) * 2 // moe_tm + N_EXPERTS
    outs = [[] for _ in range(8)]
    for l in range(depth):
        lam_init = 0.8 - 0.6 * math.exp(-0.3 * l)
        lam = (jnp.exp(jnp.sum(lambda_q1[l] * lambda_k1[l])) - jnp.exp(jnp.sum(lambda_q2[l] * lambda_k2[l]))
               + lam_init).reshape(1).astype(F32)
        out_scale = 1.0 - lam_init
        w_in_bf = w_in[l].astype(BF16)
        w_out_bf = w_out[l].astype(BF16)
        w_xq_bf = w_xq[l].astype(BF16)
        w_xkv_bf = w_xkv[l].astype(BF16)
        w_xo_bf = w_xo[l].astype(BF16)

        hb = _rmsnorm_cast(xp, norm_mix_g[l], 512)
        q, kf, kb, vf, vb, z, b, ga, gb = _inproj(hb, w_in_bf, q_norm_g[l], k_norm_g[l], b_gate[l], BF16, BF16,
                                                  1024, 256)
        o = _prompt_attn(q.reshape(bp, sp, D_MODEL), kb.reshape(bp, sp, D_MODEL), vb.reshape(bp, sp, D_MODEL),
                         lam, slopes, subln_g[l], out_scale, 512, 4)
        xp = _mix_outproj_prompt(o.reshape(tp, D_MODEL), ga, gb, b, z, conv_w[l], w_out_bf, xp, sp, 256, D_MODEL)
        outs[0].append(kf.reshape(bp, sp, N_HEADS, HEAD_W))
        outs[1].append(vf.reshape(bp, sp, N_HEADS, V_DIM))
        outs[2].append(z.reshape(bp, sp, D_MODEL)[:, sp - 2:, :])

        hb = _rmsnorm_cast(xs, norm_mix_g[l], 512)
        q, kf, kb, vf, vb, z, b, ga, gb = _inproj(hb, w_in_bf, q_norm_g[l], k_norm_g[l], b_gate[l], F32, F32,
                                                  1024, 256)
        shp = (bs, ts, D_MODEL)
        o = _sample_attn(q.reshape(shp), kf.reshape(shp), vf.reshape(shp), cache_attn_k[l], cache_attn_v[l],
                         page_table, lam, slopes, subln_g[l], out_scale, 4, 3)
        z3 = z.reshape(shp)
        merged = _mix_sample(o, ga.reshape(shp), gb.reshape(shp), b.reshape(shp), z3, cache_conv[l], conv_w[l], 32)
        xs = _mm_res(merged.reshape(tsn, D_MODEL), w_out_bf, xs, 512, 1024)
        outs[5].append(kf.reshape(bs, ts, N_HEADS, HEAD_W))
        outs[6].append(vf.reshape(bs, ts, N_HEADS, V_DIM))
        outs[7].append(z3[:, ts - 2:, :])

        mem_nb = _rmsnorm_cast(mem_prompt.reshape(bp * N_MEM, D_MODEL), norm_mem_g[l], 512)
        mkf, mkb, mvf, mvb = _mem_kv(mem_nb, w_xkv_bf, xk_norm_g[l])
        xp = _xattn_prompt(xp.reshape(bp, sp, D_MODEL), norm_x_g[l], w_xq_bf, xq_norm_g[l],
                           mkb.reshape(bp, N_MEM, X_WIDTH), mvb.reshape(bp, N_MEM, X_WIDTH), w_xo_bf,
                           512).reshape(tp, D_MODEL)
        xs = _xattn_sample(xs, norm_x_g[l], w_xq_bf, xq_norm_g[l], cache_mem_k[l], cache_mem_v[l], w_xo_bf, ts, 8)
        outs[3].append(mkf.reshape(bp, N_MEM, X_HEADS, X_HEAD_DIM))
        outs[4].append(mvf.reshape(bp, N_MEM, X_HEADS, X_HEAD_DIM))

        w_r = jnp.zeros((D_MODEL, LANES), F32).at[:, :N_GROUPS].set(w_group[l])
        w_r = w_r.at[:, N_GROUPS:N_GROUPS + N_EXPERTS].set(w_expert_router[l])
        b_r = jnp.zeros((1, LANES), F32).at[0, :N_GROUPS].set(b_group[l])
        b_r = b_r.at[0, N_GROUPS:N_GROUPS + N_EXPERTS].set(b_expert_router[l])
        hf, eid, wts = _router(xp, xs, norm_ffn_g[l], w_r, b_r, 512)
        dest, tile_e, slot_tab, fetch_tab, e1, n_active, zflag = _route_plan(eid[:, :2], moe_tm, moe_tiles)
        rows = _dispatch_rows(hf, dest, zflag, moe_tiles, moe_tm, 256)
        ys = _expert_ffn(rows, tile_e, slot_tab, fetch_tab, e1, n_active, w_gate_up[l], w_down[l], moe_tm)
        xp = _combine(xp, wts, ys, dest, 0, moe_tm)
        xs = _combine(xs, wts, ys, dest, tp, moe_tm)
    stack = lambda i: jnp.stack(outs[i])
    return (xp.reshape(bp, sp, D_MODEL), xs.reshape(bs, ts, D_MODEL), stack(0), stack(1), stack(2), stack(3),
            stack(4), stack(5), stack(6), stack(7))
```

```python
import functools
import math

import jax
import jax.numpy as jnp
from jax import lax
from jax.experimental import pallas as pl
from jax.experimental.pallas import tpu as pltpu

F32 = jnp.float32
BF16 = jnp.bfloat16

D_MODEL = 2048
N_HEADS = 8
HEAD_DIM = 128
V_DIM = 256
HEAD_W = 2 * HEAD_DIM
PAGE = 128
X_HEADS = 4
X_HEAD_DIM = 128
X_WIDTH = X_HEADS * X_HEAD_DIM
N_MEM = 256
N_GROUPS = 4
EXPERTS_PER_GROUP = 8
N_EXPERTS = N_GROUPS * EXPERTS_PER_GROUP
D_FF = 512
EPS = 1e-6
NEG = -1e30
LANES = 128
MIB = 1024 * 1024
LOG2E = math.log2(math.e)


def _cparams(sem, vmem_mib):
    return pltpu.CompilerParams(dimension_semantics=sem, vmem_limit_bytes=vmem_mib * MIB)


def _rms(x, g):
    ms = jnp.mean(x * x, axis=-1, keepdims=True)
    return x * lax.rsqrt(ms + EPS) * g


def _headnorm(acc, g, scale=None):
    outs = []
    for c in range(acc.shape[1] // HEAD_DIM):
        y = _rms(acc[:, c * HEAD_DIM:(c + 1) * HEAD_DIM], g)
        outs.append(y if scale is None else y * scale)
    return jnp.concatenate(outs, axis=1)


def _norm_kernel(x_ref, g_ref, o_ref):
    o_ref[...] = _rms(x_ref[...], g_ref[...]).astype(o_ref.dtype)


def _rmsnorm_cast(x2d, g, tm):
    t, d = x2d.shape
    return pl.pallas_call(
        _norm_kernel,
        out_shape=jax.ShapeDtypeStruct((t, d), BF16),
        grid=(t // tm,),
        in_specs=[pl.BlockSpec((tm, d), lambda i: (i, 0)),
                  pl.BlockSpec((1, d), lambda i: (0, 0))],
        out_specs=pl.BlockSpec((tm, d), lambda i: (i, 0)),
        compiler_params=_cparams(("parallel",), 32),
        name="rmsnorm_cast",
    )(x2d, g.reshape(1, d))


def _inproj_kernel(h_ref, wq, wk, wv, wc, wb, wu, wga, wgb, qg_ref, kg_ref, bga_ref, bgb_ref,
                   q_o, kf_o, kb_o, vf_o, vb_o, z_o, b_o, ga_o, gb_o, *, q_scale):
    h = h_ref[...]

    def mm(w):
        return jnp.dot(h, w[...], preferred_element_type=F32)

    q_o[...] = _headnorm(mm(wq), qg_ref[...], q_scale).astype(q_o.dtype)
    k = _headnorm(mm(wk), kg_ref[...])
    kf_o[...] = k
    kb_o[...] = k.astype(BF16)
    v = mm(wv)
    vf_o[...] = v
    vb_o[...] = v.astype(BF16)
    z_o[...] = mm(wc) * mm(wu)
    b_o[...] = mm(wb).astype(b_o.dtype)
    ga_o[...] = jax.nn.sigmoid(mm(wga) + bga_ref[...]).astype(ga_o.dtype)
    gb_o[...] = jax.nn.sigmoid(mm(wgb) + bgb_ref[...]).astype(gb_o.dtype)


def _inproj(hb, w_in_bf, q_g, k_g, b_gate, q_dtype, aux_dtype, tm, tn):
    t = hb.shape[0]
    nb = D_MODEL // tn
    region = {"q": 0, "k": 1, "v": 2, "c": 3, "b": 4, "u": 5, "ga": 6, "gb": 7}

    def wspec(r):
        return pl.BlockSpec((D_MODEL, tn), lambda i, j, r=r: (0, r * nb + j))

    tile = pl.BlockSpec((tm, tn), lambda i, j: (i, j))
    vec = pl.BlockSpec((1, HEAD_DIM), lambda i, j: (0, 0))
    bias = pl.BlockSpec((1, tn), lambda i, j: (0, j))
    f32o = jax.ShapeDtypeStruct((t, D_MODEL), F32)
    bfo = jax.ShapeDtypeStruct((t, D_MODEL), BF16)
    return pl.pallas_call(
        functools.partial(_inproj_kernel, q_scale=HEAD_DIM ** -0.5 * LOG2E),
        out_shape=(jax.ShapeDtypeStruct((t, D_MODEL), q_dtype), f32o, bfo, f32o, bfo, f32o)
        + (jax.ShapeDtypeStruct((t, D_MODEL), aux_dtype),) * 3,
        grid=(t // tm, nb),
        in_specs=[pl.BlockSpec((tm, D_MODEL), lambda i, j: (i, 0))]
        + [wspec(region[n]) for n in ("q", "k", "v", "c", "b", "u", "ga", "gb")]
        + [vec, vec, bias, bias],
        out_specs=(tile,) * 9,
        compiler_params=_cparams(("parallel", "arbitrary"), 48),
        name="inproj",
    )(hb, *([w_in_bf] * 8), q_g.reshape(1, HEAD_DIM), k_g.reshape(1, HEAD_DIM),
      b_gate[0:1], b_gate[1:2])


def _lane_tiles(x):
    return [x[:, c * LANES:(c + 1) * LANES] for c in range(x.shape[1] // LANES)]


def _lane_repeat(x, n):
    return x if n == 1 else jnp.concatenate([x] * n, axis=1)


def _online_update(s_list, v_list, m_prev, l_prev, acc_prev, c_list=None):
    if c_list is None:
        c_list = [None] * len(s_list)
    tops = []
    for s, c in zip(s_list, c_list):
        top = functools.reduce(jnp.maximum, _lane_tiles(s))
        tops.append(top if c is None else top + c)
    m_cur = jnp.max(functools.reduce(jnp.maximum, tops), axis=1, keepdims=True)
    m_next = jnp.maximum(m_prev, m_cur)
    alpha = jnp.exp2(m_prev - m_next)
    l_next = alpha * l_prev
    pv = None
    for s, v, c in zip(s_list, v_list, c_list):
        shift = m_next if c is None else m_next - c
        p = jnp.exp2(s - _lane_repeat(shift, s.shape[1] // LANES))
        l_next = l_next + functools.reduce(jnp.add, _lane_tiles(p))
        d = v(p.astype(BF16)) if callable(v) else jnp.dot(p.astype(BF16), v, preferred_element_type=F32)
        pv = d if pv is None else pv + d
    acc_next = acc_prev * _lane_repeat(alpha, V_DIM // LANES) + pv
    return m_next, l_next, acc_next


def _diff_out(a1, l1, a2, l2, lam, g, out_scale):
    r1 = 1.0 / jnp.sum(l1, axis=1, keepdims=True)
    r2 = 1.0 / jnp.sum(l2, axis=1, keepdims=True)
    o = a1 * r1 - lam * (a2 * r2)
    return _rms(o, g) * out_scale


def _init_softmax_state(m_scr, l_scr, acc_scr):
    m_scr[...] = jnp.full(m_scr.shape, NEG, F32)
    l_scr[...] = jnp.zeros(l_scr.shape, F32)
    acc_scr[...] = jnp.zeros(acc_scr.shape, F32)


def _pattn_kernel(qi_ref, ki_ref, lam_ref, slopes_ref, q_ref, k_ref, v_ref, g_ref, o_ref, m_scr, l_scr, acc_scr,
                  *, t, hp, out_scale):
    h = pl.program_id(1)
    step = pl.program_id(2)
    qi = qi_ref[step]
    ki = ki_ref[step]

    @pl.when(ki == 0)
    def _():
        _init_softmax_state(m_scr, l_scr, acc_scr)

    def body(masked):
        kpos = ((ki - qi) * t + lax.broadcasted_iota(jnp.int32, (1, t), 1)).astype(F32)
        if masked:
            keep = (lax.broadcasted_iota(jnp.int32, (t, t), 1)
                    <= lax.broadcasted_iota(jnp.int32, (t, t), 0))
        state = []
        for i in range(hp):
            bias = slopes_ref[h * hp + i] * kpos
            v = v_ref[0, :, i * V_DIM:(i + 1) * V_DIM]
            for m in range(2):
                cols = slice(i * HEAD_W + m * HEAD_DIM, i * HEAD_W + (m + 1) * HEAD_DIM)
                s = lax.dot_general(q_ref[0, :, cols], k_ref[0, :, cols], (((1,), (1,)), ((), ())),
                                    preferred_element_type=F32)
                s = s + bias
                if masked:
                    s = jnp.where(keep, s, NEG)
                idx = 2 * i + m
                state.append(_online_update([s], [v], m_scr[idx], l_scr[idx], acc_scr[idx]))
        return state

    @pl.when(ki < qi)
    def _():
        for idx, (m_next, l_next, acc_next) in enumerate(body(False)):
            m_scr[idx] = m_next
            l_scr[idx] = l_next
            acc_scr[idx] = acc_next

    @pl.when(ki == qi)
    def _():
        state = body(True)
        lam = lam_ref[0]
        g = g_ref[...]
        o_ref[0] = jnp.concatenate(
            [_diff_out(state[2 * i][2], state[2 * i][1], state[2 * i + 1][2], state[2 * i + 1][1], lam, g, out_scale)
             for i in range(hp)], axis=1).astype(o_ref.dtype)


def _prompt_attn(qb, kb, vb, lam, slopes2, subln_g, out_scale, t, hp):
    b, s, _ = qb.shape
    n = s // t
    pairs = [(qi, ki) for qi in range(n) for ki in range(qi + 1)]
    qi_tab = jnp.asarray([p[0] for p in pairs], jnp.int32)
    ki_tab = jnp.asarray([p[1] for p in pairs], jnp.int32)
    smem = pl.BlockSpec(memory_space=pltpu.SMEM)
    grid_spec = pltpu.PrefetchScalarGridSpec(
        num_scalar_prefetch=2,
        grid=(b, N_HEADS // hp, len(pairs)),
        in_specs=[smem, smem,
                  pl.BlockSpec((1, t, hp * HEAD_W), lambda bi, h, st, qt, kt: (bi, qt[st], h)),
                  pl.BlockSpec((1, t, hp * HEAD_W), lambda bi, h, st, qt, kt: (bi, kt[st], h)),
                  pl.BlockSpec((1, t, hp * V_DIM), lambda bi, h, st, qt, kt: (bi, kt[st], h)),
                  pl.BlockSpec((1, V_DIM), lambda bi, h, st, qt, kt: (0, 0))],
        out_specs=pl.BlockSpec((1, t, hp * V_DIM), lambda bi, h, st, qt, kt: (bi, qt[st], h)),
        scratch_shapes=[pltpu.VMEM((2 * hp, t, LANES), F32), pltpu.VMEM((2 * hp, t, LANES), F32),
                        pltpu.VMEM((2 * hp, t, V_DIM), F32)],
    )
    return pl.pallas_call(
        functools.partial(_pattn_kernel, t=t, hp=hp, out_scale=out_scale),
        out_shape=jax.ShapeDtypeStruct((b, s, D_MODEL), BF16),
        grid_spec=grid_spec,
        compiler_params=_cparams(("parallel", "parallel", "arbitrary"), 40),
        name="prompt_attn",
    )(qi_tab, ki_tab, lam, slopes2, qb, kb, vb, subln_g.reshape(1, V_DIM))


def _sattn_bias(slopes2, tq):
    rows = jnp.arange(2 * N_HEADS * tq)
    row_h = (rows // tq) % N_HEADS
    slope = slopes2[row_h][:, None]
    lanes = jnp.arange(PAGE * N_HEADS)
    page = jnp.where(row_h[:, None] == (lanes % N_HEADS)[None, :], slope * (lanes // N_HEADS)[None, :].astype(F32), NEG)
    tok = jnp.arange(LANES)
    new = jnp.where(tok[None, :] <= (rows % tq)[:, None], slope * tok[None, :].astype(F32), NEG)
    return page.astype(F32), new.astype(F32)


def _sattn_kernel(pt_ref, lam_ref, q_ref, kn_ref, vn_ref, pk_hbm, pv_hbm, bias_ref, bias_new_ref, slope_ref, g_ref,
                  o_ref, k_buf, v_buf, qbd_scr, qfull_scr, m_scr, l_scr, acc_scr, sem,
                  *, npg, n_steps, n_slots, past, out_scale):
    p = pl.program_id(1)
    tq = q_ref.shape[1]
    half = N_HEADS * tq

    g = pl.program_id(0) * n_steps + p
    total = pl.num_programs(0) * n_steps

    def page_copies(step):
        slot = step % n_slots
        cps = []
        for j in range(npg):
            page = pt_ref[step // n_steps, (step % n_steps) * npg + j]
            cps.append(pltpu.make_async_copy(pk_hbm.at[page], k_buf.at[slot, j], sem.at[0, slot]))
            cps.append(pltpu.make_async_copy(pv_hbm.at[page], v_buf.at[slot, j], sem.at[1, slot]))
        return cps

    @pl.when(g == 0)
    def _():
        for first in range(n_slots - 1):
            for cp in page_copies(jnp.int32(first)):
                cp.start()

    @pl.when(g + (n_slots - 1) < total)
    def _():
        for cp in page_copies(g + (n_slots - 1)):
            cp.start()

    for cp in page_copies(g):
        cp.wait()
    slot = g % n_slots

    @pl.when(p == 0)
    def _():
        _init_softmax_state(m_scr, l_scr, acc_scr)
        q = q_ref[0]
        zero = jnp.zeros((half, HEAD_DIM), F32)
        q1, q2 = [jnp.concatenate([q[:, h * HEAD_W + m * HEAD_DIM:h * HEAD_W + (m + 1) * HEAD_DIM]
                                   for h in range(N_HEADS)], axis=0) for m in range(2)]
        qbd_scr[...] = jnp.concatenate([jnp.concatenate([q1, zero], axis=1),
                                        jnp.concatenate([zero, q2], axis=1)], axis=0).astype(BF16)
        q_rep = jnp.concatenate([q] * N_HEADS, axis=0)
        lane = lax.broadcasted_iota(jnp.int32, q_rep.shape, 1)
        own_head = lane // HEAD_W == lax.broadcasted_iota(jnp.int32, q_rep.shape, 0) // tq
        qfull_scr[...] = jnp.concatenate(
            [jnp.where(own_head & ((lane // HEAD_DIM) % 2 == m), q_rep, 0.0) for m in range(2)],
            axis=0).astype(BF16)

    nt = (((1,), (1,)), ((), ()))

    def body(last):
        qbd = qbd_scr[...]
        bias = bias_ref[...]
        slope = slope_ref[...]
        s_list, v_list, c_list = [], [], []
        for j in range(npg):
            s = lax.dot_general(qbd, k_buf[slot, j].astype(BF16), nt, preferred_element_type=F32)
            s_list.append(s + bias)
            v_list.append(v_buf[slot, j].astype(BF16))
            c_list.append(slope * ((p * npg + j) * PAGE - past).astype(F32))
        if last:
            pad = jnp.zeros((LANES - tq, D_MODEL), F32)
            kn = jnp.concatenate([kn_ref[0], pad], axis=0).astype(BF16)
            vn = jnp.concatenate([vn_ref[0], pad], axis=0).astype(BF16)
            s = lax.dot_general(qfull_scr[...], kn, nt, preferred_element_type=F32)
            s_list.append(s + bias_new_ref[...])

            def new_values(pb):
                full = jnp.dot(pb, vn, preferred_element_type=F32)
                return jnp.concatenate([full[(m * N_HEADS + h) * tq:(m * N_HEADS + h + 1) * tq,
                                             h * V_DIM:(h + 1) * V_DIM]
                                        for m in range(2) for h in range(N_HEADS)], axis=0)

            v_list.append(new_values)
            c_list.append(None)
        return _online_update(s_list, v_list, m_scr[...], l_scr[...], acc_scr[...], c_list)

    @pl.when(p < n_steps - 1)
    def _():
        m_next, l_next, acc_next = body(False)
        m_scr[...] = m_next
        l_scr[...] = l_next
        acc_scr[...] = acc_next

    @pl.when(p == n_steps - 1)
    def _():
        _, l, acc = body(True)
        y = _diff_out(acc[:half], l[:half], acc[half:], l[half:], lam_ref[0], g_ref[...], out_scale)
        o_ref[0] = jnp.concatenate([y[h * tq:(h + 1) * tq] for h in range(N_HEADS)], axis=1)


def _sample_attn(qs, kn, vn, pool_k, pool_v, page_table, lam, slopes2, subln_g, out_scale, npg, n_slots):
    db, tq, _ = qs.shape
    n_pages = page_table.shape[1]
    n_steps = n_pages // npg
    n_phys = pool_k.shape[0]
    rows = 2 * N_HEADS * tq
    page_rows = PAGE * N_HEADS
    pk = pool_k.reshape(n_phys, page_rows, HEAD_W)
    pv = pool_v.reshape(n_phys, page_rows, V_DIM)
    bias, bias_new = _sattn_bias(slopes2, tq)
    slope_rep = jnp.broadcast_to(slopes2[(jnp.arange(rows) // tq) % N_HEADS][:, None], (rows, LANES))
    smem = pl.BlockSpec(memory_space=pltpu.SMEM)
    tok = pl.BlockSpec((1, tq, D_MODEL), lambda b, p, pt: (b, 0, 0))
    const = lambda shp: pl.BlockSpec(shp, lambda b, p, pt: (0, 0))

    hbm = pl.BlockSpec(memory_space=pl.ANY)
    grid_spec = pltpu.PrefetchScalarGridSpec(
        num_scalar_prefetch=1,
        grid=(db, n_steps),
        in_specs=[smem, tok, tok, tok, hbm, hbm,
                  const((rows, page_rows)), const((rows, LANES)), const((rows, LANES)), const((1, V_DIM))],
        out_specs=tok,
        scratch_shapes=[pltpu.VMEM((n_slots, npg, page_rows, HEAD_W), F32),
                        pltpu.VMEM((n_slots, npg, page_rows, V_DIM), F32),
                        pltpu.VMEM((rows, HEAD_W), BF16), pltpu.VMEM((rows, D_MODEL), BF16),
                        pltpu.VMEM((rows, LANES), F32), pltpu.VMEM((rows, LANES), F32),
                        pltpu.VMEM((rows, V_DIM), F32), pltpu.SemaphoreType.DMA((2, n_slots))],
    )
    return pl.pallas_call(
        functools.partial(_sattn_kernel, npg=npg, n_steps=n_steps, n_slots=n_slots, past=n_pages * PAGE,
                          out_scale=out_scale),
        out_shape=jax.ShapeDtypeStruct((db, tq, D_MODEL), F32),
        grid_spec=grid_spec,
        compiler_params=_cparams(("arbitrary", "arbitrary"), 48),
        name="sample_attn",
    )(page_table, lam, qs, kn, vn, pk, pv, bias, bias_new, slope_rep, subln_g.reshape(1, V_DIM))


def _conv_merge(o, ga, gb, b, z, z1, z2, cw):
    y = z2 * cw[0] + z1 * cw[1] + z * cw[2]
    return (ga.astype(F32) * o.astype(F32) + gb.astype(F32) * (b.astype(F32) * y)).astype(BF16)


def _mix_outproj_kernel(o_ref, ga_ref, gb_ref, b_ref, z_ref, zp_ref, cw_ref, w_ref, r_ref, out_ref, merged_scr,
                        *, tiles_per_seq):
    i = pl.program_id(0)

    @pl.when(pl.program_id(1) == 0)
    def _():
        z = z_ref[...]
        tm = z.shape[0]
        zp = jnp.where(i % tiles_per_seq == 0, 0.0, zp_ref[...])
        pm1 = zp[7:8, :]
        pm2 = zp[6:7, :]
        row = lax.broadcasted_iota(jnp.int32, (tm, 1), 0)
        z1 = jnp.where(row == 0, pm1, pltpu.roll(z, 1, 0))
        z2 = jnp.where(row == 0, pm2, jnp.where(row == 1, pm1, pltpu.roll(z, 2, 0)))
        cw = cw_ref[...]
        merged_scr[...] = _conv_merge(o_ref[...], ga_ref[...], gb_ref[...], b_ref[...], z, z1, z2,
                                      (cw[0:1], cw[1:2], cw[2:3]))

    out_ref[...] = r_ref[...] + jnp.dot(merged_scr[...], w_ref[...], preferred_element_type=F32)


def _mix_outproj_prompt(o, ga, gb, b, z, conv_w, w, r, seq, tm, tn):
    t = z.shape[0]
    n = w.shape[1]
    row = pl.BlockSpec((tm, D_MODEL), lambda i, j: (i, 0))
    prev = pl.BlockSpec((8, D_MODEL), lambda i, j: (jnp.maximum(i * (tm // 8) - 1, 0), 0))
    tile = pl.BlockSpec((tm, tn), lambda i, j: (i, j))
    return pl.pallas_call(
        functools.partial(_mix_outproj_kernel, tiles_per_seq=seq // tm),
        out_shape=jax.ShapeDtypeStruct((t, n), F32),
        grid=(t // tm, n // tn),
        in_specs=[row, row, row, row, row, prev, pl.BlockSpec((3, D_MODEL), lambda i, j: (0, 0)),
                  pl.BlockSpec((D_MODEL, tn), lambda i, j: (0, j)), tile],
        out_specs=tile,
        scratch_shapes=[pltpu.VMEM((tm, D_MODEL), BF16)],
        compiler_params=_cparams(("parallel", "arbitrary"), 48),
        name="mix_outproj",
    )(o, ga, gb, b, z, z, conv_w, w, r)


def _mix_sample_kernel(o_ref, ga_ref, gb_ref, b_ref, z_ref, cc_ref, cw_ref, out_ref):
    z = z_ref[...]
    cc = cc_ref[...]
    c0 = cc[:, 0:1, :]
    c1 = cc[:, 1:2, :]
    row = lax.broadcasted_iota(jnp.int32, (1, z.shape[1], 1), 1)
    z1 = jnp.where(row == 0, c1, pltpu.roll(z, 1, 1))
    z2 = jnp.where(row == 0, c0, jnp.where(row == 1, c1, pltpu.roll(z, 2, 1)))
    cw = cw_ref[...]
    out_ref[...] = _conv_merge(o_ref[...], ga_ref[...], gb_ref[...], b_ref[...], z, z1, z2,
                               (cw[0:1][None], cw[1:2][None], cw[2:3][None]))


def _mix_sample(o, ga, gb, b, z, cache_conv, conv_w, g):
    db, ts, _ = z.shape
    tile = pl.BlockSpec((g, ts, D_MODEL), lambda i: (i, 0, 0))
    return pl.pallas_call(
        _mix_sample_kernel,
        out_shape=jax.ShapeDtypeStruct((db, ts, D_MODEL), BF16),
        grid=(db // g,),
        in_specs=[tile, tile, tile, tile, tile,
                  pl.BlockSpec((g, cache_conv.shape[1], D_MODEL), lambda i: (i, 0, 0)),
                  pl.BlockSpec((3, D_MODEL), lambda i: (0, 0))],
        out_specs=tile,
        compiler_params=_cparams(("parallel",), 40),
        name="mix_sample",
    )(o, ga, gb, b, z, cache_conv, conv_w)


def _mm_res_kernel(a_ref, w_ref, r_ref, o_ref):
    o_ref[...] = r_ref[...] + jnp.dot(a_ref[...], w_ref[...], preferred_element_type=F32)


def _mm_res(a, w, r, tm, tn):
    t, k = a.shape
    n = w.shape[1]
    return pl.pallas_call(
        _mm_res_kernel,
        out_shape=jax.ShapeDtypeStruct((t, n), F32),
        grid=(t // tm, n // tn),
        in_specs=[pl.BlockSpec((tm, k), lambda i, j: (i, 0)),
                  pl.BlockSpec((k, tn), lambda i, j: (0, j)),
                  pl.BlockSpec((tm, tn), lambda i, j: (i, j))],
        out_specs=pl.BlockSpec((tm, tn), lambda i, j: (i, j)),
        compiler_params=_cparams(("parallel", "parallel"), 48),
        name="outproj",
    )(a, w, r)


def _memkv_kernel(h_ref, wk_ref, wv_ref, g_ref, kf_o, kb_o, vf_o, vb_o):
    h = h_ref[...]
    k = _headnorm(jnp.dot(h, wk_ref[...], preferred_element_type=F32), g_ref[...])
    kf_o[...] = k
    kb_o[...] = k.astype(BF16)
    v = jnp.dot(h, wv_ref[...], preferred_element_type=F32)
    vf_o[...] = v
    vb_o[...] = v.astype(BF16)


def _mem_kv(mem_nb, w_xkv_bf, xk_g):
    t = mem_nb.shape[0]
    f32o = jax.ShapeDtypeStruct((t, X_WIDTH), F32)
    bfo = jax.ShapeDtypeStruct((t, X_WIDTH), BF16)
    full = lambda s: pl.BlockSpec(s, lambda i: (0,) * len(s))
    return pl.pallas_call(
        _memkv_kernel,
        out_shape=(f32o, bfo, f32o, bfo),
        grid=(1,),
        in_specs=[full((t, D_MODEL)),
                  pl.BlockSpec((D_MODEL, X_WIDTH), lambda i: (0, 0)),
                  pl.BlockSpec((D_MODEL, X_WIDTH), lambda i: (0, 1)),
                  full((1, X_HEAD_DIM))],
        out_specs=(full((t, X_WIDTH)),) * 4,
        compiler_params=_cparams(("arbitrary",), 32),
        name="mem_kv",
    )(mem_nb, w_xkv_bf, w_xkv_bf, xk_g.reshape(1, X_HEAD_DIM))


def _xattn_head(qh, kh, vh):
    s = lax.dot_general(qh, kh, (((1,), (1,)), ((), ())), preferred_element_type=F32)
    e = jnp.exp(s - jnp.max(s, axis=1, keepdims=True))
    p = e * (1.0 / jnp.sum(e, axis=1, keepdims=True))
    return jnp.dot(p.astype(BF16), vh, preferred_element_type=F32)


def _xattn_q(x, g_ref, wq_ref, qg_ref):
    hn = _rms(x, g_ref[...]).astype(BF16)
    q = jnp.dot(hn, wq_ref[...], preferred_element_type=F32)
    return _headnorm(q, qg_ref[...], X_HEAD_DIM ** -0.5)


def _xattn_prompt_kernel(x_ref, g_ref, wq_ref, qg_ref, mk_ref, mv_ref, wo_ref, o_ref):
    x = x_ref[0]
    q = _xattn_q(x, g_ref, wq_ref, qg_ref).astype(BF16)
    mk = mk_ref[0]
    mv = mv_ref[0]
    heads = []
    for g in range(X_HEADS):
        sl = slice(g * X_HEAD_DIM, (g + 1) * X_HEAD_DIM)
        heads.append(_xattn_head(q[:, sl], mk[:, sl], mv[:, sl]))
    o = jnp.concatenate(heads, axis=1).astype(BF16)
    o_ref[0] = x + jnp.dot(o, wo_ref[...], preferred_element_type=F32)


def _xattn_prompt(x, g, w_xq_bf, xq_g, mk_b, mv_b, w_xo_bf, tm):
    b, s, _ = x.shape
    tile = pl.BlockSpec((1, tm, D_MODEL), lambda bi, i: (bi, i, 0))
    mem = pl.BlockSpec((1, N_MEM, X_WIDTH), lambda bi, i: (bi, 0, 0))
    const = lambda shp: pl.BlockSpec(shp, lambda bi, i: (0, 0))
    return pl.pallas_call(
        _xattn_prompt_kernel,
        out_shape=jax.ShapeDtypeStruct((b, s, D_MODEL), F32),
        grid=(b, s // tm),
        in_specs=[tile, const((1, D_MODEL)), const((D_MODEL, X_WIDTH)), const((1, X_HEAD_DIM)),
                  mem, mem, const((X_WIDTH, D_MODEL))],
        out_specs=tile,
        compiler_params=_cparams(("parallel", "parallel"), 48),
        name="xattn_prompt",
    )(x, g.reshape(1, D_MODEL), w_xq_bf, xq_g.reshape(1, X_HEAD_DIM), mk_b, mv_b, w_xo_bf)


def _xattn_sample_kernel(x_ref, g_ref, wq_ref, qg_ref, mk_ref, mv_ref, mask_ref, wo_ref, o_ref, *, ts):
    x = x_ref[...]
    q = _xattn_q(x, g_ref, wq_ref, qg_ref)
    mask = mask_ref[...]
    nt = (((1,), (1,)), ((), ()))
    per_batch = []
    for bi in range(mk_ref.shape[0]):
        rows = slice(bi * ts, (bi + 1) * ts)
        qs = jnp.concatenate([q[rows, g * X_HEAD_DIM:(g + 1) * X_HEAD_DIM] for g in range(X_HEADS)], axis=0)
        s = lax.dot_general(qs.astype(BF16), mk_ref[bi].astype(BF16), nt, preferred_element_type=F32) + mask
        e = jnp.exp(s - jnp.max(s, axis=1, keepdims=True))
        o = jnp.dot(e.astype(BF16), mv_ref[bi].astype(BF16), preferred_element_type=F32)
        o = o * (1.0 / jnp.sum(e, axis=1, keepdims=True))
        per_batch.append(jnp.concatenate([o[g * ts:(g + 1) * ts] for g in range(X_HEADS)], axis=1))
    a = jnp.concatenate(per_batch, axis=0).astype(BF16)
    o_ref[...] = x + jnp.dot(a, wo_ref[...], preferred_element_type=F32)


def _xattn_sample(x2d, g, w_xq_bf, xq_g, cache_mk, cache_mv, w_xo_bf, ts, gb):
    t = x2d.shape[0]
    db = cache_mk.shape[0]
    mk = cache_mk.reshape(db, N_MEM * X_HEADS, X_HEAD_DIM)
    mv = cache_mv.reshape(db, N_MEM * X_HEADS, X_HEAD_DIM)
    tile = pl.BlockSpec((gb * ts, D_MODEL), lambda i: (i, 0))
    mem = pl.BlockSpec((gb, N_MEM * X_HEADS, X_HEAD_DIM), lambda i: (i, 0, 0))
    const = lambda shp: pl.BlockSpec(shp, lambda i: (0, 0))
    row_head = jnp.arange(X_HEADS * ts) // ts
    lane_head = jnp.arange(N_MEM * X_HEADS) % X_HEADS
    mask = jnp.where(row_head[:, None] == lane_head[None, :], 0.0, NEG).astype(F32)
    return pl.pallas_call(
        functools.partial(_xattn_sample_kernel, ts=ts),
        out_shape=jax.ShapeDtypeStruct((t, D_MODEL), F32),
        grid=(db // gb,),
        in_specs=[tile, const((1, D_MODEL)), const((D_MODEL, X_WIDTH)), const((1, X_HEAD_DIM)),
                  mem, mem, const((X_HEADS * ts, N_MEM * X_HEADS)), const((X_WIDTH, D_MODEL))],
        out_specs=tile,
        compiler_params=_cparams(("parallel",), 48),
        name="xattn_sample",
    )(x2d, g.reshape(1, D_MODEL), w_xq_bf, xq_g.reshape(1, X_HEAD_DIM), mk, mv, mask, w_xo_bf)


def _first_argmax(v, lane):
    vmax = jnp.max(v, axis=1, keepdims=True)
    idx = jnp.min(jnp.where(v == vmax, lane, LANES), axis=1, keepdims=True)
    return vmax, idx


def _router_kernel(xp_ref, xs_ref, g_ref, wr_ref, br_ref, h_o, eid_o, w_o, *, n_prompt_tiles):
    i = pl.program_id(0)

    def body(x_ref):
        h = _rms(x_ref[...], g_ref[...])
        h_o[...] = h
        w = wr_ref[...]
        w_hi = w.astype(BF16)
        w_lo = (w - w_hi.astype(F32)).astype(BF16)
        h_hi = h.astype(BF16)
        h_lo = (h - h_hi.astype(F32)).astype(BF16)
        mm = functools.partial(jnp.dot, preferred_element_type=F32)
        logits = mm(h_hi, w_hi) + (mm(h_lo, w_hi) + mm(h_hi, w_lo)) + br_ref[...]
        lane = lax.broadcasted_iota(jnp.int32, logits.shape, 1)
        ninf = -jnp.inf
        gl = jnp.where(lane < N_GROUPS, logits, ninf)
        gmax, gidx = _first_argmax(gl, lane)
        g_w = 1.0 / jnp.sum(jnp.exp(gl - gmax), axis=1, keepdims=True)
        lo = N_GROUPS + gidx * EXPERTS_PER_GROUP
        el = jnp.where((lane >= lo) & (lane < lo + EXPERTS_PER_GROUP), logits, ninf)
        v1, i1 = _first_argmax(el, lane)
        v2, i2 = _first_argmax(jnp.where(lane == i1, ninf, el), lane)
        e2 = jnp.exp(v2 - v1)
        w1 = g_w / (1.0 + e2)
        w2 = g_w * e2 / (1.0 + e2)
        eid_o[...] = jnp.where(lane == 0, i1 - N_GROUPS, jnp.where(lane == 1, i2 - N_GROUPS, 0))
        w_o[...] = jnp.where(lane == 0, w1, jnp.where(lane == 1, w2, 0.0))

    @pl.when(i < n_prompt_tiles)
    def _():
        body(xp_ref)

    @pl.when(i >= n_prompt_tiles)
    def _():
        body(xs_ref)


def _router(xp2d, xs2d, g, w_r, b_r, tm):
    tp, ts = xp2d.shape[0], xs2d.shape[0]
    npt, nst = tp // tm, ts // tm
    t = tp + ts
    tile = pl.BlockSpec((tm, D_MODEL), lambda i: (i, 0))
    const = lambda shp: pl.BlockSpec(shp, lambda i: (0, 0))
    return pl.pallas_call(
        functools.partial(_router_kernel, n_prompt_tiles=npt),
        out_shape=(jax.ShapeDtypeStruct((t, D_MODEL), F32),
                   jax.ShapeDtypeStruct((t, LANES), jnp.int32),
                   jax.ShapeDtypeStruct((t, LANES), F32)),
        grid=(npt + nst,),
        in_specs=[pl.BlockSpec((tm, D_MODEL), lambda i: (jnp.minimum(i, npt - 1), 0)),
                  pl.BlockSpec((tm, D_MODEL), lambda i: (jnp.maximum(i - npt, 0), 0)),
                  const((1, D_MODEL)), const((D_MODEL, LANES)), const((1, LANES))],
        out_specs=(tile, pl.BlockSpec((tm, LANES), lambda i: (i, 0)), pl.BlockSpec((tm, LANES), lambda i: (i, 0))),
        compiler_params=_cparams(("parallel",), 40),
        name="moe_router",
    )(xp2d, xs2d, g.reshape(1, D_MODEL), w_r, b_r)


def _dispatch_kernel(dest_ref, zflag_ref, h_ref, rows_hbm, zero_scr, sem, *, tmd, tm, n_tiles):
    i = pl.program_id(0)

    def zero_tile(t, start):
        @pl.when(zflag_ref[t] != 0)
        def _():
            cp = pltpu.make_async_copy(zero_scr, rows_hbm.at[pl.ds(pl.multiple_of(t * tm, tm), tm)], sem.at[2])
            if start:
                cp.start()
            else:
                cp.wait()

    @pl.when(i == 0)
    def _():
        zero_scr[...] = jnp.zeros(zero_scr.shape, zero_scr.dtype)
        lax.fori_loop(0, n_tiles, lambda t, c: (zero_tile(t, True), c)[1], 0)
        lax.fori_loop(0, n_tiles, lambda t, c: (zero_tile(t, False), c)[1], 0)

    base = 2 * i * tmd

    def row_copy(r, slot):
        return pltpu.make_async_copy(h_ref.at[pl.ds(r, 1)], rows_hbm.at[pl.ds(dest_ref[base + 2 * r + slot], 1)],
                                     sem.at[slot])

    def start(r, carry):
        row_copy(r, 0).start(priority=0)
        row_copy(r, 1).start(priority=1)
        return carry

    lax.fori_loop(0, tmd, start, 0, unroll=8)
    for slot in range(2):
        pltpu.make_async_copy(h_ref, rows_hbm.at[pl.ds(0, tmd)], sem.at[slot]).wait()


def _dispatch_rows(hf, dest_flat, zflag, n_tiles, tm, tmd):
    t, d = hf.shape
    grid_spec = pltpu.PrefetchScalarGridSpec(
        num_scalar_prefetch=2,
        grid=(t // tmd,),
        in_specs=[pl.BlockSpec((tmd, d), lambda i, dest, zf: (i, 0))],
        out_specs=pl.BlockSpec(memory_space=pl.ANY),
        scratch_shapes=[pltpu.VMEM((tm, d), hf.dtype), pltpu.SemaphoreType.DMA((3,))],
    )
    return pl.pallas_call(
        functools.partial(_dispatch_kernel, tmd=tmd, tm=tm, n_tiles=n_tiles),
        out_shape=jax.ShapeDtypeStruct((n_tiles * tm, d), hf.dtype),
        grid_spec=grid_spec,
        compiler_params=_cparams(("arbitrary",), 32),
        name="moe_dispatch",
    )(dest_flat, zflag, hf)


def _ffn_kernel(te_ref, slot_ref, fetch_ref, e1_ref, na_ref, x_ref, wgu_hbm, wd_hbm, o_ref,
                wgu_land, wd_land, wgu_scr, wd_scr, sem):
    t = pl.program_id(0)

    def weight_copies(e, slot):
        return (pltpu.make_async_copy(wgu_hbm.at[e], wgu_land.at[slot], sem.at[0, slot]),
                pltpu.make_async_copy(wd_hbm.at[e], wd_land.at[slot], sem.at[1, slot]))

    @pl.when(t < na_ref[0])
    def _():
        e = te_ref[t]
        slot = slot_ref[t]

        @pl.when(t == 0)
        def _():
            for cp in weight_copies(e, 0):
                cp.start()

            @pl.when(e1_ref[0] >= 0)
            def _():
                for cp in weight_copies(e1_ref[0], 1):
                    cp.start()

        @pl.when((t == 0) | (e != te_ref[jnp.maximum(t - 1, 0)]))
        def _():
            for cp in weight_copies(e, slot):
                cp.wait()
            wgu_scr[...] = wgu_land[slot].astype(BF16)
            wd_scr[...] = wd_land[slot].astype(BF16)

            @pl.when(fetch_ref[t] >= 0)
            def _():
                for cp in weight_copies(fetch_ref[t], slot):
                    cp.start()

        gu = jnp.dot(x_ref[...].astype(BF16), wgu_scr[...], preferred_element_type=F32)
        gate = gu[:, :D_FF]
        act = (gate * jax.nn.sigmoid(gate)) * gu[:, D_FF:]
        o_ref[...] = jnp.dot(act.astype(BF16), wd_scr[...], preferred_element_type=F32)

    @pl.when(t >= na_ref[0])
    def _():
        o_ref[...] = jnp.zeros(o_ref.shape, o_ref.dtype)


def _expert_ffn(xs, tile_e, slot_tab, fetch_tab, e1, n_active, w_gu, w_down, tm):
    n_tiles = xs.shape[0] // tm
    grid_spec = pltpu.PrefetchScalarGridSpec(
        num_scalar_prefetch=5,
        grid=(n_tiles,),
        in_specs=[pl.BlockSpec((tm, D_MODEL), lambda t, te, sl, fe, e1, na: (jnp.minimum(t, na[0] - 1), 0)),
                  pl.BlockSpec(memory_space=pl.ANY), pl.BlockSpec(memory_space=pl.ANY)],
        out_specs=pl.BlockSpec((tm, D_MODEL), lambda t, te, sl, fe, e1, na: (t, 0)),
        scratch_shapes=[pltpu.VMEM((2, D_MODEL, 2 * D_FF), F32), pltpu.VMEM((2, D_FF, D_MODEL), F32),
                        pltpu.VMEM((D_MODEL, 2 * D_FF), BF16), pltpu.VMEM((D_FF, D_MODEL), BF16),
                        pltpu.SemaphoreType.DMA((2, 2))],
    )
    return pl.pallas_call(
        _ffn_kernel,
        out_shape=jax.ShapeDtypeStruct((n_tiles * tm, D_MODEL), F32),
        grid_spec=grid_spec,
        compiler_params=_cparams(("arbitrary",), 48),
        name="moe_ffn",
    )(tile_e, slot_tab, fetch_tab, e1, n_active, xs, w_gu, w_down)


def _combine_kernel(dest_ref, x_ref, w_ref, ys_hbm, o_ref, buf_a, buf_b, sem, *, tm, base_tok):
    base = 2 * (base_tok + pl.program_id(0) * tm)
    bufs = (buf_a, buf_b)

    def start(r, carry):
        for slot in range(2):
            pltpu.make_async_copy(ys_hbm.at[pl.ds(dest_ref[base + 2 * r + slot], 1)], bufs[slot].at[pl.ds(r, 1)],
                                  sem.at[slot]).start(priority=slot)
        return carry

    lax.fori_loop(0, tm, start, 0, unroll=8)
    for slot in range(2):
        pltpu.make_async_copy(ys_hbm.at[pl.ds(0, tm)], bufs[slot], sem.at[slot]).wait()
    w = w_ref[...]
    o_ref[...] = x_ref[...] + (w[:, 0:1] * buf_a[...] + w[:, 1:2] * buf_b[...])


def _combine(x2d, wts, ys, dest_flat, base_tok, tm):
    t, d = x2d.shape
    base_blk = base_tok // tm
    grid_spec = pltpu.PrefetchScalarGridSpec(
        num_scalar_prefetch=1,
        grid=(t // tm,),
        in_specs=[pl.BlockSpec((tm, d), lambda i, dest: (i, 0)),
                  pl.BlockSpec((tm, LANES), lambda i, dest: (i + base_blk, 0)),
                  pl.BlockSpec(memory_space=pl.ANY)],
        out_specs=pl.BlockSpec((tm, d), lambda i, dest: (i, 0)),
        scratch_shapes=[pltpu.VMEM((tm, d), F32), pltpu.VMEM((tm, d), F32), pltpu.SemaphoreType.DMA((2,))],
    )
    return pl.pallas_call(
        functools.partial(_combine_kernel, tm=tm, base_tok=base_tok),
        out_shape=jax.ShapeDtypeStruct((t, d), F32),
        grid_spec=grid_spec,
        compiler_params=_cparams(("arbitrary",), 32),
        name="moe_combine",
    )(dest_flat, x2d, wts, ys)


def _route_plan(eid, tm, n_tiles):
    flat_e = eid.reshape(-1)
    onehot = (flat_e[:, None] == jnp.arange(N_EXPERTS, dtype=jnp.int32)[None, :]).astype(jnp.int32)
    csum = jnp.cumsum(onehot, axis=0)
    rank = jnp.sum(onehot * csum, axis=1) - 1
    counts = csum[-1]
    padded = ((counts + tm - 1) // tm) * tm
    ends = jnp.cumsum(padded)
    dest = (jnp.sum(onehot * (ends - padded)[None, :], axis=1) + rank).astype(jnp.int32)
    n_active = (ends[-1] // tm).astype(jnp.int32)
    tile = jnp.arange(n_tiles, dtype=jnp.int32)
    active = tile < n_active
    tile_e = jnp.sum((ends[None, :] <= (tile * tm)[:, None]).astype(jnp.int32), axis=1)
    last_e = jnp.sum(jnp.where(tile == n_active - 1, tile_e, 0))
    tile_e = jnp.where(active, tile_e, last_e).astype(jnp.int32)
    has_pad = jnp.any((ends[None, :] == ((tile + 1) * tm)[:, None]) & (padded != counts)[None, :], axis=1)
    zflag = (has_pad | ~active).astype(jnp.int32)
    e_ids = jnp.arange(N_EXPERTS, dtype=jnp.int32)
    present = padded > 0

    def next_present(after):
        return jnp.min(jnp.where(present[None, :] & (e_ids[None, :] > after[:, None]), e_ids[None, :], N_EXPERTS),
                       axis=1)

    run_idx = jnp.sum((present[None, :] & (e_ids[None, :] < tile_e[:, None])).astype(jnp.int32), axis=1)
    slot_tab = (run_idx % 2).astype(jnp.int32)
    n1 = next_present(tile_e)
    n2 = next_present(n1)
    fetch_tab = jnp.where(n2 < N_EXPERTS, n2, -1).astype(jnp.int32)
    e1 = jnp.where(n1[0] < N_EXPERTS, n1[0], -1).astype(jnp.int32).reshape(1)
    return dest, tile_e, slot_tab, fetch_tab, e1, n_active.reshape(1), zflag


def kernel(x_prompt, x_sample, cache_attn_k, cache_attn_v, cache_conv, cache_mem_k, cache_mem_v, page_table, mem_prompt, norm_mix_g, w_in, b_gate, q_norm_g, k_norm_g, lambda_q1, lambda_k1, lambda_q2, lambda_k2, subln_g, conv_w, w_out, norm_mem_g, norm_x_g, w_xq, w_xkv, xq_norm_g, xk_norm_g, w_xo, norm_ffn_g, w_group, b_group, w_expert_router, b_expert_router, w_gate_up, w_down):
    depth = w_in.shape[0]
    bp, sp, _ = x_prompt.shape
    bs, ts, _ = x_sample.shape
    tp, tsn = bp * sp, bs * ts
    slopes = jnp.exp2(-8.0 * jnp.arange(1, N_HEADS + 1, dtype=F32) / N_HEADS) * LOG2E
    xp = x_prompt.reshape(tp, D_MODEL)
    xs = x_sample.reshape(tsn, D_MODEL)
    moe_tm = 256
    moe_tiles = (tp + tsn) * 2 // moe_tm + N_EXPERTS
    outs = [[] for _ in range(8)]
    for l in range(depth):
        lam_init = 0.8 - 0.6 * math.exp(-0.3 * l)
        lam = (jnp.exp(jnp.sum(lambda_q1[l] * lambda_k1[l])) - jnp.exp(jnp.sum(lambda_q2[l] * lambda_k2[l]))
               + lam_init).reshape(1).astype(F32)
        out_scale = 1.0 - lam_init
        w_in_bf = w_in[l].astype(BF16)
        w_out_bf = w_out[l].astype(BF16)
        w_xq_bf = w_xq[l].astype(BF16)
        w_xkv_bf = w_xkv[l].astype(BF16)
        w_xo_bf = w_xo[l].astype(BF16)

        hb = _rmsnorm_cast(xp, norm_mix_g[l], 512)
        q, kf, kb, vf, vb, z, b, ga, gb = _inproj(hb, w_in_bf, q_norm_g[l], k_norm_g[l], b_gate[l], BF16, BF16,
                                                  1024, 256)
        o = _prompt_attn(q.reshape(bp, sp, D_MODEL), kb.reshape(bp, sp, D_MODEL), vb.reshape(bp, sp, D_MODEL),
                         lam, slopes, subln_g[l], out_scale, 512, 4)
        xp = _mix_outproj_prompt(o.reshape(tp, D_MODEL), ga, gb, b, z, conv_w[l], w_out_bf, xp, sp, 256, D_MODEL)
        outs[0].append(kf.reshape(bp, sp, N_HEADS, HEAD_W))
        outs[1].append(vf.reshape(bp, sp, N_HEADS, V_DIM))
        outs[2].append(z.reshape(bp, sp, D_MODEL)[:, sp - 2:, :])

        hb = _rmsnorm_cast(xs, norm_mix_g[l], 512)
        q, kf, kb, vf, vb, z, b, ga, gb = _inproj(hb, w_in_bf, q_norm_g[l], k_norm_g[l], b_gate[l], F32, F32,
                                                  1024, 256)
        shp = (bs, ts, D_MODEL)
        o = _sample_attn(q.reshape(shp), kf.reshape(shp), vf.reshape(shp), cache_attn_k[l], cache_attn_v[l],
                         page_table, lam, slopes, subln_g[l], out_scale, 4, 4)
        z3 = z.reshape(shp)
        merged = _mix_sample(o, ga.reshape(shp), gb.reshape(shp), b.reshape(shp), z3, cache_conv[l], conv_w[l], 32)
        xs = _mm_res(merged.reshape(tsn, D_MODEL), w_out_bf, xs, 512, 1024)
        outs[5].append(kf.reshape(bs, ts, N_HEADS, HEAD_W))
        outs[6].append(vf.reshape(bs, ts, N_HEADS, V_DIM))
        outs[7].append(z3[:, ts - 2:, :])

        mem_nb = _rmsnorm_cast(mem_prompt.reshape(bp * N_MEM, D_MODEL), norm_mem_g[l], 512)
        mkf, mkb, mvf, mvb = _mem_kv(mem_nb, w_xkv_bf, xk_norm_g[l])
        xp = _xattn_prompt(xp.reshape(bp, sp, D_MODEL), norm_x_g[l], w_xq_bf, xq_norm_g[l],
                           mkb.reshape(bp, N_MEM, X_WIDTH), mvb.reshape(bp, N_MEM, X_WIDTH), w_xo_bf,
                           512).reshape(tp, D_MODEL)
        xs = _xattn_sample(xs, norm_x_g[l], w_xq_bf, xq_norm_g[l], cache_mem_k[l], cache_mem_v[l], w_xo_bf, ts, 8)
        outs[3].append(mkf.reshape(bp, N_MEM, X_HEADS, X_HEAD_DIM))
        outs[4].append(mvf.reshape(bp, N_MEM, X_HEADS, X_HEAD_DIM))

        w_r = jnp.zeros((D_MODEL, LANES), F32).at[:, :N_GROUPS].set(w_group[l])
        w_r = w_r.at[:, N_GROUPS:N_GROUPS + N_EXPERTS].set(w_expert_router[l])
        b_r = jnp.zeros((1, LANES), F32).at[0, :N_GROUPS].set(b_group[l])
        b_r = b_r.at[0, N_GROUPS:N_GROUPS + N_EXPERTS].set(b_expert_router[l])
        hf, eid, wts = _router(xp, xs, norm_ffn_g[l], w_r, b_r, 512)
        dest, tile_e, slot_tab, fetch_tab, e1, n_active, zflag = _route_plan(eid[:, :2], moe_tm, moe_tiles)
        rows = _dispatch_rows(hf, dest, zflag, moe_tiles, moe_tm, 256)
        ys = _expert_ffn(rows, tile_e, slot_tab, fetch_tab, e1, n_active, w_gate_up[l], w_down[l], moe_tm)
        xp = _combine(xp, wts, ys, dest, 0, moe_tm)
        xs = _combine(xs, wts, ys, dest, tp, moe_tm)
    stack = lambda i: jnp.stack(outs[i])
    return (xp.reshape(bp, sp, D_MODEL), xs.reshape(bs, ts, D_MODEL), stack(0), stack(1), stack(2), stack(3),
            stack(4), stack(5), stack(6), stack(7))
```
